```python
import math
import jax, jax.numpy as jnp
from jax import lax
import numpy as np

D_MODEL = 1024
BATCH = 2
SEQ = 8192
DEPTH = 1

POOL_WIDTH = D_MODEL // 2
POOL_WINDOWS = (2, 4, 8, 16)
N_POOL_GROUPS = len(POOL_WINDOWS)
POOL_GROUP = POOL_WIDTH // N_POOL_GROUPS
ATTN_WIDTH = D_MODEL // 2
N_HEADS = 4
HEAD_DIM = ATTN_WIDTH // (2 * N_HEADS)
V_DIM = 2 * HEAD_DIM
Q_BLOCK = 128
N_BRANCHES = 2
D_FF = 4 * D_MODEL
IN_WIDTH = POOL_WIDTH + 2 * ATTN_WIDTH + N_HEADS * V_DIM + N_BRANCHES * D_MODEL
EPS = 1e-6

kernel_name = "hybrid_pool_diffattn_gated_block"


def _alibi_slopes(n_heads):
    return jnp.asarray(np.array([2.0 ** (-8.0 * (h + 1) / n_heads) for h in range(n_heads)], dtype=np.float32))


def _lambda_init(layer_idx):
    return 0.8 - 0.6 * math.exp(-0.3 * layer_idx)


def rmsnorm(x, g):
    xf = x.astype(jnp.float32)
    y = xf * lax.rsqrt(jnp.mean(xf * xf, axis=-1, keepdims=True) + EPS)
    return (y * g.astype(jnp.float32)).astype(x.dtype)


def causal_multiscale_pool(u, pool_w, pool_scale):
    B, S, _ = u.shape
    ug = u.reshape(B, S, N_POOL_GROUPS, POOL_GROUP).astype(jnp.float32)
    cs = jnp.cumsum(ug, axis=1)
    cs = jnp.concatenate([jnp.zeros((B, 1, N_POOL_GROUPS, POOL_GROUP), jnp.float32), cs], axis=1)
    t = jnp.arange(S)
    means = []
    for g, w in enumerate(POOL_WINDOWS):
        lo = jnp.maximum(t + 1 - w, 0)
        cnt = (t + 1 - lo).astype(jnp.float32)
        means.append((cs[:, 1:, g] - cs[:, lo, g]) / cnt[None, :, None])
    pooled = jnp.stack(means, axis=2) - ug
    mixed = jnp.einsum('bsgc,gcd->bsgd', pooled.astype(u.dtype), pool_w)
    return mixed.reshape(B, S, POOL_WIDTH) * pool_scale


def differential_attention(q, k, v, lam, lambda_init, subln_g):
    B, S = q.shape[0], q.shape[1]
    nb = S // Q_BLOCK
    scale = 1.0 / math.sqrt(HEAD_DIM)
    slopes = _alibi_slopes(N_HEADS)
    kpos = jnp.arange(S)
    qb = q.reshape(B, nb, Q_BLOCK, N_HEADS, 2, HEAD_DIM).transpose(1, 0, 2, 3, 4, 5)

    def one_block(args):
        qi, bi = args
        qpos = bi * Q_BLOCK + jnp.arange(Q_BLOCK)
        s = jnp.einsum('bqhcd,bkhcd->bhcqk', qi, k, preferred_element_type=jnp.float32) * scale
        dist = (qpos[:, None] - kpos[None, :]).astype(jnp.float32)
        s = s - slopes[:, None, None, None] * dist[None, None]
        causal = kpos[None, :] <= qpos[:, None]
        s = jnp.where(causal[None, None, None], s, -jnp.inf)
        p = jax.nn.softmax(s, axis=-1)
        a = p[:, :, 0] - lam * p[:, :, 1]
        return jnp.einsum('bhqk,bkhv->bqhv', a.astype(v.dtype), v)

    o = lax.map(one_block, (qb, jnp.arange(nb)))
    o = o.transpose(1, 0, 2, 3, 4).reshape(B, S, N_HEADS, V_DIM)
    o = rmsnorm(o, subln_g) * (1.0 - lambda_init)
    return o.reshape(B, S, N_HEADS * V_DIM)


def setup_inputs(seed: int = 0) -> dict:
    key = jax.random.key(seed)
    ks = jax.random.split(key, 20)
    f32 = jnp.float32
    nrm = lambda k, shape, s: (jax.random.normal(k, shape, f32) * s).astype(f32)
    return {
        "x": nrm(ks[0], (BATCH, SEQ, D_MODEL), 1.0),
        "norm1_g": 1.0 + nrm(ks[1], (DEPTH, D_MODEL), 0.05),
        "w_in": nrm(ks[2], (DEPTH, D_MODEL, IN_WIDTH), D_MODEL ** -0.5),
        "gate_b": nrm(ks[3], (DEPTH, N_BRANCHES * D_MODEL), 0.02),
        "pool_w": nrm(ks[4], (DEPTH, N_POOL_GROUPS, POOL_GROUP, POOL_GROUP), POOL_GROUP ** -0.5),
        "pool_scale": 1.0 + nrm(ks[5], (DEPTH, POOL_WIDTH), 0.05),
        "q_norm_g": 1.0 + nrm(ks[6], (DEPTH, HEAD_DIM), 0.05),
        "k_norm_g": 1.0 + nrm(ks[7], (DEPTH, HEAD_DIM), 0.05),
        "lambda_q1": nrm(ks[8], (DEPTH, HEAD_DIM), 0.1),
        "lambda_k1": nrm(ks[9], (DEPTH, HEAD_DIM), 0.1),
        "lambda_q2": nrm(ks[10], (DEPTH, HEAD_DIM), 0.1),
        "lambda_k2": nrm(ks[11], (DEPTH, HEAD_DIM), 0.1),
        "subln_g": 1.0 + nrm(ks[12], (DEPTH, V_DIM), 0.05),
        "w_branch_a": nrm(ks[13], (DEPTH, POOL_WIDTH, D_MODEL), POOL_WIDTH ** -0.5),
        "w_branch_b": nrm(ks[14], (DEPTH, N_HEADS * V_DIM, D_MODEL), (N_HEADS * V_DIM) ** -0.5),
        "w_out": nrm(ks[15], (DEPTH, D_MODEL, D_MODEL), D_MODEL ** -0.5),
        "norm2_g": 1.0 + nrm(ks[16], (DEPTH, D_MODEL), 0.05),
        "w_ff1": nrm(ks[17], (DEPTH, D_MODEL, D_FF), D_MODEL ** -0.5),
        "w_ff2": nrm(ks[18], (DEPTH, D_FF, D_MODEL), D_FF ** -0.5),
    }


def reference(x, norm1_g, w_in, gate_b, pool_w, pool_scale, q_norm_g, k_norm_g,
              lambda_q1, lambda_k1, lambda_q2, lambda_k2, subln_g,
              w_branch_a, w_branch_b, w_out, norm2_g, w_ff1, w_ff2):
    B, S, D = x.shape
    splits = np.cumsum([POOL_WIDTH, ATTN_WIDTH, ATTN_WIDTH, N_HEADS * V_DIM]).tolist()
    for l in range(DEPTH):
        lambda_init = _lambda_init(l)
        h = rmsnorm(x, norm1_g[l])
        proj = h @ w_in[l]
        u, q, k, v, gl = jnp.split(proj, splits, axis=-1)
        ya = causal_multiscale_pool(u, pool_w[l], pool_scale[l]) @ w_branch_a[l]
        q = rmsnorm(q.reshape(B, S, N_HEADS, 2, HEAD_DIM), q_norm_g[l])
        k = rmsnorm(k.reshape(B, S, N_HEADS, 2, HEAD_DIM), k_norm_g[l])
        v = v.reshape(B, S, N_HEADS, V_DIM)
        lq1k1 = jnp.sum(lambda_q1[l].astype(jnp.float32) * lambda_k1[l].astype(jnp.float32))
        lq2k2 = jnp.sum(lambda_q2[l].astype(jnp.float32) * lambda_k2[l].astype(jnp.float32))
        lam = jnp.exp(lq1k1) - jnp.exp(lq2k2) + lambda_init
        yb = differential_attention(q, k, v, lam, lambda_init, subln_g[l]) @ w_branch_b[l]
        g = jax.nn.sigmoid((gl + gate_b[l]).astype(jnp.float32)).astype(x.dtype)
        g = g.reshape(B, S, N_BRANCHES, D)
        merged = g[:, :, 0] * ya + g[:, :, 1] * yb
        x = x + merged @ w_out[l]
        h2 = rmsnorm(x, norm2_g[l])
        x = x + jnp.square(jax.nn.relu(h2 @ w_ff1[l])) @ w_ff2[l]
    return x
```

```python
import functools
import math

import numpy as np
import jax
import jax.numpy as jnp
from jax import lax
from jax.experimental import pallas as pl
from jax.experimental.pallas import tpu as pltpu

F32 = jnp.float32
BF16 = jnp.bfloat16
EPS = 1e-6

POOL_WINDOWS = (2, 4, 8, 16)
POOL_GROUP = 128
HALO = max(POOL_WINDOWS)
N_HEADS = 4
HEAD_DIM = 64
V_DIM = 2 * HEAD_DIM
LANES = 128
MXU_DIM = 256
LAMBDA_INIT = 0.8 - 0.6 * math.exp(-0.3 * 0)
NEG_BIG = -1e30

ROW_TILE = 512
ATTN_TILE = 256
VMEM_LIMIT = 56 * 1024 * 1024


def _dot(a, b):
    return jnp.dot(a, b, preferred_element_type=F32)


def _sigmoid(x):
    return 1.0 / (1.0 + jnp.exp(-x))


def _inproj_kernel(x_ref, g1_ref, w_ref, gb_ref, gq_ref, gk_ref, bd_ref,
                   u_ref, q_ref, k_ref, v_ref, g_ref, *, widths, scale):
    pw, aw, vw, gw = widths
    x = x_ref[...]
    ms = jnp.mean(x * x, axis=-1, keepdims=True)
    h = (x * lax.rsqrt(ms + EPS) * g1_ref[...]).astype(BF16)

    def proj(lo, width):
        return _dot(h, w_ref[:, lo:lo + width])

    bd = bd_ref[...]

    def group_rms(y):
        y2 = y * y
        hi = y2.astype(BF16)
        lo = (y2 - hi.astype(F32)).astype(BF16)
        parts = []
        for c in range(y.shape[1] // MXU_DIM):
            sl = slice(c * MXU_DIM, (c + 1) * MXU_DIM)
            parts.append(_dot(hi[:, sl], bd) + _dot(lo[:, sl], bd))
        ss = jnp.concatenate(parts, axis=1)
        return y * lax.rsqrt(ss * (1.0 / HEAD_DIM) + EPS)

    u_ref[...] = proj(0, pw)
    q_ref[...] = (group_rms(proj(pw, aw)) * gq_ref[...] * scale).astype(BF16)
    k_ref[...] = (group_rms(proj(pw + aw, aw)) * gk_ref[...]).astype(BF16)
    v_ref[...] = proj(pw + 2 * aw, vw).astype(BF16)
    g0 = pw + 2 * aw + vw
    chunk = 512
    for c in range(gw // chunk):
        gl = proj(g0 + c * chunk, chunk) + gb_ref[:, c * chunk:(c + 1) * chunk]
        g_ref[:, c * chunk:(c + 1) * chunk] = _sigmoid(gl).astype(BF16)


def _attn_kernel(slope_ref, q_ref, k_ref, v_ref, kb_ref, lq1_ref, lk1_ref, lq2_ref, lk2_ref,
                 sg_ref, o_ref, acc_ref, *, tile):
    h = pl.program_id(1)
    qi = pl.program_id(2)
    slope = slope_ref[h]

    q = q_ref[0]
    lane = lax.broadcasted_iota(jnp.int32, q.shape, 1)
    zero = jnp.zeros_like(q)
    ones_cols = jnp.where(lane < 2, 1.0, 0.0).astype(BF16)
    qa = (jnp.concatenate([jnp.where(lane < HEAD_DIM, q, zero), ones_cols], axis=1),
          jnp.concatenate([jnp.where(lane >= HEAD_DIM, q, zero), ones_cols], axis=1))
    kbias = kb_ref[0]

    row = lax.broadcasted_iota(jnp.int32, (tile, tile), 0)
    col = lax.broadcasted_iota(jnp.int32, (tile, tile), 1)
    future = row > col

    def block(j, masked):
        start = pl.multiple_of(j * tile, tile)
        ka = jnp.concatenate([k_ref[0, pl.ds(start, tile), :], kbias], axis=1)
        vb = v_ref[0, pl.ds(start, tile), :]
        out = []
        for c in range(2):
            s = lax.dot_general(ka, qa[c], (((1,), (1,)), ((), ())), preferred_element_type=F32)
            if masked:
                s = jnp.where(future, NEG_BIG, s)
            out.append(s)
        return out, vb

    def pv(vb, p):
        return lax.dot_general(vb, p.astype(BF16), (((0,), (0,)), ((), ())),
                               preferred_element_type=F32)

    ss, vb = block(qi, True)
    ms, ls = [], []
    for c in range(2):
        m = jnp.max(ss[c], axis=0, keepdims=True)
        p = jnp.exp(ss[c] - m)
        ms.append(m)
        ls.append(jnp.sum(p, axis=0, keepdims=True))
        acc_ref[c] = pv(vb, p)

    def body(t, carry):
        m0, m1, l0, l1 = carry
        j = qi - 1 - t
        off = slope * ((j - qi) * tile).astype(F32)
        ss, vb = block(j, False)
        new = []
        for c, (m_old, l_old) in enumerate(((m0, l0), (m1, l1))):
            mb = jnp.max(ss[c], axis=0, keepdims=True) + off
            m_new = jnp.maximum(m_old, mb)
            p = jnp.exp(ss[c] - (m_new - off))
            alpha = jnp.exp(m_old - m_new)
            l_new = alpha * l_old + jnp.sum(p, axis=0, keepdims=True)
            acc_ref[c] = alpha * acc_ref[c] + pv(vb, p)
            new.append((m_new, l_new))
        return new[0][0], new[1][0], new[0][1], new[1][1]

    m0, m1, l0, l1 = lax.fori_loop(0, qi, body, (ms[0], ms[1], ls[0], ls[1]))

    lam = (jnp.exp(jnp.sum(lq1_ref[...] * lk1_ref[...], axis=-1, keepdims=True))
           - jnp.exp(jnp.sum(lq2_ref[...] * lk2_ref[...], axis=-1, keepdims=True))
           + LAMBDA_INIT)
    o = acc_ref[0] / l0 - lam * (acc_ref[1] / l1)
    ot = o.T
    msq = jnp.mean(ot * ot, axis=-1, keepdims=True)
    y = ot * lax.rsqrt(msq + EPS) * sg_ref[...] * (1.0 - LAMBDA_INIT)
    o_ref[0] = y.astype(BF16)


def _merge_kernel(x_ref, u_ref, halo_ref, a_ref, g_ref, pw_ref, ps_ref, wa_ref, wb_ref, wo_ref,
                  o_ref, ext_ref, *, seq, tm):
    i = pl.program_id(0)
    t0 = (i * tm) % seq
    halo = halo_ref[...]
    ext_ref[0:HALO, :] = jnp.where(t0 == 0, jnp.zeros_like(halo), halo)
    ext_ref[HALO:, :] = u_ref[...]

    t = t0 + lax.broadcasted_iota(jnp.int32, (tm, 1), 0)
    mixed = []
    for g, w in enumerate(POOL_WINDOWS):
        lanes = slice(g * POOL_GROUP, (g + 1) * POOL_GROUP)
        tot = ext_ref[HALO:, lanes]
        for d in range(1, w):
            tot = tot + ext_ref[HALO - d:HALO - d + tm, lanes]
        cnt = jnp.minimum(t + 1, w).astype(F32)
        pooled = tot / cnt - ext_ref[HALO:, lanes]
        mixed.append(_dot(pooled.astype(BF16), pw_ref[g]))
    mixed = jnp.concatenate(mixed, axis=1) * ps_ref[...]
    ya = _dot(mixed.astype(BF16), wa_ref[...])
    yb = _dot(a_ref[...], wb_ref[...])
    d = ya.shape[1]
    merged = g_ref[:, 0:d].astype(F32) * ya + g_ref[:, d:2 * d].astype(F32) * yb
    o_ref[...] = x_ref[...] + _dot(merged.astype(BF16), wo_ref[...])


def _ffn_kernel(x_ref, g2_ref, w1_ref, w2_ref, o_ref, *, chunk):
    x = x_ref[...]
    ms = jnp.mean(x * x, axis=-1, keepdims=True)
    h = (x * lax.rsqrt(ms + EPS) * g2_ref[...]).astype(BF16)
    acc = x
    for c in range(w1_ref.shape[1] // chunk):
        a = jnp.maximum(_dot(h, w1_ref[:, c * chunk:(c + 1) * chunk]), 0.0)
        acc = acc + _dot((a * a).astype(BF16), w2_ref[c * chunk:(c + 1) * chunk, :])
    o_ref[...] = acc


def _const_spec(shape):
    return pl.BlockSpec(shape, lambda *_: (0,) * len(shape))


def _params(n_axes):
    return pltpu.CompilerParams(dimension_semantics=("arbitrary",) * n_axes,
                                vmem_limit_bytes=VMEM_LIMIT)


def kernel(x, norm1_g, w_in, gate_b, pool_w, pool_scale, q_norm_g, k_norm_g, lambda_q1, lambda_k1,
           lambda_q2, lambda_k2, subln_g, w_branch_a, w_branch_b, w_out, norm2_g, w_ff1, w_ff2):
    B, S, D = x.shape
    depth = norm1_g.shape[0]
    assert depth == 1
    M = B * S
    pw = pool_scale.shape[1]
    aw = 2 * N_HEADS * HEAD_DIM
    vw = N_HEADS * V_DIM
    gw = 2 * D
    d_ff = w_ff1.shape[2]
    tm = ROW_TILE
    ta = ATTN_TILE
    assert S % tm == 0 and S % ta == 0 and tm % HALO == 0
    assert pw == len(POOL_WINDOWS) * POOL_GROUP and w_in.shape[2] == pw + 2 * aw + vw + gw

    x2 = x.reshape(M, D)
    row = lambda width: pl.BlockSpec((tm, width), lambda i: (i, 0))

    bd = jnp.asarray(np.kron(np.eye(MXU_DIM // HEAD_DIM), np.ones((HEAD_DIM, HEAD_DIM))), BF16)
    reps = aw // HEAD_DIM
    u, q, k, v, gates = pl.pallas_call(
        functools.partial(_inproj_kernel, widths=(pw, aw, vw, gw), scale=1.0 / math.sqrt(HEAD_DIM)),
        grid=(M // tm,),
        in_specs=[row(D), _const_spec((1, D)), _const_spec(w_in.shape[1:]), _const_spec((1, gw)),
                  _const_spec((1, aw)), _const_spec((1, aw)), _const_spec((MXU_DIM, MXU_DIM))],
        out_specs=[row(pw), row(aw), row(aw), row(vw), row(gw)],
        out_shape=[jax.ShapeDtypeStruct((M, pw), F32), jax.ShapeDtypeStruct((M, aw), BF16),
                   jax.ShapeDtypeStruct((M, aw), BF16), jax.ShapeDtypeStruct((M, vw), BF16),
                   jax.ShapeDtypeStruct((M, gw), BF16)],
        compiler_params=_params(1),
        name="inproj",
    )(x2, norm1_g, w_in[0].astype(BF16), gate_b, jnp.tile(q_norm_g, (1, reps)),
      jnp.tile(k_norm_g, (1, reps)), bd)

    slopes = np.array([2.0 ** (-8.0 * (h + 1) / N_HEADS) for h in range(N_HEADS)], np.float32)
    kpos = slopes[:, None] * np.arange(ta, dtype=np.float32)[None, :]
    kpos_hi = jnp.asarray(kpos, BF16)
    kpos_lo = (jnp.asarray(kpos) - kpos_hi.astype(F32)).astype(BF16)
    kbias = jnp.zeros((N_HEADS, ta, LANES), BF16).at[:, :, 0].set(kpos_hi).at[:, :, 1].set(kpos_lo)

    head_tile = pl.BlockSpec((1, ta, LANES), lambda b, h, i: (b, i, h))
    head_seq = pl.BlockSpec((1, S, LANES), lambda b, h, i: (b, 0, h))
    attn = pl.pallas_call(
        functools.partial(_attn_kernel, tile=ta),
        grid=(B, N_HEADS, S // ta),
        in_specs=[pl.BlockSpec(memory_space=pltpu.SMEM), head_tile, head_seq, head_seq,
                  pl.BlockSpec((1, ta, LANES), lambda b, h, i: (h, 0, 0)),
                  _const_spec((1, HEAD_DIM)), _const_spec((1, HEAD_DIM)),
                  _const_spec((1, HEAD_DIM)), _const_spec((1, HEAD_DIM)), _const_spec((1, V_DIM))],
        out_specs=head_tile,
        out_shape=jax.ShapeDtypeStruct((B, S, vw), BF16),
        scratch_shapes=[pltpu.VMEM((2, V_DIM, ta), F32)],
        compiler_params=_params(3),
        name="diffattn",
    )(jnp.asarray(slopes), q.reshape(B, S, aw), k.reshape(B, S, aw), v.reshape(B, S, vw), kbias,
      lambda_q1, lambda_k1, lambda_q2, lambda_k2, subln_g)

    x_mid = pl.pallas_call(
        functools.partial(_merge_kernel, seq=S, tm=tm),
        grid=(M // tm,),
        in_specs=[row(D), row(pw),
                  pl.BlockSpec((HALO, pw), lambda i: (jnp.maximum(i * (tm // HALO) - 1, 0), 0)),
                  row(vw), row(gw), _const_spec(pool_w.shape[1:]), _const_spec((1, pw)),
                  _const_spec((pw, D)), _const_spec((vw, D)), _const_spec((D, D))],
        out_specs=row(D),
        out_shape=jax.ShapeDtypeStruct((M, D), F32),
        scratch_shapes=[pltpu.VMEM((HALO + tm, pw), F32)],
        compiler_params=_params(1),
        name="merge",
    )(x2, u, u, attn.reshape(M, vw), gates, pool_w[0].astype(BF16), pool_scale,
      w_branch_a[0].astype(BF16), w_branch_b[0].astype(BF16), w_out[0].astype(BF16))

    out = pl.pallas_call(
        functools.partial(_ffn_kernel, chunk=1024),
        grid=(M // tm,),
        in_specs=[row(D), _const_spec((1, D)), _const_spec((D, d_ff)), _const_spec((d_ff, D))],
        out_specs=row(D),
        out_shape=jax.ShapeDtypeStruct((M, D), F32),
        compiler_params=_params(1),
        name="ffn",
    )(x_mid, norm2_g, w_ff1[0].astype(BF16), w_ff2[0].astype(BF16))
    return out.reshape(B, S, D)
```

```python
import functools
import math

import numpy as np
import jax
import jax.numpy as jnp
from jax import lax
from jax.experimental import pallas as pl
from jax.experimental.pallas import tpu as pltpu

F32 = jnp.float32
BF16 = jnp.bfloat16
EPS = 1e-6

POOL_WINDOWS = (2, 4, 8, 16)
POOL_GROUP = 128
HALO = max(POOL_WINDOWS)
N_HEADS = 4
HEAD_DIM = 64
V_DIM = 2 * HEAD_DIM
LANES = 128
MXU_DIM = 256
LAMBDA_INIT = 0.8 - 0.6 * math.exp(-0.3 * 0)
NEG_BIG = -1e30
LOG2E = math.log2(math.e)
BIAS_TERMS = 3

ROW_TILE = 512
ATTN_TILE = 512
VMEM_LIMIT = 56 * 1024 * 1024


def _dot(a, b):
    return jnp.dot(a, b, preferred_element_type=F32)


def _sigmoid(x):
    return 1.0 / (1.0 + jnp.exp(-x))


def _inproj_kernel(x_ref, g1_ref, w_ref, gb_ref, gq_ref, gk_ref, bd_ref,
                   u_ref, q_ref, k_ref, v_ref, g_ref, *, widths, scale):
    pw, aw, vw, gw = widths
    x = x_ref[...]
    ms = jnp.mean(x * x, axis=-1, keepdims=True)
    h = (x * lax.rsqrt(ms + EPS) * g1_ref[...]).astype(BF16)

    def proj(lo, width):
        return _dot(h, w_ref[:, lo:lo + width])

    bd = bd_ref[...]

    def group_rms(y):
        y2 = y * y
        hi = y2.astype(BF16)
        lo = (y2 - hi.astype(F32)).astype(BF16)
        parts = []
        for c in range(y.shape[1] // MXU_DIM):
            sl = slice(c * MXU_DIM, (c + 1) * MXU_DIM)
            parts.append(_dot(hi[:, sl], bd) + _dot(lo[:, sl], bd))
        ss = jnp.concatenate(parts, axis=1)
        return y * lax.rsqrt(ss * (1.0 / HEAD_DIM) + EPS)

    u_ref[...] = proj(0, pw)
    q_ref[...] = (group_rms(proj(pw, aw)) * gq_ref[...] * scale).astype(BF16)
    k_ref[...] = (group_rms(proj(pw + aw, aw)) * gk_ref[...]).astype(BF16)
    v_ref[...] = proj(pw + 2 * aw, vw).astype(BF16)
    g0 = pw + 2 * aw + vw
    chunk = 512
    for c in range(gw // chunk):
        gl = proj(g0 + c * chunk, chunk) + gb_ref[:, c * chunk:(c + 1) * chunk]
        g_ref[:, c * chunk:(c + 1) * chunk] = _sigmoid(gl).astype(BF16)


def _attn_kernel(slope_ref, q_ref, k_ref, v_ref, kb_ref, lq1_ref, lk1_ref, lq2_ref, lk2_ref,
                 sg_ref, o_ref, acc_ref, *, tile):
    h = pl.program_id(1)
    qi = pl.program_id(2)
    slope = slope_ref[h]

    q = q_ref[0]
    lane = lax.broadcasted_iota(jnp.int32, q.shape, 1)
    zero = jnp.zeros_like(q)
    ones_cols = jnp.where(lane < BIAS_TERMS, 1.0, 0.0).astype(BF16)
    qa = (jnp.concatenate([jnp.where(lane < HEAD_DIM, q, zero), ones_cols], axis=1),
          jnp.concatenate([jnp.where(lane >= HEAD_DIM, q, zero), ones_cols], axis=1))
    kbias = kb_ref[0]

    row = lax.broadcasted_iota(jnp.int32, (tile, tile), 0)
    col = lax.broadcasted_iota(jnp.int32, (tile, tile), 1)
    future = row > col

    def block(j, masked):
        start = pl.multiple_of(j * tile, tile)
        ka = jnp.concatenate([k_ref[0, pl.ds(start, tile), :], kbias], axis=1)
        vb = v_ref[0, pl.ds(start, tile), :]
        out = []
        for c in range(2):
            s = lax.dot_general(ka, qa[c], (((1,), (1,)), ((), ())), preferred_element_type=F32)
            if masked:
                s = jnp.where(future, NEG_BIG, s)
            out.append(s)
        return out, vb

    def pv(vb, p):
        return lax.dot_general(vb, p.astype(BF16), (((0,), (0,)), ((), ())),
                               preferred_element_type=F32)

    ss, vb = block(qi, True)
    ms, ls = [], []
    for c in range(2):
        m = jnp.max(ss[c], axis=0, keepdims=True)
        p = jnp.exp2(ss[c] - m)
        ms.append(m)
        ls.append(jnp.sum(p, axis=0, keepdims=True))
        acc_ref[c] = pv(vb, p)

    def body(t, carry):
        m0, m1, l0, l1 = carry
        j = qi - 1 - t
        off = slope * ((j - qi) * tile).astype(F32)
        ss, vb = block(j, False)
        new = []
        for c, (m_old, l_old) in enumerate(((m0, l0), (m1, l1))):
            mb = jnp.max(ss[c], axis=0, keepdims=True) + off
            m_new = jnp.maximum(m_old, mb)
            p = jnp.exp2(ss[c] - (m_new - off))
            alpha = jnp.exp2(m_old - m_new)
            l_new = alpha * l_old + jnp.sum(p, axis=0, keepdims=True)
            acc_ref[c] = alpha * acc_ref[c] + pv(vb, p)
            new.append((m_new, l_new))
        return new[0][0], new[1][0], new[0][1], new[1][1]

    m0, m1, l0, l1 = lax.fori_loop(0, qi, body, (ms[0], ms[1], ls[0], ls[1]))

    lam = (jnp.exp(jnp.sum(lq1_ref[...] * lk1_ref[...], axis=-1, keepdims=True))
           - jnp.exp(jnp.sum(lq2_ref[...] * lk2_ref[...], axis=-1, keepdims=True))
           + LAMBDA_INIT)
    o = acc_ref[0] / l0 - lam * (acc_ref[1] / l1)
    ot = o.T
    msq = jnp.mean(ot * ot, axis=-1, keepdims=True)
    y = ot * lax.rsqrt(msq + EPS) * sg_ref[...] * (1.0 - LAMBDA_INIT)
    o_ref[0] = y.astype(BF16)


def _merge_kernel(x_ref, u_ref, halo_ref, a_ref, g_ref, pw_ref, ps_ref, wa_ref, wb_ref, wo_ref,
                  o_ref, ext_ref, *, seq, tm):
    i = pl.program_id(0)
    t0 = (i * tm) % seq
    halo = halo_ref[...]
    ext_ref[0:HALO, :] = jnp.where(t0 == 0, jnp.zeros_like(halo), halo)
    ext_ref[HALO:, :] = u_ref[...]

    t = t0 + lax.broadcasted_iota(jnp.int32, (tm, 1), 0)
    mixed = []
    for g, w in enumerate(POOL_WINDOWS):
        lanes = slice(g * POOL_GROUP, (g + 1) * POOL_GROUP)
        tot = ext_ref[HALO:, lanes]
        for d in range(1, w):
            tot = tot + ext_ref[HALO - d:HALO - d + tm, lanes]
        cnt = jnp.minimum(t + 1, w).astype(F32)
        pooled = tot / cnt - ext_ref[HALO:, lanes]
        mixed.append(_dot(pooled.astype(BF16), pw_ref[g]))
    mixed = jnp.concatenate(mixed, axis=1) * ps_ref[...]
    ya = _dot(mixed.astype(BF16), wa_ref[...])
    yb = _dot(a_ref[...], wb_ref[...])
    d = ya.shape[1]
    merged = g_ref[:, 0:d].astype(F32) * ya + g_ref[:, d:2 * d].astype(F32) * yb
    o_ref[...] = x_ref[...] + _dot(merged.astype(BF16), wo_ref[...])


def _ffn_kernel(x_ref, g2_ref, w1_ref, w2_ref, o_ref, *, chunk):
    x = x_ref[...]
    ms = jnp.mean(x * x, axis=-1, keepdims=True)
    h = (x * lax.rsqrt(ms + EPS) * g2_ref[...]).astype(BF16)
    acc = x
    for c in range(w1_ref.shape[1] // chunk):
        a = jnp.maximum(_dot(h, w1_ref[:, c * chunk:(c + 1) * chunk]), 0.0)
        acc = acc + _dot((a * a).astype(BF16), w2_ref[c * chunk:(c + 1) * chunk, :])
    o_ref[...] = acc


def _const_spec(shape):
    return pl.BlockSpec(shape, lambda *_: (0,) * len(shape))


def _params(n_axes):
    return pltpu.CompilerParams(dimension_semantics=("arbitrary",) * n_axes,
                                vmem_limit_bytes=VMEM_LIMIT)


def kernel(x, norm1_g, w_in, gate_b, pool_w, pool_scale, q_norm_g, k_norm_g, lambda_q1, lambda_k1,
           lambda_q2, lambda_k2, subln_g, w_branch_a, w_branch_b, w_out, norm2_g, w_ff1, w_ff2):
    B, S, D = x.shape
    depth = norm1_g.shape[0]
    assert depth == 1
    M = B * S
    pw = pool_scale.shape[1]
    aw = 2 * N_HEADS * HEAD_DIM
    vw = N_HEADS * V_DIM
    gw = 2 * D
    d_ff = w_ff1.shape[2]
    tm = ROW_TILE
    ta = ATTN_TILE
    assert S % tm == 0 and S % ta == 0 and tm % HALO == 0
    assert pw == len(POOL_WINDOWS) * POOL_GROUP and w_in.shape[2] == pw + 2 * aw + vw + gw

    x2 = x.reshape(M, D)
    row = lambda width: pl.BlockSpec((tm, width), lambda i: (i, 0))

    bd = jnp.asarray(np.kron(np.eye(MXU_DIM // HEAD_DIM), np.ones((HEAD_DIM, HEAD_DIM))), BF16)
    reps = aw // HEAD_DIM
    u, q, k, v, gates = pl.pallas_call(
        functools.partial(_inproj_kernel, widths=(pw, aw, vw, gw), scale=LOG2E / math.sqrt(HEAD_DIM)),
        grid=(M // tm,),
        in_specs=[row(D), _const_spec((1, D)), _const_spec(w_in.shape[1:]), _const_spec((1, gw)),
                  _const_spec((1, aw)), _const_spec((1, aw)), _const_spec((MXU_DIM, MXU_DIM))],
        out_specs=[row(pw), row(aw), row(aw), row(vw), row(gw)],
        out_shape=[jax.ShapeDtypeStruct((M, pw), F32), jax.ShapeDtypeStruct((M, aw), BF16),
                   jax.ShapeDtypeStruct((M, aw), BF16), jax.ShapeDtypeStruct((M, vw), BF16),
                   jax.ShapeDtypeStruct((M, gw), BF16)],
        compiler_params=_params(1),
        name="inproj",
    )(x2, norm1_g, w_in[0].astype(BF16), gate_b, jnp.tile(q_norm_g, (1, reps)),
      jnp.tile(k_norm_g, (1, reps)), bd)

    slopes = np.array([2.0 ** (-8.0 * (h + 1) / N_HEADS) for h in range(N_HEADS)], np.float64)
    slopes = (slopes * LOG2E).astype(np.float32)
    rest = jnp.asarray(slopes[:, None] * np.arange(ta, dtype=np.float32)[None, :])
    kbias = jnp.zeros((N_HEADS, ta, LANES), BF16)
    for term in range(BIAS_TERMS):
        part = rest.astype(BF16)
        kbias = kbias.at[:, :, term].set(part)
        rest = rest - part.astype(F32)

    head_tile = pl.BlockSpec((1, ta, LANES), lambda b, h, i: (b, i, h))
    head_seq = pl.BlockSpec((1, S, LANES), lambda b, h, i: (b, 0, h))
    attn = pl.pallas_call(
        functools.partial(_attn_kernel, tile=ta),
        grid=(B, N_HEADS, S // ta),
        in_specs=[pl.BlockSpec(memory_space=pltpu.SMEM), head_tile, head_seq, head_seq,
                  pl.BlockSpec((1, ta, LANES), lambda b, h, i: (h, 0, 0)),
                  _const_spec((1, HEAD_DIM)), _const_spec((1, HEAD_DIM)),
                  _const_spec((1, HEAD_DIM)), _const_spec((1, HEAD_DIM)), _const_spec((1, V_DIM))],
        out_specs=head_tile,
        out_shape=jax.ShapeDtypeStruct((B, S, vw), BF16),
        scratch_shapes=[pltpu.VMEM((2, V_DIM, ta), F32)],
        compiler_params=_params(3),
        name="diffattn",
    )(jnp.asarray(slopes), q.reshape(B, S, aw), k.reshape(B, S, aw), v.reshape(B, S, vw), kbias,
      lambda_q1, lambda_k1, lambda_q2, lambda_k2, subln_g)

    x_mid = pl.pallas_call(
        functools.partial(_merge_kernel, seq=S, tm=tm),
        grid=(M // tm,),
        in_specs=[row(D), row(pw),
                  pl.BlockSpec((HALO, pw), lambda i: (jnp.maximum(i * (tm // HALO) - 1, 0), 0)),
                  row(vw), row(gw), _const_spec(pool_w.shape[1:]), _const_spec((1, pw)),
                  _const_spec((pw, D)), _const_spec((vw, D)), _const_spec((D, D))],
        out_specs=row(D),
        out_shape=jax.ShapeDtypeStruct((M, D), F32),
        scratch_shapes=[pltpu.VMEM((HALO + tm, pw), F32)],
        compiler_params=_params(1),
        name="merge",
    )(x2, u, u, attn.reshape(M, vw), gates, pool_w[0].astype(BF16), pool_scale,
      w_branch_a[0].astype(BF16), w_branch_b[0].astype(BF16), w_out[0].astype(BF16))

    out = pl.pallas_call(
        functools.partial(_ffn_kernel, chunk=1024),
        grid=(M // tm,),
        in_specs=[row(D), _const_spec((1, D)), _const_spec((D, d_ff)), _const_spec((d_ff, D))],
        out_specs=row(D),
        out_shape=jax.ShapeDtypeStruct((M, D), F32),
        compiler_params=_params(1),
        name="ffn",
    )(x_mid, norm2_g, w_ff1[0].astype(BF16), w_ff2[0].astype(BF16))
    return out.reshape(B, S, D)
```

```python
import functools
import math

import numpy as np
import jax
import jax.numpy as jnp
from jax import lax
from jax.experimental import pallas as pl
from jax.experimental.pallas import tpu as pltpu

F32 = jnp.float32
BF16 = jnp.bfloat16
EPS = 1e-6

POOL_WINDOWS = (2, 4, 8, 16)
POOL_GROUP = 128
HALO = max(POOL_WINDOWS)
N_HEADS = 4
HEAD_DIM = 64
V_DIM = 2 * HEAD_DIM
LANES = 128
MXU_DIM = 256
LAMBDA_INIT = 0.8 - 0.6 * math.exp(-0.3 * 0)
NEG_BIG = -1e30
LOG2E = math.log2(math.e)
BIAS_TERMS = 3

ROW_TILE = 512
ATTN_TILE = 512
VMEM_LIMIT = 56 * 1024 * 1024


def _dot(a, b):
    return jnp.dot(a, b, preferred_element_type=F32)


def _sigmoid(x):
    return 1.0 / (1.0 + jnp.exp(-x))


def _inproj_kernel(x_ref, g1_ref, w_ref, gb_ref, gq_ref, gk_ref, bd_ref,
                   u_ref, q_ref, k_ref, v_ref, g_ref, *, widths, scale):
    pw, aw, vw, gw = widths
    x = x_ref[...]
    ms = jnp.mean(x * x, axis=-1, keepdims=True)
    h = (x * lax.rsqrt(ms + EPS) * g1_ref[...]).astype(BF16)

    def proj(lo, width):
        return _dot(h, w_ref[:, lo:lo + width])

    bd = bd_ref[...]

    def group_rms(y):
        y2 = y * y
        hi = y2.astype(BF16)
        lo = (y2 - hi.astype(F32)).astype(BF16)
        parts = []
        for c in range(y.shape[1] // MXU_DIM):
            sl = slice(c * MXU_DIM, (c + 1) * MXU_DIM)
            parts.append(_dot(hi[:, sl], bd) + _dot(lo[:, sl], bd))
        ss = jnp.concatenate(parts, axis=1)
        return y * lax.rsqrt(ss * (1.0 / HEAD_DIM) + EPS)

    u_ref[...] = proj(0, pw)
    q_ref[...] = (group_rms(proj(pw, aw)) * gq_ref[...] * scale).astype(BF16)
    k_ref[...] = (group_rms(proj(pw + aw, aw)) * gk_ref[...]).astype(BF16)
    v_ref[...] = proj(pw + 2 * aw, vw).astype(BF16)
    g0 = pw + 2 * aw + vw
    chunk = 512
    for c in range(gw // chunk):
        gl = proj(g0 + c * chunk, chunk) + gb_ref[:, c * chunk:(c + 1) * chunk]
        g_ref[:, c * chunk:(c + 1) * chunk] = _sigmoid(gl).astype(BF16)


def _attn_kernel(slope_ref, q_ref, k_ref, v_ref, kb_ref, lq1_ref, lk1_ref, lq2_ref, lk2_ref,
                 sg_ref, o_ref, s_ref, m_ref, l_ref, acc_ref, *, tile):
    h = pl.program_id(1)
    qi = pl.program_id(2)
    slope = slope_ref[h]

    q = q_ref[0]
    lane = lax.broadcasted_iota(jnp.int32, q.shape, 1)
    zero = jnp.zeros_like(q)
    ones_cols = jnp.where(lane < BIAS_TERMS, 1.0, 0.0).astype(BF16)
    qa = (jnp.concatenate([jnp.where(lane < HEAD_DIM, q, zero), ones_cols], axis=1),
          jnp.concatenate([jnp.where(lane >= HEAD_DIM, q, zero), ones_cols], axis=1))
    kbias = kb_ref[0]

    row = lax.broadcasted_iota(jnp.int32, (tile, tile), 0)
    col = lax.broadcasted_iota(jnp.int32, (tile, tile), 1)
    future = row > col

    def scores(j, slot, masked):
        start = pl.multiple_of(j * tile, tile)
        ka = jnp.concatenate([k_ref[0, pl.ds(start, tile), :], kbias], axis=1)
        for c in range(2):
            s = lax.dot_general(ka, qa[c], (((1,), (1,)), ((), ())), preferred_element_type=F32)
            if masked:
                s = jnp.where(future, NEG_BIG, s)
            s_ref[slot, c] = s

    def accumulate(j, slot):
        start = pl.multiple_of(j * tile, tile)
        vb = v_ref[0, pl.ds(start, tile), :]
        off = slope * ((j - qi) * tile).astype(F32)
        for c in range(2):
            s = s_ref[slot, c]
            m_old = m_ref[c]
            m_new = jnp.maximum(m_old, jnp.max(s, axis=0, keepdims=True) + off)
            p = jnp.exp2(s - (m_new - off))
            alpha = jnp.exp2(m_old - m_new)
            l_ref[c] = alpha * l_ref[c] + jnp.sum(p, axis=0, keepdims=True)
            pv = lax.dot_general(vb, p.astype(BF16), (((0,), (0,)), ((), ())),
                                 preferred_element_type=F32)
            acc_ref[c] = alpha * acc_ref[c] + pv
            m_ref[c] = m_new

    m_ref[...] = jnp.full(m_ref.shape, NEG_BIG, F32)
    l_ref[...] = jnp.zeros(l_ref.shape, F32)
    acc_ref[...] = jnp.zeros(acc_ref.shape, F32)

    n_off = qi
    n_pairs = lax.shift_right_logical(n_off, 1)
    odd = (n_off & 1) == 1
    scores(qi, 0, True)

    def pair(t, carry):
        j = qi - 2 * t
        scores(j - 1, 1, False)
        accumulate(j, 0)
        scores(j - 2, 0, False)
        accumulate(j - 1, 1)
        return carry

    lax.fori_loop(0, n_pairs, pair, 0)
    j_last = qi - 2 * n_pairs

    @pl.when(odd)
    def _():
        scores(qi - n_off, 1, False)

    accumulate(j_last, 0)

    @pl.when(odd)
    def _():
        accumulate(qi - n_off, 1)

    lam = (jnp.exp(jnp.sum(lq1_ref[...] * lk1_ref[...], axis=-1, keepdims=True))
           - jnp.exp(jnp.sum(lq2_ref[...] * lk2_ref[...], axis=-1, keepdims=True))
           + LAMBDA_INIT)
    o = acc_ref[0] / l_ref[0] - lam * (acc_ref[1] / l_ref[1])
    ot = o.T
    msq = jnp.mean(ot * ot, axis=-1, keepdims=True)
    y = ot * lax.rsqrt(msq + EPS) * sg_ref[...] * (1.0 - LAMBDA_INIT)
    o_ref[0] = y.astype(BF16)


def _merge_kernel(x_ref, u_ref, halo_ref, a_ref, g_ref, pw_ref, ps_ref, wa_ref, wb_ref, wo_ref,
                  o_ref, ext_ref, *, seq, tm):
    i = pl.program_id(0)
    t0 = (i * tm) % seq
    halo = halo_ref[...]
    ext_ref[0:HALO, :] = jnp.where(t0 == 0, jnp.zeros_like(halo), halo)
    ext_ref[HALO:, :] = u_ref[...]

    t = t0 + lax.broadcasted_iota(jnp.int32, (tm, 1), 0)
    mixed = []
    for g, w in enumerate(POOL_WINDOWS):
        lanes = slice(g * POOL_GROUP, (g + 1) * POOL_GROUP)
        tot = ext_ref[HALO:, lanes]
        for d in range(1, w):
            tot = tot + ext_ref[HALO - d:HALO - d + tm, lanes]
        cnt = jnp.minimum(t + 1, w).astype(F32)
        pooled = tot / cnt - ext_ref[HALO:, lanes]
        mixed.append(_dot(pooled.astype(BF16), pw_ref[g]))
    mixed = jnp.concatenate(mixed, axis=1) * ps_ref[...]
    ya = _dot(mixed.astype(BF16), wa_ref[...])
    yb = _dot(a_ref[...], wb_ref[...])
    d = ya.shape[1]
    merged = g_ref[:, 0:d].astype(F32) * ya + g_ref[:, d:2 * d].astype(F32) * yb
    o_ref[...] = x_ref[...] + _dot(merged.astype(BF16), wo_ref[...])


def _ffn_kernel(x_ref, g2_ref, w1_ref, w2_ref, o_ref, *, chunk):
    x = x_ref[...]
    ms = jnp.mean(x * x, axis=-1, keepdims=True)
    h = (x * lax.rsqrt(ms + EPS) * g2_ref[...]).astype(BF16)
    acc = x
    for c in range(w1_ref.shape[1] // chunk):
        a = jnp.maximum(_dot(h, w1_ref[:, c * chunk:(c + 1) * chunk]), 0.0)
        acc = acc + _dot((a * a).astype(BF16), w2_ref[c * chunk:(c + 1) * chunk, :])
    o_ref[...] = acc


def _const_spec(shape):
    return pl.BlockSpec(shape, lambda *_: (0,) * len(shape))


def _params(n_axes):
    return pltpu.CompilerParams(dimension_semantics=("arbitrary",) * n_axes,
                                vmem_limit_bytes=VMEM_LIMIT)


def kernel(x, norm1_g, w_in, gate_b, pool_w, pool_scale, q_norm_g, k_norm_g, lambda_q1, lambda_k1,
           lambda_q2, lambda_k2, subln_g, w_branch_a, w_branch_b, w_out, norm2_g, w_ff1, w_ff2):
    B, S, D = x.shape
    depth = norm1_g.shape[0]
    assert depth == 1
    M = B * S
    pw = pool_scale.shape[1]
    aw = 2 * N_HEADS * HEAD_DIM
    vw = N_HEADS * V_DIM
    gw = 2 * D
    d_ff = w_ff1.shape[2]
    tm = ROW_TILE
    ta = ATTN_TILE
    assert S % tm == 0 and S % ta == 0 and tm % HALO == 0
    assert pw == len(POOL_WINDOWS) * POOL_GROUP and w_in.shape[2] == pw + 2 * aw + vw + gw

    x2 = x.reshape(M, D)
    row = lambda width: pl.BlockSpec((tm, width), lambda i: (i, 0))

    bd = jnp.asarray(np.kron(np.eye(MXU_DIM // HEAD_DIM), np.ones((HEAD_DIM, HEAD_DIM))), BF16)
    reps = aw // HEAD_DIM
    u, q, k, v, gates = pl.pallas_call(
        functools.partial(_inproj_kernel, widths=(pw, aw, vw, gw), scale=LOG2E / math.sqrt(HEAD_DIM)),
        grid=(M // tm,),
        in_specs=[row(D), _const_spec((1, D)), _const_spec(w_in.shape[1:]), _const_spec((1, gw)),
                  _const_spec((1, aw)), _const_spec((1, aw)), _const_spec((MXU_DIM, MXU_DIM))],
        out_specs=[row(pw), row(aw), row(aw), row(vw), row(gw)],
        out_shape=[jax.ShapeDtypeStruct((M, pw), F32), jax.ShapeDtypeStruct((M, aw), BF16),
                   jax.ShapeDtypeStruct((M, aw), BF16), jax.ShapeDtypeStruct((M, vw), BF16),
                   jax.ShapeDtypeStruct((M, gw), BF16)],
        compiler_params=_params(1),
        name="inproj",
    )(x2, norm1_g, w_in[0].astype(BF16), gate_b, jnp.tile(q_norm_g, (1, reps)),
      jnp.tile(k_norm_g, (1, reps)), bd)

    slopes = np.array([2.0 ** (-8.0 * (h + 1) / N_HEADS) for h in range(N_HEADS)], np.float64)
    slopes = (slopes * LOG2E).astype(np.float32)
    rest = jnp.asarray(slopes[:, None] * np.arange(ta, dtype=np.float32)[None, :])
    kbias = jnp.zeros((N_HEADS, ta, LANES), BF16)
    for term in range(BIAS_TERMS):
        part = rest.astype(BF16)
        kbias = kbias.at[:, :, term].set(part)
        rest = rest - part.astype(F32)

    head_tile = pl.BlockSpec((1, ta, LANES), lambda b, h, i: (b, i, h))
    head_seq = pl.BlockSpec((1, S, LANES), lambda b, h, i: (b, 0, h))
    attn = pl.pallas_call(
        functools.partial(_attn_kernel, tile=ta),
        grid=(B, N_HEADS, S // ta),
        in_specs=[pl.BlockSpec(memory_space=pltpu.SMEM), head_tile, head_seq, head_seq,
                  pl.BlockSpec((1, ta, LANES), lambda b, h, i: (h, 0, 0)),
                  _const_spec((1, HEAD_DIM)), _const_spec((1, HEAD_DIM)),
                  _const_spec((1, HEAD_DIM)), _const_spec((1, HEAD_DIM)), _const_spec((1, V_DIM))],
        out_specs=head_tile,
        out_shape=jax.ShapeDtypeStruct((B, S, vw), BF16),
        scratch_shapes=[pltpu.VMEM((2, 2, ta, ta), F32),
                        pltpu.VMEM((2, 1, ta), F32),
                        pltpu.VMEM((2, 1, ta), F32),
                        pltpu.VMEM((2, V_DIM, ta), F32)],
        compiler_params=_params(3),
        name="diffattn",
    )(jnp.asarray(slopes), q.reshape(B, S, aw), k.reshape(B, S, aw), v.reshape(B, S, vw), kbias,
      lambda_q1, lambda_k1, lambda_q2, lambda_k2, subln_g)

    x_mid = pl.pallas_call(
        functools.partial(_merge_kernel, seq=S, tm=tm),
        grid=(M // tm,),
        in_specs=[row(D), row(pw),
                  pl.BlockSpec((HALO, pw), lambda i: (jnp.maximum(i * (tm // HALO) - 1, 0), 0)),
                  row(vw), row(gw), _const_spec(pool_w.shape[1:]), _const_spec((1, pw)),
                  _const_spec((pw, D)), _const_spec((vw, D)), _const_spec((D, D))],
        out_specs=row(D),
        out_shape=jax.ShapeDtypeStruct((M, D), F32),
        scratch_shapes=[pltpu.VMEM((HALO + tm, pw), F32)],
        compiler_params=_params(1),
        name="merge",
    )(x2, u, u, attn.reshape(M, vw), gates, pool_w[0].astype(BF16), pool_scale,
      w_branch_a[0].astype(BF16), w_branch_b[0].astype(BF16), w_out[0].astype(BF16))

    out = pl.pallas_call(
        functools.partial(_ffn_kernel, chunk=1024),
        grid=(M // tm,),
        in_specs=[row(D), _const_spec((1, D)), _const_spec((D, d_ff)), _const_spec((d_ff, D))],
        out_specs=row(D),
        out_shape=jax.ShapeDtypeStruct((M, D), F32),
        compiler_params=_params(1),
        name="ffn",
    )(x_mid, norm2_g, w_ff1[0].astype(BF16), w_ff2[0].astype(BF16))
    return out.reshape(B, S, D)
```

```python
import functools
import math

import numpy as np
import jax
import jax.numpy as jnp
from jax import lax
from jax.experimental import pallas as pl
from jax.experimental.pallas import tpu as pltpu

F32 = jnp.float32
BF16 = jnp.bfloat16
EPS = 1e-6

POOL_WINDOWS = (2, 4, 8, 16)
POOL_GROUP = 128
HALO = max(POOL_WINDOWS)
N_HEADS = 4
HEAD_DIM = 64
V_DIM = 2 * HEAD_DIM
LANES = 128
MXU_DIM = 256
LAMBDA_INIT = 0.8 - 0.6 * math.exp(-0.3 * 0)
NEG_BIG = -1e30
LOG2E = math.log2(math.e)
BIAS_TERMS = 3
SCORE_BOUND_MAX = 60.0
EXP2_ZERO_BELOW = -150.0

ROW_TILE = 512
ATTN_TILE = 512
VMEM_LIMIT = 56 * 1024 * 1024


def _dot(a, b):
    return jnp.dot(a, b, preferred_element_type=F32)


def _sigmoid(x):
    return 1.0 / (1.0 + jnp.exp(-x))


def _inproj_kernel(x_ref, g1_ref, w_ref, gb_ref, gq_ref, gk_ref, bd_ref,
                   u_ref, q_ref, k_ref, v_ref, g_ref, *, widths, scale):
    pw, aw, vw, gw = widths
    x = x_ref[...]
    ms = jnp.mean(x * x, axis=-1, keepdims=True)
    h = (x * lax.rsqrt(ms + EPS) * g1_ref[...]).astype(BF16)

    def proj(lo, width):
        return _dot(h, w_ref[:, lo:lo + width])

    bd = bd_ref[...]

    def group_rms(y):
        y2 = y * y
        hi = y2.astype(BF16)
        lo = (y2 - hi.astype(F32)).astype(BF16)
        parts = []
        for c in range(y.shape[1] // MXU_DIM):
            sl = slice(c * MXU_DIM, (c + 1) * MXU_DIM)
            parts.append(_dot(hi[:, sl], bd) + _dot(lo[:, sl], bd))
        ss = jnp.concatenate(parts, axis=1)
        return y * lax.rsqrt(ss * (1.0 / HEAD_DIM) + EPS)

    u_ref[...] = proj(0, pw)
    q_ref[...] = (group_rms(proj(pw, aw)) * gq_ref[...] * scale).astype(BF16)
    k_ref[...] = (group_rms(proj(pw + aw, aw)) * gk_ref[...]).astype(BF16)
    v_ref[...] = proj(pw + 2 * aw, vw).astype(BF16)
    g0 = pw + 2 * aw + vw
    chunk = 512
    for c in range(gw // chunk):
        gl = proj(g0 + c * chunk, chunk) + gb_ref[:, c * chunk:(c + 1) * chunk]
        g_ref[:, c * chunk:(c + 1) * chunk] = _sigmoid(gl).astype(BF16)


def _attn_kernel(plan_ref, q_ref, k_ref, v_ref, qc_ref, kc_ref, lq1_ref, lk1_ref, lq2_ref, lk2_ref,
                 sg_ref, o_ref, p_ref, m_ref, l_ref, acc_ref, *, tile):
    h = pl.program_id(1)
    qi = pl.program_id(2)
    bounded = plan_ref[0] == 1
    n_reach = plan_ref[1 + h]

    q = q_ref[0]
    lane = lax.broadcasted_iota(jnp.int32, q.shape, 1)
    zero = jnp.zeros_like(q)
    qcols = qc_ref[0]
    qa = (jnp.concatenate([jnp.where(lane < HEAD_DIM, q, zero), qcols], axis=1),
          jnp.concatenate([jnp.where(lane >= HEAD_DIM, q, zero), qcols], axis=1))
    kcols = kc_ref[0]
    lane1 = lax.broadcasted_iota(jnp.int32, (1, LANES), 1)
    dist_lane = (lane1 >= 2 * BIAS_TERMS) & (lane1 < 3 * BIAS_TERMS)

    row = lax.broadcasted_iota(jnp.int32, (tile, tile), 0)
    col = lax.broadcasted_iota(jnp.int32, (tile, tile), 1)
    future = row > col

    def scores(j, masked):
        start = pl.multiple_of(j * tile, tile)
        dist = jnp.where(dist_lane, (j - qi).astype(F32), 0.0).astype(BF16)
        ka = jnp.concatenate([k_ref[0, pl.ds(start, tile), :], kcols + dist], axis=1)
        out = []
        for c in range(2):
            s = lax.dot_general(ka, qa[c], (((1,), (1,)), ((), ())), preferred_element_type=F32)
            out.append(jnp.where(future, NEG_BIG, s) if masked else s)
        return out

    def values_t_probs(j, p):
        start = pl.multiple_of(j * tile, tile)
        return lax.dot_general(v_ref[0, pl.ds(start, tile), :], p, (((0,), (0,)), ((), ())),
                               preferred_element_type=F32)

    l_ref[...] = jnp.zeros(l_ref.shape, F32)
    acc_ref[...] = jnp.zeros(acc_ref.shape, F32)

    @pl.when(bounded)
    def _():
        def probs(j, slot, masked):
            for c, s in enumerate(scores(j, masked)):
                p = jnp.exp2(s)
                l_ref[c] += jnp.sum(p, axis=0, keepdims=True)
                p_ref[slot, c] = p.astype(BF16)

        def accumulate(j, slot):
            for c in range(2):
                acc_ref[c] += values_t_probs(j, p_ref[slot, c])

        n_off = jnp.minimum(qi, n_reach)
        n_pairs = lax.shift_right_logical(n_off, 1)
        odd = (n_off & 1) == 1
        probs(qi, 0, True)

        def pair(t, carry):
            j = qi - 2 * t
            probs(j - 1, 1, False)
            accumulate(j, 0)
            probs(j - 2, 0, False)
            accumulate(j - 1, 1)
            return carry

        lax.fori_loop(0, n_pairs, pair, 0)

        @pl.when(odd)
        def _():
            probs(qi - n_off, 1, False)

        accumulate(qi - 2 * n_pairs, 0)

        @pl.when(odd)
        def _():
            accumulate(qi - n_off, 1)

    @pl.when(jnp.logical_not(bounded))
    def _():
        m_ref[...] = jnp.full(m_ref.shape, NEG_BIG, F32)

        def block(j, masked):
            for c, s in enumerate(scores(j, masked)):
                m_old = m_ref[c]
                m_new = jnp.maximum(m_old, jnp.max(s, axis=0, keepdims=True))
                p = jnp.exp2(s - m_new)
                alpha = jnp.exp2(m_old - m_new)
                l_ref[c] = alpha * l_ref[c] + jnp.sum(p, axis=0, keepdims=True)
                acc_ref[c] = alpha * acc_ref[c] + values_t_probs(j, p.astype(BF16))
                m_ref[c] = m_new

        block(qi, True)

        def body(t, carry):
            block(qi - 1 - t, False)
            return carry

        lax.fori_loop(0, qi, body, 0)

    lam =(jnp.exp(jnp.sum(lq1_ref[...] * lk1_ref[...], axis=-1, keepdims=True))
           - jnp.exp(jnp.sum(lq2_ref[...] * lk2_ref[...], axis=-1, keepdims=True))
           + LAMBDA_INIT)
    o = acc_ref[0] / l_ref[0] - lam * (acc_ref[1] / l_ref[1])
    ot = o.T
    msq = jnp.mean(ot * ot, axis=-1, keepdims=True)
    y = ot * lax.rsqrt(msq + EPS) * sg_ref[...] * (1.0 - LAMBDA_INIT)
    o_ref[0] = y.astype(BF16)


def _merge_kernel(x_ref, u_ref, halo_ref, a_ref, g_ref, pw_ref, ps_ref, wa_ref, wb_ref, wo_ref,
                  o_ref, ext_ref, *, seq, tm):
    i = pl.program_id(0)
    t0 = (i * tm) % seq
    halo = halo_ref[...]
    ext_ref[0:HALO, :] = jnp.where(t0 == 0, jnp.zeros_like(halo), halo)
    ext_ref[HALO:, :] = u_ref[...]

    t = t0 + lax.broadcasted_iota(jnp.int32, (tm, 1), 0)
    mixed = []
    for g, w in enumerate(POOL_WINDOWS):
        lanes = slice(g * POOL_GROUP, (g + 1) * POOL_GROUP)
        tot = ext_ref[HALO:, lanes]
        for d in range(1, w):
            tot = tot + ext_ref[HALO - d:HALO - d + tm, lanes]
        cnt = jnp.minimum(t + 1, w).astype(F32)
        pooled = tot / cnt - ext_ref[HALO:, lanes]
        mixed.append(_dot(pooled.astype(BF16), pw_ref[g]))
    mixed = jnp.concatenate(mixed, axis=1) * ps_ref[...]
    ya = _dot(mixed.astype(BF16), wa_ref[...])
    yb = _dot(a_ref[...], wb_ref[...])
    d = ya.shape[1]
    merged = g_ref[:, 0:d].astype(F32) * ya + g_ref[:, d:2 * d].astype(F32) * yb
    o_ref[...] = x_ref[...] + _dot(merged.astype(BF16), wo_ref[...])


def _ffn_kernel(x_ref, g2_ref, w1_ref, w2_ref, o_ref, *, chunk):
    x = x_ref[...]
    ms = jnp.mean(x * x, axis=-1, keepdims=True)
    h = (x * lax.rsqrt(ms + EPS) * g2_ref[...]).astype(BF16)
    acc = x
    for c in range(w1_ref.shape[1] // chunk):
        a = jnp.maximum(_dot(h, w1_ref[:, c * chunk:(c + 1) * chunk]), 0.0)
        acc = acc + _dot((a * a).astype(BF16), w2_ref[c * chunk:(c + 1) * chunk, :])
    o_ref[...] = acc


def _alibi_columns(tile):
    slopes2 = np.array([2.0 ** (-8.0 * (h + 1) / N_HEADS) for h in range(N_HEADS)]) * LOG2E
    pos = np.arange(tile, dtype=np.float64)

    def terms(x):
        out = []
        for _ in range(BIAS_TERMS):
            part = x.astype(BF16)
            out.append(part)
            x = x - part.astype(np.float64)
        return out

    kcols = np.zeros((N_HEADS, tile, LANES), BF16)
    qcols = np.zeros((N_HEADS, tile, LANES), BF16)
    k_terms = terms(slopes2[:, None] * pos[None, :])
    q_terms = terms(-slopes2[:, None] * pos[None, :])
    t_terms = terms(np.broadcast_to(slopes2[:, None] * tile, (N_HEADS, tile)))
    for t in range(BIAS_TERMS):
        kcols[:, :, t] = k_terms[t]
        kcols[:, :, BIAS_TERMS + t] = 1.0
        qcols[:, :, t] = 1.0
        qcols[:, :, BIAS_TERMS + t] = q_terms[t]
        qcols[:, :, 2 * BIAS_TERMS + t] = t_terms[t]
    return jnp.asarray(qcols), jnp.asarray(kcols), slopes2.astype(np.float32)


def _const_spec(shape):
    return pl.BlockSpec(shape, lambda *_: (0,) * len(shape))


def _params(n_axes):
    return pltpu.CompilerParams(dimension_semantics=("arbitrary",) * n_axes,
                                vmem_limit_bytes=VMEM_LIMIT)


def kernel(x, norm1_g, w_in, gate_b, pool_w, pool_scale, q_norm_g, k_norm_g, lambda_q1, lambda_k1,
           lambda_q2, lambda_k2, subln_g, w_branch_a, w_branch_b, w_out, norm2_g, w_ff1, w_ff2):
    B, S, D = x.shape
    depth = norm1_g.shape[0]
    assert depth == 1
    M = B * S
    pw = pool_scale.shape[1]
    aw = 2 * N_HEADS * HEAD_DIM
    vw = N_HEADS * V_DIM
    gw = 2 * D
    d_ff = w_ff1.shape[2]
    tm = ROW_TILE
    ta = ATTN_TILE
    assert S % tm == 0 and S % ta == 0 and tm % HALO == 0
    assert pw == len(POOL_WINDOWS) * POOL_GROUP and w_in.shape[2] == pw + 2 * aw + vw + gw

    x2 = x.reshape(M, D)
    row = lambda width: pl.BlockSpec((tm, width), lambda i: (i, 0))

    bd = jnp.asarray(np.kron(np.eye(MXU_DIM // HEAD_DIM), np.ones((HEAD_DIM, HEAD_DIM))), BF16)
    reps = aw // HEAD_DIM
    u, q, k, v, gates = pl.pallas_call(
        functools.partial(_inproj_kernel, widths=(pw, aw, vw, gw), scale=LOG2E / math.sqrt(HEAD_DIM)),
        grid=(M // tm,),
        in_specs=[row(D), _const_spec((1, D)), _const_spec(w_in.shape[1:]), _const_spec((1, gw)),
                  _const_spec((1, aw)), _const_spec((1, aw)), _const_spec((MXU_DIM, MXU_DIM))],
        out_specs=[row(pw), row(aw), row(aw), row(vw), row(gw)],
        out_shape=[jax.ShapeDtypeStruct((M, pw), F32), jax.ShapeDtypeStruct((M, aw), BF16),
                   jax.ShapeDtypeStruct((M, aw), BF16), jax.ShapeDtypeStruct((M, vw), BF16),
                   jax.ShapeDtypeStruct((M, gw), BF16)],
        compiler_params=_params(1),
        name="inproj",
    )(x2, norm1_g, w_in[0].astype(BF16), gate_b, jnp.tile(q_norm_g, (1, reps)),
      jnp.tile(k_norm_g, (1, reps)), bd)

    qcols, kcols, slopes2 = _alibi_columns(ta)

    bound = (1.01 * LOG2E * math.sqrt(HEAD_DIM)) * jnp.max(jnp.abs(q_norm_g)) * jnp.max(jnp.abs(k_norm_g))
    bounded = bound <= SCORE_BOUND_MAX
    reach = jnp.floor(((bound - EXP2_ZERO_BELOW) / jnp.asarray(slopes2, F32) - 1.0) / ta) + 1.0
    reach = jnp.clip(jnp.where(bounded, reach, S // ta), 0, S // ta)
    plan = jnp.concatenate([bounded.astype(jnp.int32)[None], reach.astype(jnp.int32)])

    head_tile = pl.BlockSpec((1, ta, LANES), lambda b, h, i: (b, i, h))
    head_seq = pl.BlockSpec((1, S, LANES), lambda b, h, i: (b, 0, h))
    head_cols = pl.BlockSpec((1, ta, LANES), lambda b, h, i: (h, 0, 0))
    attn = pl.pallas_call(
        functools.partial(_attn_kernel, tile=ta),
        grid=(B, N_HEADS, S // ta),
        in_specs=[pl.BlockSpec(memory_space=pltpu.SMEM), head_tile, head_seq, head_seq,
                  head_cols, head_cols,
                  _const_spec((1, HEAD_DIM)), _const_spec((1, HEAD_DIM)),
                  _const_spec((1, HEAD_DIM)), _const_spec((1, HEAD_DIM)), _const_spec((1, V_DIM))],
        out_specs=head_tile,
        out_shape=jax.ShapeDtypeStruct((B, S, vw), BF16),
        scratch_shapes=[pltpu.VMEM((2, 2, ta, ta), BF16),
                        pltpu.VMEM((2, 1, ta), F32),
                        pltpu.VMEM((2, 1, ta), F32),
                        pltpu.VMEM((2, V_DIM, ta), F32)],
        compiler_params=_params(3),
        name="diffattn",
    )(plan, q.reshape(B, S, aw), k.reshape(B, S, aw), v.reshape(B, S, vw), qcols, kcols,
      lambda_q1, lambda_k1, lambda_q2, lambda_k2, subln_g)

    x_mid = pl.pallas_call(
        functools.partial(_merge_kernel, seq=S, tm=tm),
        grid=(M // tm,),
        in_specs=[row(D), row(pw),
                  pl.BlockSpec((HALO, pw), lambda i: (jnp.maximum(i * (tm // HALO) - 1, 0), 0)),
                  row(vw), row(gw), _const_spec(pool_w.shape[1:]), _const_spec((1, pw)),
                  _const_spec((pw, D)), _const_spec((vw, D)), _const_spec((D, D))],
        out_specs=row(D),
        out_shape=jax.ShapeDtypeStruct((M, D), F32),
        scratch_shapes=[pltpu.VMEM((HALO + tm, pw), F32)],
        compiler_params=_params(1),
        name="merge",
    )(x2, u, u, attn.reshape(M, vw), gates, pool_w[0].astype(BF16), pool_scale,
      w_branch_a[0].astype(BF16), w_branch_b[0].astype(BF16), w_out[0].astype(BF16))

    out = pl.pallas_call(
        functools.partial(_ffn_kernel, chunk=1024),
        grid=(M // tm,),
        in_specs=[row(D), _const_spec((1, D)), _const_spec((D, d_ff)), _const_spec((d_ff, D))],
        out_specs=row(D),
        out_shape=jax.ShapeDtypeStruct((M, D), F32),
        compiler_params=_params(1),
        name="ffn",
    )(x_mid, norm2_g, w_ff1[0].astype(BF16), w_ff2[0].astype(BF16))
    return out.reshape(B, S, D)
```

```python
import functools
import math

import numpy as np
import jax
import jax.numpy as jnp
from jax import lax
from jax.experimental import pallas as pl
from jax.experimental.pallas import tpu as pltpu

F32 = jnp.float32
BF16 = jnp.bfloat16
EPS = 1e-6

POOL_WINDOWS = (2, 4, 8, 16)
POOL_GROUP = 128
HALO = max(POOL_WINDOWS)
N_HEADS = 4
HEAD_DIM = 64
V_DIM = 2 * HEAD_DIM
LANES = 128
MXU_DIM = 256
LAMBDA_INIT = 0.8 - 0.6 * math.exp(-0.3 * 0)
NEG_BIG = -1e30
LOG2E = math.log2(math.e)
BIAS_TERMS = 3
SCORE_BOUND_MAX = 60.0
EXP2_ZERO_BELOW = -150.0

ROW_TILE = 512
ATTN_TILE = 512
BELOW_UNROLL = 4
VMEM_LIMIT = 56 * 1024 * 1024


def _dot(a, b):
    return jnp.dot(a, b, preferred_element_type=F32)


def _sigmoid(x):
    return 1.0 / (1.0 + jnp.exp(-x))


def _inproj_kernel(x_ref, g1_ref, w_ref, gb_ref, gq_ref, gk_ref, bd_ref,
                   u_ref, q_ref, k_ref, v_ref, g_ref, *, widths, scale):
    pw, aw, vw, gw = widths
    x = x_ref[...]
    ms = jnp.mean(x * x, axis=-1, keepdims=True)
    h = (x * lax.rsqrt(ms + EPS) * g1_ref[...]).astype(BF16)

    def proj(lo, width):
        return _dot(h, w_ref[:, lo:lo + width])

    bd = bd_ref[...]

    def group_rms(y):
        y2 = y * y
        hi = y2.astype(BF16)
        lo = (y2 - hi.astype(F32)).astype(BF16)
        parts = []
        for c in range(y.shape[1] // MXU_DIM):
            sl = slice(c * MXU_DIM, (c + 1) * MXU_DIM)
            parts.append(_dot(hi[:, sl], bd) + _dot(lo[:, sl], bd))
        ss = jnp.concatenate(parts, axis=1)
        return y * lax.rsqrt(ss * (1.0 / HEAD_DIM) + EPS)

    u_ref[...] = proj(0, pw)
    q_ref[...] = (group_rms(proj(pw, aw)) * gq_ref[...] * scale).astype(BF16)
    k_ref[...] = (group_rms(proj(pw + aw, aw)) * gk_ref[...]).astype(BF16)
    v_ref[...] = proj(pw + 2 * aw, vw).astype(BF16)
    g0 = pw + 2 * aw + vw
    chunk = 512
    for c in range(gw // chunk):
        gl = proj(g0 + c * chunk, chunk) + gb_ref[:, c * chunk:(c + 1) * chunk]
        g_ref[:, c * chunk:(c + 1) * chunk] = _sigmoid(gl).astype(BF16)


def _attn_kernel(plan_ref, q_ref, k_ref, v_ref, qc_ref, kc_ref, lq1_ref, lk1_ref, lq2_ref, lk2_ref,
                 sg_ref, o_ref, p_ref, m_ref, l_ref, acc_ref, *, tile, n_tiles):
    h = pl.program_id(1)
    bounded = plan_ref[0] == 1
    n_reach = plan_ref[1 + h]
    spare = n_tiles

    lane = lax.broadcasted_iota(jnp.int32, (tile, LANES), 1)
    qcols = qc_ref[0]
    kcols = kc_ref[0]
    lane1 = lax.broadcasted_iota(jnp.int32, (1, LANES), 1)
    dist_lane = (lane1 >= 2 * BIAS_TERMS) & (lane1 < 3 * BIAS_TERMS)

    row = lax.broadcasted_iota(jnp.int32, (tile, tile), 0)
    col = lax.broadcasted_iota(jnp.int32, (tile, tile), 1)
    future = row > col

    def scores(qi, j, masked):
        q = q_ref[0, pl.ds(pl.multiple_of(qi * tile, tile), tile), :]
        zero = jnp.zeros_like(q)
        dist = jnp.where(dist_lane, jnp.asarray(j - qi).astype(F32), 0.0).astype(BF16)
        ka = jnp.concatenate([k_ref[0, pl.ds(pl.multiple_of(j * tile, tile), tile), :], kcols + dist],
                             axis=1)
        out = []
        for c in range(2):
            qc = jnp.where(lane >= HEAD_DIM if c else lane < HEAD_DIM, q, zero)
            qa = jnp.concatenate([qc, qcols], axis=1)
            s = lax.dot_general(ka, qa, (((1,), (1,)), ((), ())), preferred_element_type=F32)
            out.append(jnp.where(future, NEG_BIG, s) if masked else s)
        return out

    def values_t_probs(j, p):
        start = pl.multiple_of(j * tile, tile)
        return lax.dot_general(v_ref[0, pl.ds(start, tile), :], p, (((0,), (0,)), ((), ())),
                               preferred_element_type=F32)

    l_ref[...] = jnp.zeros(l_ref.shape, F32)
    acc_ref[...] = jnp.zeros(acc_ref.shape, F32)

    @pl.when(bounded)
    def _():
        def probs(item, slot, masked):
            qi, j, state = item
            for c, s in enumerate(scores(qi, j, masked)):
                p = jnp.exp2(s)
                l_ref[state, c] += jnp.sum(p, axis=0, keepdims=True)
                p_ref[slot, c] = p.astype(BF16)

        def accumulate(item, slot):
            _, j, state = item
            for c in range(2):
                acc_ref[state, c] += values_t_probs(j, p_ref[slot, c])

        def diagonal(i):
            return i, i, i

        def below(cursor):
            d, qi = cursor
            live = d <= reach
            return jnp.where(live, qi, 0), jnp.where(live, qi - d, 0), jnp.where(live, qi, spare)

        def advance(cursor):
            d, qi = cursor
            wrap = qi + 1 >= n_tiles
            d = jnp.where(wrap, d + 1, d)
            return d, jnp.where(wrap, d, qi + 1)

        reach = jnp.minimum(n_reach, n_tiles - 1)
        n_below = reach * n_tiles - lax.shift_right_logical(reach * (reach + 1), 1)

        probs(diagonal(0), 0, True)

        def diagonal_pair(t, carry):
            i = 2 * t
            probs(diagonal(i + 1), 1, True)
            accumulate(diagonal(i), 0)
            probs(diagonal(i + 2), 0, True)
            accumulate(diagonal(i + 1), 1)
            return carry

        lax.fori_loop(0, n_tiles // 2 - 1, diagonal_pair, 0)
        first = (jnp.int32(1), jnp.int32(1))
        probs(diagonal(n_tiles - 1), 1, True)
        accumulate(diagonal(n_tiles - 2), 0)
        probs(below(first), 0, False)
        accumulate(diagonal(n_tiles - 1), 1)

        def below_group(t, cursor):
            for u in range(BELOW_UNROLL):
                nxt = advance(cursor)
                probs(below(nxt), (u + 1) % 2, False)
                accumulate(below(cursor), u % 2)
                cursor = nxt
            return cursor

        lax.fori_loop(0, (n_below + BELOW_UNROLL - 1) // BELOW_UNROLL, below_group, first)

    @pl.when(jnp.logical_not(bounded))
    def _():
        def block(qi, j, masked):
            for c, s in enumerate(scores(qi, j, masked)):
                m_old = m_ref[c]
                m_new = jnp.maximum(m_old, jnp.max(s, axis=0, keepdims=True))
                p = jnp.exp2(s - m_new)
                alpha = jnp.exp2(m_old - m_new)
                l_ref[qi, c] = alpha * l_ref[qi, c] + jnp.sum(p, axis=0, keepdims=True)
                acc_ref[qi, c] = alpha * acc_ref[qi, c] + values_t_probs(j, p.astype(BF16))
                m_ref[c] = m_new

        def query_tile(qi, carry):
            m_ref[...] = jnp.full(m_ref.shape, NEG_BIG, F32)
            block(qi, qi, True)

            def body(t, c):
                block(qi, qi - 1 - t, False)
                return c

            lax.fori_loop(0, qi, body, 0)
            return carry

        lax.fori_loop(0, n_tiles, query_tile, 0)

    lam = (jnp.exp(jnp.sum(lq1_ref[...] * lk1_ref[...], axis=-1, keepdims=True))
           - jnp.exp(jnp.sum(lq2_ref[...] * lk2_ref[...], axis=-1, keepdims=True))
           + LAMBDA_INIT)

    def finish(qi, carry):
        o = acc_ref[qi, 0] / l_ref[qi, 0] - lam * (acc_ref[qi, 1] / l_ref[qi, 1])
        ot = o.T
        msq = jnp.mean(ot * ot, axis=-1, keepdims=True)
        y = ot * lax.rsqrt(msq + EPS) * sg_ref[...] * (1.0 - LAMBDA_INIT)
        o_ref[0, pl.ds(pl.multiple_of(qi * tile, tile), tile), :] = y.astype(BF16)
        return carry

    lax.fori_loop(0, n_tiles, finish, 0)


def _merge_kernel(x_ref, u_ref, halo_ref, a_ref, g_ref, pw_ref, ps_ref, wa_ref, wb_ref, wo_ref,
                  o_ref, ext_ref, *, seq, tm):
    i = pl.program_id(0)
    t0 = (i * tm) % seq
    halo = halo_ref[...]
    ext_ref[0:HALO, :] = jnp.where(t0 == 0, jnp.zeros_like(halo), halo)
    ext_ref[HALO:, :] = u_ref[...]

    t = t0 + lax.broadcasted_iota(jnp.int32, (tm, 1), 0)
    mixed = []
    for g, w in enumerate(POOL_WINDOWS):
        lanes = slice(g * POOL_GROUP, (g + 1) * POOL_GROUP)
        tot = ext_ref[HALO:, lanes]
        for d in range(1, w):
            tot = tot + ext_ref[HALO - d:HALO - d + tm, lanes]
        cnt = jnp.minimum(t + 1, w).astype(F32)
        pooled = tot / cnt - ext_ref[HALO:, lanes]
        mixed.append(_dot(pooled.astype(BF16), pw_ref[g]))
    mixed = jnp.concatenate(mixed, axis=1) * ps_ref[...]
    ya = _dot(mixed.astype(BF16), wa_ref[...])
    yb = _dot(a_ref[...], wb_ref[...])
    d = ya.shape[1]
    merged = g_ref[:, 0:d].astype(F32) * ya + g_ref[:, d:2 * d].astype(F32) * yb
    o_ref[...] = x_ref[...] + _dot(merged.astype(BF16), wo_ref[...])


def _ffn_kernel(x_ref, g2_ref, w1_ref, w2_ref, o_ref, *, chunk):
    x = x_ref[...]
    ms = jnp.mean(x * x, axis=-1, keepdims=True)
    h = (x * lax.rsqrt(ms + EPS) * g2_ref[...]).astype(BF16)
    acc = x
    for c in range(w1_ref.shape[1] // chunk):
        a = jnp.maximum(_dot(h, w1_ref[:, c * chunk:(c + 1) * chunk]), 0.0)
        acc = acc + _dot((a * a).astype(BF16), w2_ref[c * chunk:(c + 1) * chunk, :])
    o_ref[...] = acc


def _alibi_columns(tile):
    slopes2 = np.array([2.0 ** (-8.0 * (h + 1) / N_HEADS) for h in range(N_HEADS)]) * LOG2E
    pos = np.arange(tile, dtype=np.float64)

    def terms(x):
        out = []
        for _ in range(BIAS_TERMS):
            part = x.astype(BF16)
            out.append(part)
            x = x - part.astype(np.float64)
        return out

    kcols = np.zeros((N_HEADS, tile, LANES), BF16)
    qcols = np.zeros((N_HEADS, tile, LANES), BF16)
    k_terms = terms(slopes2[:, None] * pos[None, :])
    q_terms = terms(-slopes2[:, None] * pos[None, :])
    t_terms = terms(np.broadcast_to(slopes2[:, None] * tile, (N_HEADS, tile)))
    for t in range(BIAS_TERMS):
        kcols[:, :, t] = k_terms[t]
        kcols[:, :, BIAS_TERMS + t] = 1.0
        qcols[:, :, t] = 1.0
        qcols[:, :, BIAS_TERMS + t] = q_terms[t]
        qcols[:, :, 2 * BIAS_TERMS + t] = t_terms[t]
    return jnp.asarray(qcols), jnp.asarray(kcols), slopes2.astype(np.float32)


def _const_spec(shape):
    return pl.BlockSpec(shape, lambda *_: (0,) * len(shape))


def _params(n_axes):
    return pltpu.CompilerParams(dimension_semantics=("arbitrary",) * n_axes,
                                vmem_limit_bytes=VMEM_LIMIT)


def kernel(x, norm1_g, w_in, gate_b, pool_w, pool_scale, q_norm_g, k_norm_g, lambda_q1, lambda_k1,
           lambda_q2, lambda_k2, subln_g, w_branch_a, w_branch_b, w_out, norm2_g, w_ff1, w_ff2):
    B, S, D = x.shape
    depth = norm1_g.shape[0]
    assert depth == 1
    M = B * S
    pw = pool_scale.shape[1]
    aw = 2 * N_HEADS * HEAD_DIM
    vw = N_HEADS * V_DIM
    gw = 2 * D
    d_ff = w_ff1.shape[2]
    tm = ROW_TILE
    ta = ATTN_TILE
    assert S % tm == 0 and S % ta == 0 and tm % HALO == 0
    assert pw == len(POOL_WINDOWS) * POOL_GROUP and w_in.shape[2] == pw + 2 * aw + vw + gw

    x2 = x.reshape(M, D)
    row = lambda width: pl.BlockSpec((tm, width), lambda i: (i, 0))

    bd = jnp.asarray(np.kron(np.eye(MXU_DIM // HEAD_DIM), np.ones((HEAD_DIM, HEAD_DIM))), BF16)
    reps = aw // HEAD_DIM
    u, q, k, v, gates = pl.pallas_call(
        functools.partial(_inproj_kernel, widths=(pw, aw, vw, gw), scale=LOG2E / math.sqrt(HEAD_DIM)),
        grid=(M // tm,),
        in_specs=[row(D), _const_spec((1, D)), _const_spec(w_in.shape[1:]), _const_spec((1, gw)),
                  _const_spec((1, aw)), _const_spec((1, aw)), _const_spec((MXU_DIM, MXU_DIM))],
        out_specs=[row(pw), row(aw), row(aw), row(vw), row(gw)],
        out_shape=[jax.ShapeDtypeStruct((M, pw), F32), jax.ShapeDtypeStruct((M, aw), BF16),
                   jax.ShapeDtypeStruct((M, aw), BF16), jax.ShapeDtypeStruct((M, vw), BF16),
                   jax.ShapeDtypeStruct((M, gw), BF16)],
        compiler_params=_params(1),
        name="inproj",
    )(x2, norm1_g, w_in[0].astype(BF16), gate_b, jnp.tile(q_norm_g, (1, reps)),
      jnp.tile(k_norm_g, (1, reps)), bd)

    qcols, kcols, slopes2 = _alibi_columns(ta)

    bound = (1.01 * LOG2E * math.sqrt(HEAD_DIM)) * jnp.max(jnp.abs(q_norm_g)) * jnp.max(jnp.abs(k_norm_g))
    bounded = bound <= SCORE_BOUND_MAX
    reach = jnp.floor(((bound - EXP2_ZERO_BELOW) / jnp.asarray(slopes2, F32) - 1.0) / ta) + 1.0
    reach = jnp.clip(jnp.where(bounded, reach, S // ta), 0, S // ta)
    plan = jnp.concatenate([bounded.astype(jnp.int32)[None], reach.astype(jnp.int32)])

    n_tiles = S // ta
    assert n_tiles % 2 == 0
    head_seq = pl.BlockSpec((1, S, LANES), lambda b, h: (b, 0, h))
    head_cols = pl.BlockSpec((1, ta, LANES), lambda b, h: (h, 0, 0))
    attn = pl.pallas_call(
        functools.partial(_attn_kernel, tile=ta, n_tiles=n_tiles),
        grid=(B, N_HEADS),
        in_specs=[pl.BlockSpec(memory_space=pltpu.SMEM), head_seq, head_seq, head_seq,
                  head_cols, head_cols,
                  _const_spec((1, HEAD_DIM)), _const_spec((1, HEAD_DIM)),
                  _const_spec((1, HEAD_DIM)), _const_spec((1, HEAD_DIM)), _const_spec((1, V_DIM))],
        out_specs=head_seq,
        out_shape=jax.ShapeDtypeStruct((B, S, vw), BF16),
        scratch_shapes=[pltpu.VMEM((2, 2, ta, ta), BF16),
                        pltpu.VMEM((2, 1, ta), F32),
                        pltpu.VMEM((n_tiles + 1, 2, 1, ta), F32),
                        pltpu.VMEM((n_tiles + 1, 2, V_DIM, ta), F32)],
        compiler_params=_params(2),
        name="diffattn",
    )(plan, q.reshape(B, S, aw), k.reshape(B, S, aw), v.reshape(B, S, vw), qcols, kcols,
      lambda_q1, lambda_k1, lambda_q2, lambda_k2, subln_g)

    x_mid = pl.pallas_call(
        functools.partial(_merge_kernel, seq=S, tm=tm),
        grid=(M // tm,),
        in_specs=[row(D), row(pw),
                  pl.BlockSpec((HALO, pw), lambda i: (jnp.maximum(i * (tm // HALO) - 1, 0), 0)),
                  row(vw), row(gw), _const_spec(pool_w.shape[1:]), _const_spec((1, pw)),
                  _const_spec((pw, D)), _const_spec((vw, D)), _const_spec((D, D))],
        out_specs=row(D),
        out_shape=jax.ShapeDtypeStruct((M, D), F32),
        scratch_shapes=[pltpu.VMEM((HALO + tm, pw), F32)],
        compiler_params=_params(1),
        name="merge",
    )(x2, u, u, attn.reshape(M, vw), gates, pool_w[0].astype(BF16), pool_scale,
      w_branch_a[0].astype(BF16), w_branch_b[0].astype(BF16), w_out[0].astype(BF16))

    out = pl.pallas_call(
        functools.partial(_ffn_kernel, chunk=1024),
        grid=(M // tm,),
        in_specs=[row(D), _const_spec((1, D)), _const_spec((D, d_ff)), _const_spec((d_ff, D))],
        out_specs=row(D),
        out_shape=jax.ShapeDtypeStruct((M, D), F32),
        compiler_params=_params(1),
        name="ffn",
    )(x_mid, norm2_g, w_ff1[0].astype(BF16), w_ff2[0].astype(BF16))
    return out.reshape(B, S, D)
```

```python
import functools
import math

import numpy as np
import jax
import jax.numpy as jnp
from jax import lax
from jax.experimental import pallas as pl
from jax.experimental.pallas import tpu as pltpu

F32 = jnp.float32
BF16 = jnp.bfloat16
EPS = 1e-6

POOL_WINDOWS = (2, 4, 8, 16)
POOL_GROUP = 128
HALO = max(POOL_WINDOWS)
N_HEADS = 4
HEAD_DIM = 64
V_DIM = 2 * HEAD_DIM
LANES = 128
MXU_DIM = 256
LAMBDA_INIT = 0.8 - 0.6 * math.exp(-0.3 * 0)
NEG_BIG = -1e30
LOG2E = math.log2(math.e)
BIAS_TERMS = 3
SCORE_BOUND_MAX = 60.0
EXP2_ZERO_BELOW = -150.0

ROW_TILE = 512
MERGE_CHUNK = 256
ATTN_TILE = 512
BELOW_UNROLL = 4
VMEM_LIMIT = 56 * 1024 * 1024


def _dot(a, b):
    return jnp.dot(a, b, preferred_element_type=F32)


def _sigmoid(x):
    return 1.0 / (1.0 + jnp.exp(-x))


def _inproj_kernel(x_ref, g1_ref, w_ref, gb_ref, gq_ref, gk_ref, bd_ref,
                   u_ref, q_ref, k_ref, v_ref, g_ref, *, widths, scale):
    pw, aw, vw, gw = widths
    x = x_ref[...]
    ms = jnp.mean(x * x, axis=-1, keepdims=True)
    h = (x * lax.rsqrt(ms + EPS) * g1_ref[...]).astype(BF16)

    def proj(lo, width):
        return _dot(h, w_ref[:, lo:lo + width])

    bd = bd_ref[...]

    def group_rms(y):
        y2 = y * y
        hi = y2.astype(BF16)
        lo = (y2 - hi.astype(F32)).astype(BF16)
        parts = []
        for c in range(y.shape[1] // MXU_DIM):
            sl = slice(c * MXU_DIM, (c + 1) * MXU_DIM)
            parts.append(_dot(hi[:, sl], bd) + _dot(lo[:, sl], bd))
        ss = jnp.concatenate(parts, axis=1)
        return y * lax.rsqrt(ss * (1.0 / HEAD_DIM) + EPS)

    u_ref[...] = proj(0, pw)
    q_ref[...] = (group_rms(proj(pw, aw)) * gq_ref[...] * scale).astype(BF16)
    k_ref[...] = (group_rms(proj(pw + aw, aw)) * gk_ref[...]).astype(BF16)
    v_ref[...] = proj(pw + 2 * aw, vw).astype(BF16)
    g0 = pw + 2 * aw + vw
    chunk = 512
    for c in range(gw // chunk):
        gl = proj(g0 + c * chunk, chunk) + gb_ref[:, c * chunk:(c + 1) * chunk]
        g_ref[:, c * chunk:(c + 1) * chunk] = _sigmoid(gl).astype(BF16)


def _attn_kernel(plan_ref, q_ref, k_ref, v_ref, qc_ref, kc_ref, lq1_ref, lk1_ref, lq2_ref, lk2_ref,
                 sg_ref, o_ref, p_ref, m_ref, l_ref, acc_ref, *, tile, n_tiles):
    h = pl.program_id(1)
    bounded = plan_ref[0] == 1
    n_reach = plan_ref[1 + h]
    spare = n_tiles

    lane = lax.broadcasted_iota(jnp.int32, (tile, LANES), 1)
    qcols = qc_ref[0]
    kcols = kc_ref[0]
    lane1 = lax.broadcasted_iota(jnp.int32, (1, LANES), 1)
    dist_lane = (lane1 >= 2 * BIAS_TERMS) & (lane1 < 3 * BIAS_TERMS)

    row = lax.broadcasted_iota(jnp.int32, (tile, tile), 0)
    col = lax.broadcasted_iota(jnp.int32, (tile, tile), 1)
    future = row > col

    def scores(qi, j, masked):
        q = q_ref[0, pl.ds(pl.multiple_of(qi * tile, tile), tile), :]
        zero = jnp.zeros_like(q)
        dist = jnp.where(dist_lane, jnp.asarray(j - qi).astype(F32), 0.0).astype(BF16)
        ka = jnp.concatenate([k_ref[0, pl.ds(pl.multiple_of(j * tile, tile), tile), :], kcols + dist],
                             axis=1)
        out = []
        for c in range(2):
            qc = jnp.where(lane >= HEAD_DIM if c else lane < HEAD_DIM, q, zero)
            qa = jnp.concatenate([qc, qcols], axis=1)
            s = lax.dot_general(ka, qa, (((1,), (1,)), ((), ())), preferred_element_type=F32)
            out.append(jnp.where(future, NEG_BIG, s) if masked else s)
        return out

    def values_t_probs(j, p):
        start = pl.multiple_of(j * tile, tile)
        return lax.dot_general(v_ref[0, pl.ds(start, tile), :], p, (((0,), (0,)), ((), ())),
                               preferred_element_type=F32)

    l_ref[...] = jnp.zeros(l_ref.shape, F32)
    acc_ref[...] = jnp.zeros(acc_ref.shape, F32)

    @pl.when(bounded)
    def _():
        def probs(item, slot, masked):
            qi, j, state = item
            for c, s in enumerate(scores(qi, j, masked)):
                p = jnp.exp2(s)
                l_ref[state, c] += jnp.sum(p, axis=0, keepdims=True)
                p_ref[slot, c] = p.astype(BF16)

        def accumulate(item, slot):
            _, j, state = item
            for c in range(2):
                acc_ref[state, c] += values_t_probs(j, p_ref[slot, c])

        def diagonal(i):
            return i, i, i

        def below(cursor):
            d, qi = cursor
            live = d <= reach
            return jnp.where(live, qi, 0), jnp.where(live, qi - d, 0), jnp.where(live, qi, spare)

        def advance(cursor):
            d, qi = cursor
            wrap = qi + 1 >= n_tiles
            d = jnp.where(wrap, d + 1, d)
            return d, jnp.where(wrap, d, qi + 1)

        reach = jnp.minimum(n_reach, n_tiles - 1)
        n_below = reach * n_tiles - lax.shift_right_logical(reach * (reach + 1), 1)

        probs(diagonal(0), 0, True)

        def diagonal_pair(t, carry):
            i = 2 * t
            probs(diagonal(i + 1), 1, True)
            accumulate(diagonal(i), 0)
            probs(diagonal(i + 2), 0, True)
            accumulate(diagonal(i + 1), 1)
            return carry

        lax.fori_loop(0, n_tiles // 2 - 1, diagonal_pair, 0)
        first = (jnp.int32(1), jnp.int32(1))
        probs(diagonal(n_tiles - 1), 1, True)
        accumulate(diagonal(n_tiles - 2), 0)
        probs(below(first), 0, False)
        accumulate(diagonal(n_tiles - 1), 1)

        def below_group(t, cursor):
            for u in range(BELOW_UNROLL):
                nxt = advance(cursor)
                probs(below(nxt), (u + 1) % 2, False)
                accumulate(below(cursor), u % 2)
                cursor = nxt
            return cursor

        lax.fori_loop(0, (n_below + BELOW_UNROLL - 1) // BELOW_UNROLL, below_group, first)

    @pl.when(jnp.logical_not(bounded))
    def _():
        def block(qi, j, masked):
            for c, s in enumerate(scores(qi, j, masked)):
                m_old = m_ref[c]
                m_new = jnp.maximum(m_old, jnp.max(s, axis=0, keepdims=True))
                p = jnp.exp2(s - m_new)
                alpha = jnp.exp2(m_old - m_new)
                l_ref[qi, c] = alpha * l_ref[qi, c] + jnp.sum(p, axis=0, keepdims=True)
                acc_ref[qi, c] = alpha * acc_ref[qi, c] + values_t_probs(j, p.astype(BF16))
                m_ref[c] = m_new

        def query_tile(qi, carry):
            m_ref[...] = jnp.full(m_ref.shape, NEG_BIG, F32)
            block(qi, qi, True)

            def body(t, c):
                block(qi, qi - 1 - t, False)
                return c

            lax.fori_loop(0, qi, body, 0)
            return carry

        lax.fori_loop(0, n_tiles, query_tile, 0)

    lam = (jnp.exp(jnp.sum(lq1_ref[...] * lk1_ref[...], axis=-1, keepdims=True))
           - jnp.exp(jnp.sum(lq2_ref[...] * lk2_ref[...], axis=-1, keepdims=True))
           + LAMBDA_INIT)

    def finish(qi, carry):
        o = acc_ref[qi, 0] * (1.0 / l_ref[qi, 0]) - acc_ref[qi, 1] * (lam / l_ref[qi, 1])
        msq = jnp.mean(o * o, axis=0, keepdims=True)
        y = o * lax.rsqrt(msq + EPS) * (sg_ref[...] * (1.0 - LAMBDA_INIT))
        o_ref[0, pl.ds(pl.multiple_of(qi * tile, tile), tile), :] = y.astype(BF16).T
        return carry

    lax.fori_loop(0, n_tiles, finish, 0)


def _merge_kernel(x_ref, u_ref, halo_ref, a_ref, g_ref, pw_ref, ps_ref, wa_ref, wb_ref, wo_ref,
                  o_ref, ext_ref, *, seq, tm, chunk):
    i = pl.program_id(0)
    t0 = (i * tm) % seq
    halo = halo_ref[...]
    ext_ref[0:HALO, :] = jnp.where(t0 == 0, jnp.zeros_like(halo), halo)
    ext_ref[HALO:, :] = u_ref[...]
    d = o_ref.shape[1]

    for r0 in range(0, tm, chunk):
        rows = slice(r0, r0 + chunk)
        t = t0 + r0 + lax.broadcasted_iota(jnp.int32, (chunk, 1), 0)
        mixed = []
        for g, w in enumerate(POOL_WINDOWS):
            y = ext_ref[r0:r0 + chunk + HALO, g * POOL_GROUP:(g + 1) * POOL_GROUP]
            tot = y
            span = 1
            while span < w:
                tot = tot + pltpu.roll(tot, span, axis=0)
                span *= 2
            cnt = jnp.minimum(t + 1, w).astype(F32)
            pooled = tot[HALO:] / cnt - y[HALO:]
            mixed.append(_dot(pooled.astype(BF16), pw_ref[g]))
        mixed = jnp.concatenate(mixed, axis=1) * ps_ref[...]
        ya = _dot(mixed.astype(BF16), wa_ref[...])
        yb = _dot(a_ref[rows, :], wb_ref[...])
        merged = g_ref[rows, 0:d].astype(F32) * ya + g_ref[rows, d:2 * d].astype(F32) * yb
        o_ref[rows, :] = x_ref[rows, :] + _dot(merged.astype(BF16), wo_ref[...])


def _ffn_kernel(x_ref, g2_ref, w1_ref, w2_ref, o_ref, *, chunk):
    x = x_ref[...]
    ms = jnp.mean(x * x, axis=-1, keepdims=True)
    h = (x * lax.rsqrt(ms + EPS) * g2_ref[...]).astype(BF16)
    acc = x
    for c in range(w1_ref.shape[1] // chunk):
        a = jnp.maximum(_dot(h, w1_ref[:, c * chunk:(c + 1) * chunk]), 0.0)
        acc = acc + _dot((a * a).astype(BF16), w2_ref[c * chunk:(c + 1) * chunk, :])
    o_ref[...] = acc


def _alibi_columns(tile):
    slopes2 = np.array([2.0 ** (-8.0 * (h + 1) / N_HEADS) for h in range(N_HEADS)]) * LOG2E
    pos = np.arange(tile, dtype=np.float64)

    def terms(x):
        out = []
        for _ in range(BIAS_TERMS):
            part = x.astype(BF16)
            out.append(part)
            x = x - part.astype(np.float64)
        return out

    kcols = np.zeros((N_HEADS, tile, LANES), BF16)
    qcols = np.zeros((N_HEADS, tile, LANES), BF16)
    k_terms = terms(slopes2[:, None] * pos[None, :])
    q_terms = terms(-slopes2[:, None] * pos[None, :])
    t_terms = terms(np.broadcast_to(slopes2[:, None] * tile, (N_HEADS, tile)))
    for t in range(BIAS_TERMS):
        kcols[:, :, t] = k_terms[t]
        kcols[:, :, BIAS_TERMS + t] = 1.0
        qcols[:, :, t] = 1.0
        qcols[:, :, BIAS_TERMS + t] = q_terms[t]
        qcols[:, :, 2 * BIAS_TERMS + t] = t_terms[t]
    return jnp.asarray(qcols), jnp.asarray(kcols), slopes2.astype(np.float32)


def _const_spec(shape):
    return pl.BlockSpec(shape, lambda *_: (0,) * len(shape))


def _params(n_axes):
    return pltpu.CompilerParams(dimension_semantics=("arbitrary",) * n_axes,
                                vmem_limit_bytes=VMEM_LIMIT)


def kernel(x, norm1_g, w_in, gate_b, pool_w, pool_scale, q_norm_g, k_norm_g, lambda_q1, lambda_k1,
           lambda_q2, lambda_k2, subln_g, w_branch_a, w_branch_b, w_out, norm2_g, w_ff1, w_ff2):
    B, S, D = x.shape
    depth = norm1_g.shape[0]
    assert depth == 1
    M = B * S
    pw = pool_scale.shape[1]
    aw = 2 * N_HEADS * HEAD_DIM
    vw = N_HEADS * V_DIM
    gw = 2 * D
    d_ff = w_ff1.shape[2]
    tm = ROW_TILE
    ta = ATTN_TILE
    assert S % tm == 0 and S % ta == 0 and tm % HALO == 0
    assert pw == len(POOL_WINDOWS) * POOL_GROUP and w_in.shape[2] == pw + 2 * aw + vw + gw

    x2 = x.reshape(M, D)
    row = lambda width: pl.BlockSpec((tm, width), lambda i: (i, 0))

    bd = jnp.asarray(np.kron(np.eye(MXU_DIM // HEAD_DIM), np.ones((HEAD_DIM, HEAD_DIM))), BF16)
    reps = aw // HEAD_DIM
    u, q, k, v, gates = pl.pallas_call(
        functools.partial(_inproj_kernel, widths=(pw, aw, vw, gw), scale=LOG2E / math.sqrt(HEAD_DIM)),
        grid=(M // tm,),
        in_specs=[row(D), _const_spec((1, D)), _const_spec(w_in.shape[1:]), _const_spec((1, gw)),
                  _const_spec((1, aw)), _const_spec((1, aw)), _const_spec((MXU_DIM, MXU_DIM))],
        out_specs=[row(pw), row(aw), row(aw), row(vw), row(gw)],
        out_shape=[jax.ShapeDtypeStruct((M, pw), F32), jax.ShapeDtypeStruct((M, aw), BF16),
                   jax.ShapeDtypeStruct((M, aw), BF16), jax.ShapeDtypeStruct((M, vw), BF16),
                   jax.ShapeDtypeStruct((M, gw), BF16)],
        compiler_params=_params(1),
        name="inproj",
    )(x2, norm1_g, w_in[0].astype(BF16), gate_b, jnp.tile(q_norm_g, (1, reps)),
      jnp.tile(k_norm_g, (1, reps)), bd)

    qcols, kcols, slopes2 = _alibi_columns(ta)

    bound = (1.01 * LOG2E * math.sqrt(HEAD_DIM)) * jnp.max(jnp.abs(q_norm_g)) * jnp.max(jnp.abs(k_norm_g))
    bounded = bound <= SCORE_BOUND_MAX
    reach = jnp.floor(((bound - EXP2_ZERO_BELOW) / jnp.asarray(slopes2, F32) - 1.0) / ta) + 1.0
    reach = jnp.clip(jnp.where(bounded, reach, S // ta), 0, S // ta)
    plan = jnp.concatenate([bounded.astype(jnp.int32)[None], reach.astype(jnp.int32)])

    n_tiles = S // ta
    assert n_tiles % 2 == 0
    head_seq = pl.BlockSpec((1, S, LANES), lambda b, h: (b, 0, h))
    head_cols = pl.BlockSpec((1, ta, LANES), lambda b, h: (h, 0, 0))
    attn = pl.pallas_call(
        functools.partial(_attn_kernel, tile=ta, n_tiles=n_tiles),
        grid=(B, N_HEADS),
        in_specs=[pl.BlockSpec(memory_space=pltpu.SMEM), head_seq, head_seq, head_seq,
                  head_cols, head_cols,
                  _const_spec((1, HEAD_DIM)), _const_spec((1, HEAD_DIM)),
                  _const_spec((1, HEAD_DIM)), _const_spec((1, HEAD_DIM)), _const_spec((V_DIM, 1))],
        out_specs=head_seq,
        out_shape=jax.ShapeDtypeStruct((B, S, vw), BF16),
        scratch_shapes=[pltpu.VMEM((2, 2, ta, ta), BF16),
                        pltpu.VMEM((2, 1, ta), F32),
                        pltpu.VMEM((n_tiles + 1, 2, 1, ta), F32),
                        pltpu.VMEM((n_tiles + 1, 2, V_DIM, ta), F32)],
        compiler_params=_params(2),
        name="diffattn",
    )(plan, q.reshape(B, S, aw), k.reshape(B, S, aw), v.reshape(B, S, vw), qcols, kcols,
      lambda_q1, lambda_k1, lambda_q2, lambda_k2, subln_g.reshape(V_DIM, 1))

    x_mid = pl.pallas_call(
        functools.partial(_merge_kernel, seq=S, tm=tm, chunk=MERGE_CHUNK),
        grid=(M // tm,),
        in_specs=[row(D), row(pw),
                  pl.BlockSpec((HALO, pw), lambda i: (jnp.maximum(i * (tm // HALO) - 1, 0), 0)),
                  row(vw), row(gw), _const_spec(pool_w.shape[1:]), _const_spec((1, pw)),
                  _const_spec((pw, D)), _const_spec((vw, D)), _const_spec((D, D))],
        out_specs=row(D),
        out_shape=jax.ShapeDtypeStruct((M, D), F32),
        scratch_shapes=[pltpu.VMEM((HALO + tm, pw), F32)],
        compiler_params=_params(1),
        name="merge",
    )(x2, u, u, attn.reshape(M, vw), gates, pool_w[0].astype(BF16), pool_scale,
      w_branch_a[0].astype(BF16), w_branch_b[0].astype(BF16), w_out[0].astype(BF16))

    out = pl.pallas_call(
        functools.partial(_ffn_kernel, chunk=1024),
        grid=(M // tm,),
        in_specs=[row(D), _const_spec((1, D)), _const_spec((D, d_ff)), _const_spec((d_ff, D))],
        out_specs=row(D),
        out_shape=jax.ShapeDtypeStruct((M, D), F32),
        compiler_params=_params(1),
        name="ffn",
    )(x_mid, norm2_g, w_ff1[0].astype(BF16), w_ff2[0].astype(BF16))
    return out.reshape(B, S, D)
```

```python
import functools
import math

import numpy as np
import jax
import jax.numpy as jnp
from jax import lax
from jax.experimental import pallas as pl
from jax.experimental.pallas import tpu as pltpu

F32 = jnp.float32
BF16 = jnp.bfloat16
EPS = 1e-6

POOL_WINDOWS = (2, 4, 8, 16)
POOL_GROUP = 128
HALO = max(POOL_WINDOWS)
N_HEADS = 4
HEAD_DIM = 64
V_DIM = 2 * HEAD_DIM
LANES = 128
MXU_DIM = 256
LAMBDA_INIT = 0.8 - 0.6 * math.exp(-0.3 * 0)
NEG_BIG = -1e30
LOG2E = math.log2(math.e)
BIAS_TERMS = 3
SCORE_BOUND_MAX = 60.0
EXP2_ZERO_BELOW = -150.0

ROW_TILE = 512
FFN_CHUNK = 1024
MERGE_CHUNK = 256
ATTN_TILE = 512
BELOW_UNROLL = 4
VMEM_LIMIT = 56 * 1024 * 1024


def _dot(a, b):
    return jnp.dot(a, b, preferred_element_type=F32)


def _sigmoid(x):
    return 1.0 / (1.0 + jnp.exp(-x))


def _inproj_kernel(x_ref, g1_ref, w_ref, gb_ref, gq_ref, gk_ref, bd_ref,
                   u_ref, q_ref, k_ref, v_ref, g_ref, wbf_ref, *, widths, scale):
    pw, aw, vw, gw = widths

    @pl.when(pl.program_id(0) == 0)
    def _():
        wbf_ref[...] = w_ref[...].astype(BF16)

    x = x_ref[...]
    ms = jnp.mean(x * x, axis=-1, keepdims=True)
    h = (x * lax.rsqrt(ms + EPS) * g1_ref[...]).astype(BF16)

    def proj(lo, width):
        return _dot(h, wbf_ref[:, lo:lo + width])

    bd = bd_ref[...]

    def group_rms(y):
        y2 = y * y
        hi = y2.astype(BF16)
        lo = (y2 - hi.astype(F32)).astype(BF16)
        parts = []
        for c in range(y.shape[1] // MXU_DIM):
            sl = slice(c * MXU_DIM, (c + 1) * MXU_DIM)
            parts.append(_dot(hi[:, sl], bd) + _dot(lo[:, sl], bd))
        ss = jnp.concatenate(parts, axis=1)
        return y * lax.rsqrt(ss * (1.0 / HEAD_DIM) + EPS)

    u_ref[...] = proj(0, pw)
    q_ref[...] = (group_rms(proj(pw, aw)) * gq_ref[...] * scale).astype(BF16)
    k_ref[...] = (group_rms(proj(pw + aw, aw)) * gk_ref[...]).astype(BF16)
    v_ref[...] = proj(pw + 2 * aw, vw).astype(BF16)
    g0 = pw + 2 * aw + vw
    chunk = 512
    for c in range(gw // chunk):
        gl = proj(g0 + c * chunk, chunk) + gb_ref[:, c * chunk:(c + 1) * chunk]
        g_ref[:, c * chunk:(c + 1) * chunk] = _sigmoid(gl).astype(BF16)


def _attn_kernel(plan_ref, q_ref, k_ref, v_ref, qc_ref, kc_ref, lq1_ref, lk1_ref, lq2_ref, lk2_ref,
                 sg_ref, o_ref, p_ref, m_ref, l_ref, acc_ref, *, tile, n_tiles):
    h = pl.program_id(1)
    bounded = plan_ref[0] == 1
    n_reach = plan_ref[1 + h]
    spare = n_tiles

    lane = lax.broadcasted_iota(jnp.int32, (tile, LANES), 1)
    qcols = qc_ref[0]
    kcols = kc_ref[0]
    lane1 = lax.broadcasted_iota(jnp.int32, (1, LANES), 1)
    dist_lane = (lane1 >= 2 * BIAS_TERMS) & (lane1 < 3 * BIAS_TERMS)

    row = lax.broadcasted_iota(jnp.int32, (tile, tile), 0)
    col = lax.broadcasted_iota(jnp.int32, (tile, tile), 1)
    future = row > col

    def scores(qi, j, masked):
        q = q_ref[0, pl.ds(pl.multiple_of(qi * tile, tile), tile), :]
        zero = jnp.zeros_like(q)
        dist = jnp.where(dist_lane, jnp.asarray(j - qi).astype(F32), 0.0).astype(BF16)
        ka = jnp.concatenate([k_ref[0, pl.ds(pl.multiple_of(j * tile, tile), tile), :], kcols + dist],
                             axis=1)
        out = []
        for c in range(2):
            qc = jnp.where(lane >= HEAD_DIM if c else lane < HEAD_DIM, q, zero)
            qa = jnp.concatenate([qc, qcols], axis=1)
            s = lax.dot_general(ka, qa, (((1,), (1,)), ((), ())), preferred_element_type=F32)
            out.append(jnp.where(future, NEG_BIG, s) if masked else s)
        return out

    def values_t_probs(j, p):
        start = pl.multiple_of(j * tile, tile)
        return lax.dot_general(v_ref[0, pl.ds(start, tile), :], p, (((0,), (0,)), ((), ())),
                               preferred_element_type=F32)

    l_ref[...] = jnp.zeros(l_ref.shape, F32)
    acc_ref[...] = jnp.zeros(acc_ref.shape, F32)

    @pl.when(bounded)
    def _():
        def probs(item, slot, masked):
            qi, j, state = item
            for c, s in enumerate(scores(qi, j, masked)):
                p = jnp.exp2(s)
                l_ref[state, c] += jnp.sum(p, axis=0, keepdims=True)
                p_ref[slot, c] = p.astype(BF16)

        def accumulate(item, slot):
            _, j, state = item
            for c in range(2):
                acc_ref[state, c] += values_t_probs(j, p_ref[slot, c])

        def diagonal(i):
            return i, i, i

        def below(cursor):
            d, qi = cursor
            live = d <= reach
            return jnp.where(live, qi, 0), jnp.where(live, qi - d, 0), jnp.where(live, qi, spare)

        def advance(cursor):
            d, qi = cursor
            wrap = qi + 1 >= n_tiles
            d = jnp.where(wrap, d + 1, d)
            return d, jnp.where(wrap, d, qi + 1)

        reach = jnp.minimum(n_reach, n_tiles - 1)
        n_below = reach * n_tiles - lax.shift_right_logical(reach * (reach + 1), 1)

        probs(diagonal(0), 0, True)

        def diagonal_pair(t, carry):
            i = 2 * t
            probs(diagonal(i + 1), 1, True)
            accumulate(diagonal(i), 0)
            probs(diagonal(i + 2), 0, True)
            accumulate(diagonal(i + 1), 1)
            return carry

        lax.fori_loop(0, n_tiles // 2 - 1, diagonal_pair, 0)
        first = (jnp.int32(1), jnp.int32(1))
        probs(diagonal(n_tiles - 1), 1, True)
        accumulate(diagonal(n_tiles - 2), 0)
        probs(below(first), 0, False)
        accumulate(diagonal(n_tiles - 1), 1)

        def below_group(t, cursor):
            for u in range(BELOW_UNROLL):
                nxt = advance(cursor)
                probs(below(nxt), (u + 1) % 2, False)
                accumulate(below(cursor), u % 2)
                cursor = nxt
            return cursor

        lax.fori_loop(0, (n_below + BELOW_UNROLL - 1) // BELOW_UNROLL, below_group, first)

    @pl.when(jnp.logical_not(bounded))
    def _():
        def block(qi, j, masked):
            for c, s in enumerate(scores(qi, j, masked)):
                m_old = m_ref[c]
                m_new = jnp.maximum(m_old, jnp.max(s, axis=0, keepdims=True))
                p = jnp.exp2(s - m_new)
                alpha = jnp.exp2(m_old - m_new)
                l_ref[qi, c] = alpha * l_ref[qi, c] + jnp.sum(p, axis=0, keepdims=True)
                acc_ref[qi, c] = alpha * acc_ref[qi, c] + values_t_probs(j, p.astype(BF16))
                m_ref[c] = m_new

        def query_tile(qi, carry):
            m_ref[...] = jnp.full(m_ref.shape, NEG_BIG, F32)
            block(qi, qi, True)

            def body(t, c):
                block(qi, qi - 1 - t, False)
                return c

            lax.fori_loop(0, qi, body, 0)
            return carry

        lax.fori_loop(0, n_tiles, query_tile, 0)

    lam = (jnp.exp(jnp.sum(lq1_ref[...] * lk1_ref[...], axis=-1, keepdims=True))
           - jnp.exp(jnp.sum(lq2_ref[...] * lk2_ref[...], axis=-1, keepdims=True))
           + LAMBDA_INIT)

    def finish(qi, carry):
        o = acc_ref[qi, 0] * (1.0 / l_ref[qi, 0]) - acc_ref[qi, 1] * (lam / l_ref[qi, 1])
        msq = jnp.mean(o * o, axis=0, keepdims=True)
        y = o * lax.rsqrt(msq + EPS) * (sg_ref[...] * (1.0 - LAMBDA_INIT))
        o_ref[0, pl.ds(pl.multiple_of(qi * tile, tile), tile), :] = y.astype(BF16).T
        return carry

    lax.fori_loop(0, n_tiles, finish, 0)


def _merge_kernel(x_ref, u_ref, halo_ref, a_ref, g_ref, pw32_ref, ps_ref, wa32_ref, wb32_ref, wo32_ref,
                  o_ref, ext_ref, pw_ref, wa_ref, wb_ref, wo_ref, *, seq, tm, chunk):
    i = pl.program_id(0)

    @pl.when(i == 0)
    def _():
        for src, dst in ((pw32_ref, pw_ref), (wa32_ref, wa_ref), (wb32_ref, wb_ref), (wo32_ref, wo_ref)):
            dst[...] = src[...].astype(BF16)

    t0 = (i * tm) % seq
    halo = halo_ref[...]
    ext_ref[0:HALO, :] = jnp.where(t0 == 0, jnp.zeros_like(halo), halo)
    ext_ref[HALO:, :] = u_ref[...]
    d = o_ref.shape[1]

    for r0 in range(0, tm, chunk):
        rows = slice(r0, r0 + chunk)
        t = t0 + r0 + lax.broadcasted_iota(jnp.int32, (chunk, 1), 0)
        mixed = []
        for g, w in enumerate(POOL_WINDOWS):
            y = ext_ref[r0:r0 + chunk + HALO, g * POOL_GROUP:(g + 1) * POOL_GROUP]
            tot = y
            span = 1
            while span < w:
                tot = tot + pltpu.roll(tot, span, axis=0)
                span *= 2
            cnt = jnp.minimum(t + 1, w).astype(F32)
            pooled = tot[HALO:] / cnt - y[HALO:]
            mixed.append(_dot(pooled.astype(BF16), pw_ref[g]))
        mixed = jnp.concatenate(mixed, axis=1) * ps_ref[...]
        ya = _dot(mixed.astype(BF16), wa_ref[...])
        yb = _dot(a_ref[rows, :], wb_ref[...])
        merged = g_ref[rows, 0:d].astype(F32) * ya + g_ref[rows, d:2 * d].astype(F32) * yb
        o_ref[rows, :] = x_ref[rows, :] + _dot(merged.astype(BF16), wo_ref[...])


def _ffn_kernel(x_ref, g2_ref, w1_hbm, w2_hbm, o_ref, w1_ref, w2_ref, stage_ref, sem, *, chunk):
    n1 = w1_ref.shape[1] // chunk
    n2 = w2_ref.shape[0] // chunk

    def weight_chunk(c):
        if c < n1:
            return w1_hbm.at[:, c * chunk:(c + 1) * chunk], w1_ref.at[:, c * chunk:(c + 1) * chunk]
        c -= n1
        return w2_hbm.at[c * chunk:(c + 1) * chunk, :], w2_ref.at[c * chunk:(c + 1) * chunk, :]

    def fetch(c):
        return pltpu.make_async_copy(weight_chunk(c)[0], stage_ref.at[c % 2], sem.at[c % 2])

    @pl.when(pl.program_id(0) == 0)
    def _():
        fetch(0).start()
        for c in range(n1 + n2):
            if c + 1 < n1 + n2:
                fetch(c + 1).start()
            fetch(c).wait()
            weight_chunk(c)[1][...] = stage_ref[c % 2].astype(BF16)

    x = x_ref[...]
    ms = jnp.mean(x * x, axis=-1, keepdims=True)
    h = (x * lax.rsqrt(ms + EPS) * g2_ref[...]).astype(BF16)
    acc = x
    for c in range(w1_ref.shape[1] // chunk):
        a = jnp.maximum(_dot(h, w1_ref[:, c * chunk:(c + 1) * chunk]), 0.0)
        acc = acc + _dot((a * a).astype(BF16), w2_ref[c * chunk:(c + 1) * chunk, :])
    o_ref[...] = acc


def _alibi_columns(tile):
    slopes2 = np.array([2.0 ** (-8.0 * (h + 1) / N_HEADS) for h in range(N_HEADS)]) * LOG2E
    pos = np.arange(tile, dtype=np.float64)

    def terms(x):
        out = []
        for _ in range(BIAS_TERMS):
            part = x.astype(BF16)
            out.append(part)
            x = x - part.astype(np.float64)
        return out

    kcols = np.zeros((N_HEADS, tile, LANES), BF16)
    qcols = np.zeros((N_HEADS, tile, LANES), BF16)
    k_terms = terms(slopes2[:, None] * pos[None, :])
    q_terms = terms(-slopes2[:, None] * pos[None, :])
    t_terms = terms(np.broadcast_to(slopes2[:, None] * tile, (N_HEADS, tile)))
    for t in range(BIAS_TERMS):
        kcols[:, :, t] = k_terms[t]
        kcols[:, :, BIAS_TERMS + t] = 1.0
        qcols[:, :, t] = 1.0
        qcols[:, :, BIAS_TERMS + t] = q_terms[t]
        qcols[:, :, 2 * BIAS_TERMS + t] = t_terms[t]
    return jnp.asarray(qcols), jnp.asarray(kcols), slopes2.astype(np.float32)


def _const_spec(shape):
    return pl.BlockSpec(shape, lambda *_: (0,) * len(shape))


def _resident_spec(shape):
    return pl.BlockSpec(shape, lambda *_: (0,) * len(shape), pipeline_mode=pl.Buffered(1))


def _params(n_axes):
    return pltpu.CompilerParams(dimension_semantics=("arbitrary",) * n_axes,
                                vmem_limit_bytes=VMEM_LIMIT)


def kernel(x, norm1_g, w_in, gate_b, pool_w, pool_scale, q_norm_g, k_norm_g, lambda_q1, lambda_k1,
           lambda_q2, lambda_k2, subln_g, w_branch_a, w_branch_b, w_out, norm2_g, w_ff1, w_ff2):
    B, S, D = x.shape
    depth = norm1_g.shape[0]
    assert depth == 1
    M = B * S
    pw = pool_scale.shape[1]
    aw = 2 * N_HEADS * HEAD_DIM
    vw = N_HEADS * V_DIM
    gw = 2 * D
    d_ff = w_ff1.shape[2]
    tm = ROW_TILE
    ta = ATTN_TILE
    assert S % tm == 0 and S % ta == 0 and tm % HALO == 0
    assert D == FFN_CHUNK and d_ff % FFN_CHUNK == 0
    assert pw == len(POOL_WINDOWS) * POOL_GROUP and w_in.shape[2] == pw + 2 * aw + vw + gw

    x2 = x.reshape(M, D)
    row = lambda width: pl.BlockSpec((tm, width), lambda i: (i, 0))

    bd = jnp.asarray(np.kron(np.eye(MXU_DIM // HEAD_DIM), np.ones((HEAD_DIM, HEAD_DIM))), BF16)
    reps = aw // HEAD_DIM
    u, q, k, v, gates = pl.pallas_call(
        functools.partial(_inproj_kernel, widths=(pw, aw, vw, gw), scale=LOG2E / math.sqrt(HEAD_DIM)),
        grid=(M // tm,),
        in_specs=[row(D), _const_spec((1, D)), _resident_spec(w_in.shape[1:]), _const_spec((1, gw)),
                  _const_spec((1, aw)), _const_spec((1, aw)), _const_spec((MXU_DIM, MXU_DIM))],
        out_specs=[row(pw), row(aw), row(aw), row(vw), row(gw)],
        out_shape=[jax.ShapeDtypeStruct((M, pw), F32), jax.ShapeDtypeStruct((M, aw), BF16),
                   jax.ShapeDtypeStruct((M, aw), BF16), jax.ShapeDtypeStruct((M, vw), BF16),
                   jax.ShapeDtypeStruct((M, gw), BF16)],
        scratch_shapes=[pltpu.VMEM(w_in.shape[1:], BF16)],
        compiler_params=_params(1),
        name="inproj",
    )(x2, norm1_g, w_in[0], gate_b, jnp.tile(q_norm_g, (1, reps)),
      jnp.tile(k_norm_g, (1, reps)), bd)

    qcols, kcols, slopes2 = _alibi_columns(ta)

    bound = (1.01 * LOG2E * math.sqrt(HEAD_DIM)) * jnp.max(jnp.abs(q_norm_g)) * jnp.max(jnp.abs(k_norm_g))
    bounded = bound <= SCORE_BOUND_MAX
    reach = jnp.floor(((bound - EXP2_ZERO_BELOW) / jnp.asarray(slopes2, F32) - 1.0) / ta) + 1.0
    reach = jnp.clip(jnp.where(bounded, reach, S // ta), 0, S // ta)
    plan = jnp.concatenate([bounded.astype(jnp.int32)[None], reach.astype(jnp.int32)])

    n_tiles = S // ta
    assert n_tiles % 2 == 0
    head_seq = pl.BlockSpec((1, S, LANES), lambda b, h: (b, 0, h))
    head_cols = pl.BlockSpec((1, ta, LANES), lambda b, h: (h, 0, 0))
    attn = pl.pallas_call(
        functools.partial(_attn_kernel, tile=ta, n_tiles=n_tiles),
        grid=(B, N_HEADS),
        in_specs=[pl.BlockSpec(memory_space=pltpu.SMEM), head_seq, head_seq, head_seq,
                  head_cols, head_cols,
                  _const_spec((1, HEAD_DIM)), _const_spec((1, HEAD_DIM)),
                  _const_spec((1, HEAD_DIM)), _const_spec((1, HEAD_DIM)), _const_spec((V_DIM, 1))],
        out_specs=head_seq,
        out_shape=jax.ShapeDtypeStruct((B, S, vw), BF16),
        scratch_shapes=[pltpu.VMEM((2, 2, ta, ta), BF16),
                        pltpu.VMEM((2, 1, ta), F32),
                        pltpu.VMEM((n_tiles + 1, 2, 1, ta), F32),
                        pltpu.VMEM((n_tiles + 1, 2, V_DIM, ta), F32)],
        compiler_params=_params(2),
        name="diffattn",
    )(plan, q.reshape(B, S, aw), k.reshape(B, S, aw), v.reshape(B, S, vw), qcols, kcols,
      lambda_q1, lambda_k1, lambda_q2, lambda_k2, subln_g.reshape(V_DIM, 1))

    x_mid = pl.pallas_call(
        functools.partial(_merge_kernel, seq=S, tm=tm, chunk=MERGE_CHUNK),
        grid=(M // tm,),
        in_specs=[row(D), row(pw),
                  pl.BlockSpec((HALO, pw), lambda i: (jnp.maximum(i * (tm // HALO) - 1, 0), 0)),
                  row(vw), row(gw), _resident_spec(pool_w.shape[1:]), _const_spec((1, pw)),
                  _resident_spec((pw, D)), _resident_spec((vw, D)), _resident_spec((D, D))],
        out_specs=row(D),
        out_shape=jax.ShapeDtypeStruct((M, D), F32),
        scratch_shapes=[pltpu.VMEM((HALO + tm, pw), F32), pltpu.VMEM(pool_w.shape[1:], BF16),
                        pltpu.VMEM((pw, D), BF16), pltpu.VMEM((vw, D), BF16), pltpu.VMEM((D, D), BF16)],
        compiler_params=_params(1),
        name="merge",
    )(x2, u, u, attn.reshape(M, vw), gates, pool_w[0], pool_scale,
      w_branch_a[0], w_branch_b[0], w_out[0])

    out = pl.pallas_call(
        functools.partial(_ffn_kernel, chunk=FFN_CHUNK),
        grid=(M // tm,),
        in_specs=[row(D), _const_spec((1, D)), pl.BlockSpec(memory_space=pl.ANY),
                  pl.BlockSpec(memory_space=pl.ANY)],
        out_specs=row(D),
        out_shape=jax.ShapeDtypeStruct((M, D), F32),
        scratch_shapes=[pltpu.VMEM((D, d_ff), BF16), pltpu.VMEM((d_ff, D), BF16),
                        pltpu.VMEM((2, D, FFN_CHUNK), F32),
                        pltpu.SemaphoreType.DMA((2,))],
        compiler_params=_params(1),
        name="ffn",
    )(x_mid, norm2_g, w_ff1[0], w_ff2[0])
    return out.reshape(B, S, D)
```

```python
import functools
import math

import numpy as np
import jax
import jax.numpy as jnp
from jax import lax
from jax.experimental import pallas as pl
from jax.experimental.pallas import tpu as pltpu

F32 = jnp.float32
BF16 = jnp.bfloat16
EPS = 1e-6

POOL_WINDOWS = (2, 4, 8, 16)
POOL_GROUP = 128
HALO = max(POOL_WINDOWS)
N_HEADS = 4
HEAD_DIM = 64
V_DIM = 2 * HEAD_DIM
LANES = 128
MXU_DIM = 256
LAMBDA_INIT = 0.8 - 0.6 * math.exp(-0.3 * 0)
NEG_BIG = -1e30
LOG2E = math.log2(math.e)
BIAS_TERMS = 3
SCORE_BOUND_MAX = 60.0
EXP2_ZERO_BELOW = -150.0

ROW_TILE = 512
FFN_CHUNK = 1024
MERGE_CHUNK = 256
ATTN_TILE = 512
BELOW_UNROLL = 4
VMEM_LIMIT = 56 * 1024 * 1024


def _dot(a, b):
    return jnp.dot(a, b, preferred_element_type=F32)


def _sigmoid(x):
    return 1.0 / (1.0 + jnp.exp(-x))


def _inproj_kernel(x_ref, g1_ref, w_ref, gb_ref, gq_ref, gk_ref, bd_ref,
                   u_ref, q_ref, k_ref, v_ref, g_ref, wbf_ref, *, widths, scale):
    pw, aw, vw, gw = widths

    @pl.when(pl.program_id(0) == 0)
    def _():
        wbf_ref[...] = w_ref[...].astype(BF16)

    x = x_ref[...]
    ms = jnp.mean(x * x, axis=-1, keepdims=True)
    h = (x * lax.rsqrt(ms + EPS) * g1_ref[...]).astype(BF16)

    def proj(lo, width):
        return _dot(h, wbf_ref[:, lo:lo + width])

    bd = bd_ref[...]

    def group_rms(y):
        y2 = y * y
        hi = y2.astype(BF16)
        lo = (y2 - hi.astype(F32)).astype(BF16)
        parts = []
        for c in range(y.shape[1] // MXU_DIM):
            sl = slice(c * MXU_DIM, (c + 1) * MXU_DIM)
            parts.append(_dot(hi[:, sl], bd) + _dot(lo[:, sl], bd))
        ss = jnp.concatenate(parts, axis=1)
        return y * lax.rsqrt(ss * (1.0 / HEAD_DIM) + EPS)

    u_ref[...] = proj(0, pw)
    q_ref[...] = (group_rms(proj(pw, aw)) * gq_ref[...] * scale).astype(BF16)
    k_ref[...] = (group_rms(proj(pw + aw, aw)) * gk_ref[...]).astype(BF16)
    v_ref[...] = proj(pw + 2 * aw, vw).astype(BF16)
    g0 = pw + 2 * aw + vw
    chunk = 512
    for c in range(gw // chunk):
        gl = proj(g0 + c * chunk, chunk) + gb_ref[:, c * chunk:(c + 1) * chunk]
        g_ref[:, c * chunk:(c + 1) * chunk] = _sigmoid(gl).astype(BF16)


def _attn_kernel(plan_ref, q_ref, k_ref, v_ref, qc_ref, kc_ref, lq1_ref, lk1_ref, lq2_ref, lk2_ref,
                 sg_ref, o_ref, p_ref, m_ref, l_ref, acc_ref, *, tile, n_tiles):
    h = pl.program_id(1)
    bounded = plan_ref[0] == 1
    n_reach = plan_ref[1 + h]
    spare = n_tiles

    lane = lax.broadcasted_iota(jnp.int32, (tile, LANES), 1)
    qcols = qc_ref[0]
    kcols = kc_ref[0]
    lane1 = lax.broadcasted_iota(jnp.int32, (1, LANES), 1)
    dist_lane = (lane1 >= 2 * BIAS_TERMS) & (lane1 < 3 * BIAS_TERMS)

    row = lax.broadcasted_iota(jnp.int32, (tile, tile), 0)
    col = lax.broadcasted_iota(jnp.int32, (tile, tile), 1)
    future = row > col

    def scores(qi, j, masked):
        q = q_ref[0, pl.ds(pl.multiple_of(qi * tile, tile), tile), :]
        zero = jnp.zeros_like(q)
        dist = jnp.where(dist_lane, jnp.asarray(j - qi).astype(F32), 0.0).astype(BF16)
        ka = jnp.concatenate([k_ref[0, pl.ds(pl.multiple_of(j * tile, tile), tile), :], kcols + dist],
                             axis=1)
        out = []
        for c in range(2):
            qc = jnp.where(lane >= HEAD_DIM if c else lane < HEAD_DIM, q, zero)
            qa = jnp.concatenate([qc, qcols], axis=1)
            s = lax.dot_general(ka, qa, (((1,), (1,)), ((), ())), preferred_element_type=F32)
            out.append(jnp.where(future, NEG_BIG, s) if masked else s)
        return out

    def values_t_probs(j, p):
        start = pl.multiple_of(j * tile, tile)
        return lax.dot_general(v_ref[0, pl.ds(start, tile), :], p, (((0,), (0,)), ((), ())),
                               preferred_element_type=F32)

    l_ref[...] = jnp.zeros(l_ref.shape, F32)
    acc_ref[...] = jnp.zeros(acc_ref.shape, F32)

    @pl.when(bounded)
    def _():
        def probs(item, slot, masked):
            qi, j, state = item
            for c, s in enumerate(scores(qi, j, masked)):
                p = jnp.exp2(s)
                l_ref[state, c] += jnp.sum(p, axis=0, keepdims=True)
                p_ref[slot, c] = p.astype(BF16)

        def accumulate(item, slot):
            _, j, state = item
            for c in range(2):
                acc_ref[state, c] += values_t_probs(j, p_ref[slot, c])

        def diagonal(i):
            return i, i, i

        def below(cursor):
            d, qi = cursor
            live = d <= reach
            return jnp.where(live, qi, 0), jnp.where(live, qi - d, 0), jnp.where(live, qi, spare)

        def advance(cursor):
            d, qi = cursor
            wrap = qi + 1 >= n_tiles
            d = jnp.where(wrap, d + 1, d)
            return d, jnp.where(wrap, d, qi + 1)

        reach = jnp.minimum(n_reach, n_tiles - 1)
        n_below = reach * n_tiles - lax.shift_right_logical(reach * (reach + 1), 1)

        probs(diagonal(0), 0, True)

        def diagonal_pair(t, carry):
            i = 2 * t
            probs(diagonal(i + 1), 1, True)
            accumulate(diagonal(i), 0)
            probs(diagonal(i + 2), 0, True)
            accumulate(diagonal(i + 1), 1)
            return carry

        lax.fori_loop(0, n_tiles // 2 - 1, diagonal_pair, 0)
        first = (jnp.int32(1), jnp.int32(1))
        probs(diagonal(n_tiles - 1), 1, True)
        accumulate(diagonal(n_tiles - 2), 0)
        probs(below(first), 0, False)
        accumulate(diagonal(n_tiles - 1), 1)

        def below_group(t, cursor):
            for u in range(BELOW_UNROLL):
                nxt = advance(cursor)
                probs(below(nxt), (u + 1) % 2, False)
                accumulate(below(cursor), u % 2)
                cursor = nxt
            return cursor

        lax.fori_loop(0, (n_below + BELOW_UNROLL - 1) // BELOW_UNROLL, below_group, first)

    @pl.when(jnp.logical_not(bounded))
    def _():
        def block(qi, j, masked):
            for c, s in enumerate(scores(qi, j, masked)):
                m_old = m_ref[c]
                m_new = jnp.maximum(m_old, jnp.max(s, axis=0, keepdims=True))
                p = jnp.exp2(s - m_new)
                alpha = jnp.exp2(m_old - m_new)
                l_ref[qi, c] = alpha * l_ref[qi, c] + jnp.sum(p, axis=0, keepdims=True)
                acc_ref[qi, c] = alpha * acc_ref[qi, c] + values_t_probs(j, p.astype(BF16))
                m_ref[c] = m_new

        def query_tile(qi, carry):
            m_ref[...] = jnp.full(m_ref.shape, NEG_BIG, F32)
            block(qi, qi, True)

            def body(t, c):
                block(qi, qi - 1 - t, False)
                return c

            lax.fori_loop(0, qi, body, 0)
            return carry

        lax.fori_loop(0, n_tiles, query_tile, 0)

    lam = (jnp.exp(jnp.sum(lq1_ref[...] * lk1_ref[...], axis=-1, keepdims=True))
           - jnp.exp(jnp.sum(lq2_ref[...] * lk2_ref[...], axis=-1, keepdims=True))
           + LAMBDA_INIT)

    def finish(qi, carry):
        o = acc_ref[qi, 0] * (1.0 / l_ref[qi, 0]) - acc_ref[qi, 1] * (lam / l_ref[qi, 1])
        msq = jnp.mean(o * o, axis=0, keepdims=True)
        y = o * lax.rsqrt(msq + EPS) * (sg_ref[...] * (1.0 - LAMBDA_INIT))
        o_ref[0, pl.ds(pl.multiple_of(qi * tile, tile), tile), :] = y.astype(BF16).T
        return carry

    lax.fori_loop(0, n_tiles, finish, 0)


def _merge_ffn_kernel(x_ref, u_ref, halo_ref, a_ref, g_ref, ps_ref, g2_ref,
                      pw_hbm, wa_hbm, wb_hbm, wo_hbm, w1_hbm, w2_hbm, o_ref,
                      ext_ref, mid_ref, pw_ref, wa_ref, wb_ref, wo_ref, w1_ref, w2_ref, stage_ref, sem,
                      *, seq, tm, chunk, ff_chunk):
    i = pl.program_id(0)
    n_ff = w1_ref.shape[1] // ff_chunk

    pieces = [(pw_hbm, pw_ref), (wa_hbm, wa_ref), (wb_hbm, wb_ref), (wo_hbm, wo_ref)]
    for c in range(n_ff):
        cols = slice(c * ff_chunk, (c + 1) * ff_chunk)
        pieces.append((w1_hbm.at[:, cols], w1_ref.at[:, cols]))
        pieces.append((w2_hbm.at[cols, :], w2_ref.at[cols, :]))

    def staged(c):
        rows, cols = pieces[c][0].shape
        return stage_ref.at[c % 2, 0:rows, 0:cols]

    def fetch(c):
        return pltpu.make_async_copy(pieces[c][0], staged(c), sem.at[c % 2])

    @pl.when(i == 0)
    def _():
        fetch(0).start()
        for c in range(len(pieces)):
            if c + 1 < len(pieces):
                fetch(c + 1).start()
            fetch(c).wait()
            pieces[c][1][...] = staged(c)[...].astype(BF16)

    t0 = (i * tm) % seq
    halo = halo_ref[...]
    ext_ref[0:HALO, :] = jnp.where(t0 == 0, jnp.zeros_like(halo), halo)
    ext_ref[HALO:, :] = u_ref[...]
    d = o_ref.shape[1]

    for r0 in range(0, tm, chunk):
        rows = slice(r0, r0 + chunk)
        t = t0 + r0 + lax.broadcasted_iota(jnp.int32, (chunk, 1), 0)
        mixed = []
        for g, w in enumerate(POOL_WINDOWS):
            y = ext_ref[r0:r0 + chunk + HALO, g * POOL_GROUP:(g + 1) * POOL_GROUP]
            tot = y
            span = 1
            while span < w:
                tot = tot + pltpu.roll(tot, span, axis=0)
                span *= 2
            cnt = jnp.minimum(t + 1, w).astype(F32)
            pooled = tot[HALO:] / cnt - y[HALO:]
            mixed.append(_dot(pooled.astype(BF16), pw_ref[g * POOL_GROUP:(g + 1) * POOL_GROUP, :]))
        mixed = jnp.concatenate(mixed, axis=1) * ps_ref[...]
        ya = _dot(mixed.astype(BF16), wa_ref[...])
        yb = _dot(a_ref[rows, :], wb_ref[...])
        merged = g_ref[rows, 0:d].astype(F32) * ya + g_ref[rows, d:2 * d].astype(F32) * yb
        mid_ref[rows, :] = x_ref[rows, :] + _dot(merged.astype(BF16), wo_ref[...])

    x = mid_ref[...]
    ms = jnp.mean(x * x, axis=-1, keepdims=True)
    h = (x * lax.rsqrt(ms + EPS) * g2_ref[...]).astype(BF16)
    acc = x
    for c in range(n_ff):
        cols = slice(c * ff_chunk, (c + 1) * ff_chunk)
        a = jnp.maximum(_dot(h, w1_ref[:, cols]), 0.0)
        acc = acc + _dot((a * a).astype(BF16), w2_ref[cols, :])
    o_ref[...] = acc


def _alibi_columns(tile):
    slopes2 = np.array([2.0 ** (-8.0 * (h + 1) / N_HEADS) for h in range(N_HEADS)]) * LOG2E
    pos = np.arange(tile, dtype=np.float64)

    def terms(x):
        out = []
        for _ in range(BIAS_TERMS):
            part = x.astype(BF16)
            out.append(part)
            x = x - part.astype(np.float64)
        return out

    kcols = np.zeros((N_HEADS, tile, LANES), BF16)
    qcols = np.zeros((N_HEADS, tile, LANES), BF16)
    k_terms = terms(slopes2[:, None] * pos[None, :])
    q_terms = terms(-slopes2[:, None] * pos[None, :])
    t_terms = terms(np.broadcast_to(slopes2[:, None] * tile, (N_HEADS, tile)))
    for t in range(BIAS_TERMS):
        kcols[:, :, t] = k_terms[t]
        kcols[:, :, BIAS_TERMS + t] = 1.0
        qcols[:, :, t] = 1.0
        qcols[:, :, BIAS_TERMS + t] = q_terms[t]
        qcols[:, :, 2 * BIAS_TERMS + t] = t_terms[t]
    return jnp.asarray(qcols), jnp.asarray(kcols), slopes2.astype(np.float32)


def _const_spec(shape):
    return pl.BlockSpec(shape, lambda *_: (0,) * len(shape))


def _resident_spec(shape):
    return pl.BlockSpec(shape, lambda *_: (0,) * len(shape), pipeline_mode=pl.Buffered(1))


def _params(n_axes):
    return pltpu.CompilerParams(dimension_semantics=("arbitrary",) * n_axes,
                                vmem_limit_bytes=VMEM_LIMIT)


def kernel(x, norm1_g, w_in, gate_b, pool_w, pool_scale, q_norm_g, k_norm_g, lambda_q1, lambda_k1,
           lambda_q2, lambda_k2, subln_g, w_branch_a, w_branch_b, w_out, norm2_g, w_ff1, w_ff2):
    B, S, D = x.shape
    depth = norm1_g.shape[0]
    assert depth == 1
    M = B * S
    pw = pool_scale.shape[1]
    aw = 2 * N_HEADS * HEAD_DIM
    vw = N_HEADS * V_DIM
    gw = 2 * D
    d_ff = w_ff1.shape[2]
    tm = ROW_TILE
    ta = ATTN_TILE
    assert S % tm == 0 and S % ta == 0 and tm % HALO == 0
    assert D == FFN_CHUNK and d_ff % FFN_CHUNK == 0 and max(pw, vw) <= D
    assert pw == len(POOL_WINDOWS) * POOL_GROUP and w_in.shape[2] == pw + 2 * aw + vw + gw

    x2 = x.reshape(M, D)
    row = lambda width: pl.BlockSpec((tm, width), lambda i: (i, 0))

    bd = jnp.asarray(np.kron(np.eye(MXU_DIM // HEAD_DIM), np.ones((HEAD_DIM, HEAD_DIM))), BF16)
    reps = aw // HEAD_DIM
    u, q, k, v, gates = pl.pallas_call(
        functools.partial(_inproj_kernel, widths=(pw, aw, vw, gw), scale=LOG2E / math.sqrt(HEAD_DIM)),
        grid=(M // tm,),
        in_specs=[row(D), _const_spec((1, D)), _resident_spec(w_in.shape[1:]), _const_spec((1, gw)),
                  _const_spec((1, aw)), _const_spec((1, aw)), _const_spec((MXU_DIM, MXU_DIM))],
        out_specs=[row(pw), row(aw), row(aw), row(vw), row(gw)],
        out_shape=[jax.ShapeDtypeStruct((M, pw), F32), jax.ShapeDtypeStruct((M, aw), BF16),
                   jax.ShapeDtypeStruct((M, aw), BF16), jax.ShapeDtypeStruct((M, vw), BF16),
                   jax.ShapeDtypeStruct((M, gw), BF16)],
        scratch_shapes=[pltpu.VMEM(w_in.shape[1:], BF16)],
        compiler_params=_params(1),
        name="inproj",
    )(x2, norm1_g, w_in[0], gate_b, jnp.tile(q_norm_g, (1, reps)),
      jnp.tile(k_norm_g, (1, reps)), bd)

    qcols, kcols, slopes2 = _alibi_columns(ta)

    bound = (1.01 * LOG2E * math.sqrt(HEAD_DIM)) * jnp.max(jnp.abs(q_norm_g)) * jnp.max(jnp.abs(k_norm_g))
    bounded = bound <= SCORE_BOUND_MAX
    reach = jnp.floor(((bound - EXP2_ZERO_BELOW) / jnp.asarray(slopes2, F32) - 1.0) / ta) + 1.0
    reach = jnp.clip(jnp.where(bounded, reach, S // ta), 0, S // ta)
    plan = jnp.concatenate([bounded.astype(jnp.int32)[None], reach.astype(jnp.int32)])

    n_tiles = S // ta
    assert n_tiles % 2 == 0
    head_seq = pl.BlockSpec((1, S, LANES), lambda b, h: (b, 0, h))
    head_cols = pl.BlockSpec((1, ta, LANES), lambda b, h: (h, 0, 0))
    attn = pl.pallas_call(
        functools.partial(_attn_kernel, tile=ta, n_tiles=n_tiles),
        grid=(B, N_HEADS),
        in_specs=[pl.BlockSpec(memory_space=pltpu.SMEM), head_seq, head_seq, head_seq,
                  head_cols, head_cols,
                  _const_spec((1, HEAD_DIM)), _const_spec((1, HEAD_DIM)),
                  _const_spec((1, HEAD_DIM)), _const_spec((1, HEAD_DIM)), _const_spec((V_DIM, 1))],
        out_specs=head_seq,
        out_shape=jax.ShapeDtypeStruct((B, S, vw), BF16),
        scratch_shapes=[pltpu.VMEM((2, 2, ta, ta), BF16),
                        pltpu.VMEM((2, 1, ta), F32),
                        pltpu.VMEM((n_tiles + 1, 2, 1, ta), F32),
                        pltpu.VMEM((n_tiles + 1, 2, V_DIM, ta), F32)],
        compiler_params=_params(2),
        name="diffattn",
    )(plan, q.reshape(B, S, aw), k.reshape(B, S, aw), v.reshape(B, S, vw), qcols, kcols,
      lambda_q1, lambda_k1, lambda_q2, lambda_k2, subln_g.reshape(V_DIM, 1))

    in_hbm = pl.BlockSpec(memory_space=pl.ANY)
    out = pl.pallas_call(
        functools.partial(_merge_ffn_kernel, seq=S, tm=tm, chunk=MERGE_CHUNK, ff_chunk=FFN_CHUNK),
        grid=(M // tm,),
        in_specs=[row(D), row(pw),
                  pl.BlockSpec((HALO, pw), lambda i: (jnp.maximum(i * (tm // HALO) - 1, 0), 0)),
                  row(vw), row(gw), _const_spec((1, pw)), _const_spec((1, D)),
                  in_hbm, in_hbm, in_hbm, in_hbm, in_hbm, in_hbm],
        out_specs=row(D),
        out_shape=jax.ShapeDtypeStruct((M, D), F32),
        scratch_shapes=[pltpu.VMEM((HALO + tm, pw), F32),
                        pltpu.VMEM((tm, D), F32),
                        pltpu.VMEM((pw, POOL_GROUP), BF16), pltpu.VMEM((pw, D), BF16),
                        pltpu.VMEM((vw, D), BF16), pltpu.VMEM((D, D), BF16),
                        pltpu.VMEM((D, d_ff), BF16), pltpu.VMEM((d_ff, D), BF16),
                        pltpu.VMEM((2, D, FFN_CHUNK), F32),
                        pltpu.SemaphoreType.DMA((2,))],
        compiler_params=_params(1),
        name="merge_ffn",
    )(x2, u, u, attn.reshape(M, vw), gates, pool_scale, norm2_g,
      pool_w[0].reshape(pw, POOL_GROUP), w_branch_a[0], w_branch_b[0], w_out[0], w_ff1[0], w_ff2[0])
    return out.reshape(B, S, D)
```

```python
import functools
import math

import numpy as np
import jax
import jax.numpy as jnp
from jax import lax
from jax.experimental import pallas as pl
from jax.experimental.pallas import tpu as pltpu

F32 = jnp.float32
BF16 = jnp.bfloat16
EPS = 1e-6

POOL_WINDOWS = (2, 4, 8, 16)
POOL_GROUP = 128
HALO = max(POOL_WINDOWS)
N_HEADS = 4
HEAD_DIM = 64
V_DIM = 2 * HEAD_DIM
LANES = 128
MXU_DIM = 256
LAMBDA_INIT = 0.8 - 0.6 * math.exp(-0.3 * 0)
NEG_BIG = -1e30
LOG2E = math.log2(math.e)
BIAS_TERMS = 3
SCORE_BOUND_MAX = 60.0
EXP2_ZERO_BELOW = -150.0

ROW_TILE = 512
FFN_CHUNK = 1024
MERGE_CHUNK = 256
ATTN_TILE = 512
DIAG_UNROLL = 3
BELOW_UNROLL = 4
VMEM_LIMIT = 56 * 1024 * 1024


def _dot(a, b):
    return jnp.dot(a, b, preferred_element_type=F32)


def _sigmoid(x):
    return 1.0 / (1.0 + jnp.exp(-x))


def _inproj_kernel(x_ref, g1_ref, w_ref, gb_ref, gq_ref, gk_ref, bd_ref,
                   u_ref, q_ref, k_ref, v_ref, g_ref, wbf_ref, *, widths, scale):
    pw, aw, vw, gw = widths

    @pl.when(pl.program_id(0) == 0)
    def _():
        wbf_ref[...] = w_ref[...].astype(BF16)

    x = x_ref[...]
    ms = jnp.mean(x * x, axis=-1, keepdims=True)
    h = (x * lax.rsqrt(ms + EPS) * g1_ref[...]).astype(BF16)

    def proj(lo, width):
        return _dot(h, wbf_ref[:, lo:lo + width])

    bd = bd_ref[...]

    def group_rms(y):
        y2 = y * y
        hi = y2.astype(BF16)
        lo = (y2 - hi.astype(F32)).astype(BF16)
        parts = []
        for c in range(y.shape[1] // MXU_DIM):
            sl = slice(c * MXU_DIM, (c + 1) * MXU_DIM)
            parts.append(_dot(hi[:, sl], bd) + _dot(lo[:, sl], bd))
        ss = jnp.concatenate(parts, axis=1)
        return y * lax.rsqrt(ss * (1.0 / HEAD_DIM) + EPS)

    q_ref[...] = (group_rms(proj(pw, aw)) * gq_ref[...] * scale).astype(BF16)
    k_ref[...] = (group_rms(proj(pw + aw, aw)) * gk_ref[...]).astype(BF16)
    g0 = pw + 2 * aw + vw
    chunk = 512
    for c in range(gw // chunk):
        gl = proj(g0 + c * chunk, chunk) + gb_ref[:, c * chunk:(c + 1) * chunk]
        g_ref[:, c * chunk:(c + 1) * chunk] = _sigmoid(gl).astype(BF16)
    v_ref[...] = proj(pw + 2 * aw, vw).astype(BF16)
    u_ref[...] = proj(0, pw)


def _attn_kernel(plan_ref, q_ref, k_ref, v_ref, qc_ref, kc_ref, lq1_ref, lk1_ref, lq2_ref, lk2_ref,
                 sg_ref, o_ref, p_ref, m_ref, l_ref, acc_ref, *, tile, n_tiles):
    h = pl.program_id(1)
    bounded = plan_ref[0] == 1
    n_reach = plan_ref[1 + h]
    spare = n_tiles

    lane = lax.broadcasted_iota(jnp.int32, (tile, LANES), 1)
    qcols = qc_ref[0]
    kcols = kc_ref[0]
    lane1 = lax.broadcasted_iota(jnp.int32, (1, LANES), 1)
    dist_lane = (lane1 >= 2 * BIAS_TERMS) & (lane1 < 3 * BIAS_TERMS)

    whole = (0, tile)

    def scores(qi, j, masked, keys=whole, queries=whole):
        k0, nk = keys
        q0, nq = queries
        q = q_ref[0, pl.ds(pl.multiple_of(qi * tile, tile) + q0, nq), :]
        zero = jnp.zeros_like(q)
        dist = jnp.where(dist_lane, jnp.asarray(j - qi).astype(F32), 0.0).astype(BF16)
        ka = jnp.concatenate([k_ref[0, pl.ds(pl.multiple_of(j * tile, tile) + k0, nk), :],
                              kc_ref[0, k0:k0 + nk, :] + dist], axis=1)
        if masked:
            future = (lax.broadcasted_iota(jnp.int32, (nk, nq), 0) + k0
                      > lax.broadcasted_iota(jnp.int32, (nk, nq), 1) + q0)
        lane = lax.broadcasted_iota(jnp.int32, (nq, LANES), 1)
        out = []
        for c in range(2):
            qc = jnp.where(lane >= HEAD_DIM if c else lane < HEAD_DIM, q, zero)
            qa = jnp.concatenate([qc, qc_ref[0, q0:q0 + nq, :]], axis=1)
            s = lax.dot_general(ka, qa, (((1,), (1,)), ((), ())), preferred_element_type=F32)
            out.append(jnp.where(future, NEG_BIG, s) if masked else s)
        return out

    def values_t_probs(j, p, keys=whole):
        start = pl.multiple_of(j * tile, tile) + keys[0]
        return lax.dot_general(v_ref[0, pl.ds(start, keys[1]), :], p, (((0,), (0,)), ((), ())),
                               preferred_element_type=F32)

    l_ref[spare] = jnp.zeros(l_ref.shape[1:], F32)
    acc_ref[spare] = jnp.zeros(acc_ref.shape[1:], F32)

    @pl.when(bounded)
    def _():
        def probs(item, slot, masked):
            qi, j, state = item
            for c, s in enumerate(scores(qi, j, masked)):
                p = jnp.exp2(s)
                l_ref[state, c] += jnp.sum(p, axis=0, keepdims=True)
                p_ref[slot, c] = p.astype(BF16)

        def accumulate(item, slot):
            _, j, state = item
            for c in range(2):
                acc_ref[state, c] += values_t_probs(j, p_ref[slot, c])

        half = tile // 2
        first_keys, last_keys, last_queries = (0, half), (half, half), (half, half)

        def first_half_probs(i):
            for c, s in enumerate(scores(i, i, True, keys=first_keys)):
                p = jnp.exp2(s)
                l_ref[i, c] = jnp.sum(p, axis=0, keepdims=True)
                p_ref[0, c, 0:half, :] = p.astype(BF16)

        def first_half_accumulate(i):
            for c in range(2):
                acc_ref[i, c] = values_t_probs(i, p_ref[0, c, 0:half, :], keys=first_keys)

        def last_half_probs(i):
            for c, s in enumerate(scores(i, i, True, keys=last_keys, queries=last_queries)):
                p = jnp.exp2(s)
                part = jnp.sum(p, axis=0, keepdims=True)
                l_ref[i, c] += jnp.concatenate([jnp.zeros_like(part), part], axis=1)
                p_ref[1, c, 0:half, 0:half] = p.astype(BF16)

        def last_half_accumulate(i):
            for c in range(2):
                acc_ref[i, c, :, half:] += values_t_probs(i, p_ref[1, c, 0:half, 0:half], keys=last_keys)

        def below(cursor):
            d, qi = cursor
            live = d <= reach
            return jnp.where(live, qi, 0), jnp.where(live, qi - d, 0), jnp.where(live, qi, spare)

        def advance(cursor):
            d, qi = cursor
            wrap = qi + 1 >= n_tiles
            d = jnp.where(wrap, d + 1, d)
            return d, jnp.where(wrap, d, qi + 1)

        reach = jnp.minimum(n_reach, n_tiles - 1)
        n_below = reach * n_tiles - lax.shift_right_logical(reach * (reach + 1), 1)

        first_half_probs(0)

        def diagonal_tile(i):
            last_half_probs(i)
            first_half_accumulate(i)
            first_half_probs(i + 1)
            last_half_accumulate(i)

        def diagonal_tiles(t, carry):
            for u in range(DIAG_UNROLL):
                diagonal_tile(DIAG_UNROLL * t + u)
            return carry

        n_looped = (n_tiles - 1) // DIAG_UNROLL * DIAG_UNROLL
        lax.fori_loop(0, n_looped // DIAG_UNROLL, diagonal_tiles, 0)
        for i in range(n_looped, n_tiles - 1):
            diagonal_tile(i)
        first = (jnp.int32(1), jnp.int32(1))
        last_half_probs(n_tiles - 1)
        first_half_accumulate(n_tiles - 1)
        probs(below(first), 0, False)
        last_half_accumulate(n_tiles - 1)

        def below_group(t, cursor):
            for u in range(BELOW_UNROLL):
                nxt = advance(cursor)
                probs(below(nxt), (u + 1) % 2, False)
                accumulate(below(cursor), u % 2)
                cursor = nxt
            return cursor

        lax.fori_loop(0, (n_below + BELOW_UNROLL - 1) // BELOW_UNROLL, below_group, first)

    @pl.when(jnp.logical_not(bounded))
    def _():
        def block(qi, j, masked):
            for c, s in enumerate(scores(qi, j, masked)):
                m_old = m_ref[c]
                m_new = jnp.maximum(m_old, jnp.max(s, axis=0, keepdims=True))
                p = jnp.exp2(s - m_new)
                alpha = jnp.exp2(m_old - m_new)
                l_ref[qi, c] = alpha * l_ref[qi, c] + jnp.sum(p, axis=0, keepdims=True)
                acc_ref[qi, c] = alpha * acc_ref[qi, c] + values_t_probs(j, p.astype(BF16))
                m_ref[c] = m_new

        def query_tile(qi, carry):
            m_ref[...] = jnp.full(m_ref.shape, NEG_BIG, F32)
            l_ref[qi] = jnp.zeros(l_ref.shape[1:], F32)
            acc_ref[qi] = jnp.zeros(acc_ref.shape[1:], F32)
            block(qi, qi, True)

            def body(t, c):
                block(qi, qi - 1 - t, False)
                return c

            lax.fori_loop(0, qi, body, 0)
            return carry

        lax.fori_loop(0, n_tiles, query_tile, 0)

    lam = (jnp.exp(jnp.sum(lq1_ref[...] * lk1_ref[...], axis=-1, keepdims=True))
           - jnp.exp(jnp.sum(lq2_ref[...] * lk2_ref[...], axis=-1, keepdims=True))
           + LAMBDA_INIT)

    def finish(qi, carry):
        o = acc_ref[qi, 0] * (1.0 / l_ref[qi, 0]) - acc_ref[qi, 1] * (lam / l_ref[qi, 1])
        msq = jnp.mean(o * o, axis=0, keepdims=True)
        y = o * lax.rsqrt(msq + EPS) * (sg_ref[...] * (1.0 - LAMBDA_INIT))
        o_ref[0, pl.ds(pl.multiple_of(qi * tile, tile), tile), :] = y.astype(BF16).T
        return carry

    lax.fori_loop(0, n_tiles, finish, 0)


def _merge_ffn_kernel(x_ref, u_ref, halo_ref, a_ref, g_ref, ps_ref, g2_ref,
                      pw_hbm, wa_hbm, wb_hbm, wo_hbm, w1_hbm, w2_hbm, o_ref,
                      ext_ref, mid_ref, pw_ref, wa_ref, wb_ref, wo_ref, w1_ref, w2_ref, stage_ref, sem,
                      *, seq, tm, chunk, ff_chunk):
    i = pl.program_id(0)
    n_ff = w1_ref.shape[1] // ff_chunk

    pieces = [(pw_hbm, pw_ref), (wa_hbm, wa_ref), (wb_hbm, wb_ref), (wo_hbm, wo_ref)]
    for c in range(n_ff):
        cols = slice(c * ff_chunk, (c + 1) * ff_chunk)
        pieces.append((w1_hbm.at[:, cols], w1_ref.at[:, cols]))
        pieces.append((w2_hbm.at[cols, :], w2_ref.at[cols, :]))

    def staged(c):
        rows, cols = pieces[c][0].shape
        return stage_ref.at[c % 2, 0:rows, 0:cols]

    def fetch(c):
        return pltpu.make_async_copy(pieces[c][0], staged(c), sem.at[c % 2])

    @pl.when(i == 0)
    def _():
        fetch(0).start()
        for c in range(len(pieces)):
            if c + 1 < len(pieces):
                fetch(c + 1).start()
            fetch(c).wait()
            pieces[c][1][...] = staged(c)[...].astype(BF16)

    t0 = (i * tm) % seq
    halo = halo_ref[...]
    ext_ref[0:HALO, :] = jnp.where(t0 == 0, jnp.zeros_like(halo), halo)
    ext_ref[HALO:, :] = u_ref[...]
    d = o_ref.shape[1]

    for r0 in range(0, tm, chunk):
        rows = slice(r0, r0 + chunk)
        t = t0 + r0 + lax.broadcasted_iota(jnp.int32, (chunk, 1), 0)
        mixed = []
        for g, w in enumerate(POOL_WINDOWS):
            y = ext_ref[r0:r0 + chunk + HALO, g * POOL_GROUP:(g + 1) * POOL_GROUP]
            tot = y
            span = 1
            while span < w:
                tot = tot + pltpu.roll(tot, span, axis=0)
                span *= 2
            cnt = jnp.minimum(t + 1, w).astype(F32)
            pooled = tot[HALO:] / cnt - y[HALO:]
            mixed.append(_dot(pooled.astype(BF16), pw_ref[g * POOL_GROUP:(g + 1) * POOL_GROUP, :]))
        mixed = jnp.concatenate(mixed, axis=1) * ps_ref[...]
        ya = _dot(mixed.astype(BF16), wa_ref[...])
        yb = _dot(a_ref[rows, :], wb_ref[...])
        merged = g_ref[rows, 0:d].astype(F32) * ya + g_ref[rows, d:2 * d].astype(F32) * yb
        mid_ref[rows, :] = x_ref[rows, :] + _dot(merged.astype(BF16), wo_ref[...])

    x = mid_ref[...]
    ms = jnp.mean(x * x, axis=-1, keepdims=True)
    h = (x * lax.rsqrt(ms + EPS) * g2_ref[...]).astype(BF16)
    acc = x
    for c in range(n_ff):
        cols = slice(c * ff_chunk, (c + 1) * ff_chunk)
        a = jnp.maximum(_dot(h, w1_ref[:, cols]), 0.0)
        acc = acc + _dot((a * a).astype(BF16), w2_ref[cols, :])
    o_ref[...] = acc


def _alibi_columns(tile):
    slopes2 = np.array([2.0 ** (-8.0 * (h + 1) / N_HEADS) for h in range(N_HEADS)]) * LOG2E
    pos = np.arange(tile, dtype=np.float64)

    def terms(x):
        out = []
        for _ in range(BIAS_TERMS):
            part = x.astype(BF16)
            out.append(part)
            x = x - part.astype(np.float64)
        return out

    kcols = np.zeros((N_HEADS, tile, LANES), BF16)
    qcols = np.zeros((N_HEADS, tile, LANES), BF16)
    k_terms = terms(slopes2[:, None] * pos[None, :])
    q_terms = terms(-slopes2[:, None] * pos[None, :])
    t_terms = terms(np.broadcast_to(slopes2[:, None] * tile, (N_HEADS, tile)))
    for t in range(BIAS_TERMS):
        kcols[:, :, t] = k_terms[t]
        kcols[:, :, BIAS_TERMS + t] = 1.0
        qcols[:, :, t] = 1.0
        qcols[:, :, BIAS_TERMS + t] = q_terms[t]
        qcols[:, :, 2 * BIAS_TERMS + t] = t_terms[t]
    return jnp.asarray(qcols), jnp.asarray(kcols), slopes2.astype(np.float32)


def _const_spec(shape):
    return pl.BlockSpec(shape, lambda *_: (0,) * len(shape))


def _resident_spec(shape):
    return pl.BlockSpec(shape, lambda *_: (0,) * len(shape), pipeline_mode=pl.Buffered(1))


def _params(n_axes):
    return pltpu.CompilerParams(dimension_semantics=("arbitrary",) * n_axes,
                                vmem_limit_bytes=VMEM_LIMIT)


def kernel(x, norm1_g, w_in, gate_b, pool_w, pool_scale, q_norm_g, k_norm_g, lambda_q1, lambda_k1,
           lambda_q2, lambda_k2, subln_g, w_branch_a, w_branch_b, w_out, norm2_g, w_ff1, w_ff2):
    B, S, D = x.shape
    depth = norm1_g.shape[0]
    assert depth == 1
    M = B * S
    pw = pool_scale.shape[1]
    aw = 2 * N_HEADS * HEAD_DIM
    vw = N_HEADS * V_DIM
    gw = 2 * D
    d_ff = w_ff1.shape[2]
    tm = ROW_TILE
    ta = ATTN_TILE
    assert S % tm == 0 and S % ta == 0 and tm % HALO == 0
    assert D == FFN_CHUNK and d_ff % FFN_CHUNK == 0 and max(pw, vw) <= D
    assert pw == len(POOL_WINDOWS) * POOL_GROUP and w_in.shape[2] == pw + 2 * aw + vw + gw

    x2 = x.reshape(M, D)
    row = lambda width: pl.BlockSpec((tm, width), lambda i: (i, 0))

    bd = jnp.asarray(np.kron(np.eye(MXU_DIM // HEAD_DIM), np.ones((HEAD_DIM, HEAD_DIM))), BF16)
    reps = aw // HEAD_DIM
    u, q, k, v, gates = pl.pallas_call(
        functools.partial(_inproj_kernel, widths=(pw, aw, vw, gw), scale=LOG2E / math.sqrt(HEAD_DIM)),
        grid=(M // tm,),
        in_specs=[row(D), _const_spec((1, D)), _resident_spec(w_in.shape[1:]), _const_spec((1, gw)),
                  _const_spec((1, aw)), _const_spec((1, aw)), _const_spec((MXU_DIM, MXU_DIM))],
        out_specs=[row(pw), row(aw), row(aw), row(vw), row(gw)],
        out_shape=[jax.ShapeDtypeStruct((M, pw), F32), jax.ShapeDtypeStruct((M, aw), BF16),
                   jax.ShapeDtypeStruct((M, aw), BF16), jax.ShapeDtypeStruct((M, vw), BF16),
                   jax.ShapeDtypeStruct((M, gw), BF16)],
        scratch_shapes=[pltpu.VMEM(w_in.shape[1:], BF16)],
        compiler_params=_params(1),
        name="inproj",
    )(x2, norm1_g, w_in[0], gate_b, jnp.tile(q_norm_g, (1, reps)),
      jnp.tile(k_norm_g, (1, reps)), bd)

    qcols, kcols, slopes2 = _alibi_columns(ta)

    bound = (1.01 * LOG2E * math.sqrt(HEAD_DIM)) * jnp.max(jnp.abs(q_norm_g)) * jnp.max(jnp.abs(k_norm_g))
    bounded = bound <= SCORE_BOUND_MAX
    reach = jnp.floor(((bound - EXP2_ZERO_BELOW) / jnp.asarray(slopes2, F32) - 1.0) / ta) + 1.0
    reach = jnp.clip(jnp.where(bounded, reach, S // ta), 0, S // ta)
    plan = jnp.concatenate([bounded.astype(jnp.int32)[None], reach.astype(jnp.int32)])

    n_tiles = S // ta
    assert n_tiles % 2 == 0
    head_seq = pl.BlockSpec((1, S, LANES), lambda b, h: (b, 0, h))
    head_cols = pl.BlockSpec((1, ta, LANES), lambda b, h: (h, 0, 0))
    attn = pl.pallas_call(
        functools.partial(_attn_kernel, tile=ta, n_tiles=n_tiles),
        grid=(B, N_HEADS),
        in_specs=[pl.BlockSpec(memory_space=pltpu.SMEM), head_seq, head_seq, head_seq,
                  head_cols, head_cols,
                  _const_spec((1, HEAD_DIM)), _const_spec((1, HEAD_DIM)),
                  _const_spec((1, HEAD_DIM)), _const_spec((1, HEAD_DIM)), _const_spec((V_DIM, 1))],
        out_specs=head_seq,
        out_shape=jax.ShapeDtypeStruct((B, S, vw), BF16),
        scratch_shapes=[pltpu.VMEM((2, 2, ta, ta), BF16),
                        pltpu.VMEM((2, 1, ta), F32),
                        pltpu.VMEM((n_tiles + 1, 2, 1, ta), F32),
                        pltpu.VMEM((n_tiles + 1, 2, V_DIM, ta), F32)],
        compiler_params=_params(2),
        name="diffattn",
    )(plan, q.reshape(B, S, aw), k.reshape(B, S, aw), v.reshape(B, S, vw), qcols, kcols,
      lambda_q1, lambda_k1, lambda_q2, lambda_k2, subln_g.reshape(V_DIM, 1))

    in_hbm = pl.BlockSpec(memory_space=pl.ANY)
    out = pl.pallas_call(
        functools.partial(_merge_ffn_kernel, seq=S, tm=tm, chunk=MERGE_CHUNK, ff_chunk=FFN_CHUNK),
        grid=(M // tm,),
        in_specs=[row(D), row(pw),
                  pl.BlockSpec((HALO, pw), lambda i: (jnp.maximum(i * (tm // HALO) - 1, 0), 0)),
                  row(vw), row(gw), _const_spec((1, pw)), _const_spec((1, D)),
                  in_hbm, in_hbm, in_hbm, in_hbm, in_hbm, in_hbm],
        out_specs=row(D),
        out_shape=jax.ShapeDtypeStruct((M, D), F32),
        scratch_shapes=[pltpu.VMEM((HALO + tm, pw), F32),
                        pltpu.VMEM((tm, D), F32),
                        pltpu.VMEM((pw, POOL_GROUP), BF16), pltpu.VMEM((pw, D), BF16),
                        pltpu.VMEM((vw, D), BF16), pltpu.VMEM((D, D), BF16),
                        pltpu.VMEM((D, d_ff), BF16), pltpu.VMEM((d_ff, D), BF16),
                        pltpu.VMEM((2, D, FFN_CHUNK), F32),
                        pltpu.SemaphoreType.DMA((2,))],
        compiler_params=_params(1),
        name="merge_ffn",
    )(x2, u, u, attn.reshape(M, vw), gates, pool_scale, norm2_g,
      pool_w[0].reshape(pw, POOL_GROUP), w_branch_a[0], w_branch_b[0], w_out[0], w_ff1[0], w_ff2[0])
    return out.reshape(B, S, D)
```

```python
import functools
import math

import numpy as np
import jax
import jax.numpy as jnp
from jax import lax
from jax.experimental import pallas as pl
from jax.experimental.pallas import tpu as pltpu

F32 = jnp.float32
BF16 = jnp.bfloat16
EPS = 1e-6

POOL_WINDOWS = (2, 4, 8, 16)
POOL_GROUP = 128
HALO = max(POOL_WINDOWS)
N_HEADS = 4
HEAD_DIM = 64
V_DIM = 2 * HEAD_DIM
LANES = 128
MXU_DIM = 256
LAMBDA_INIT = 0.8 - 0.6 * math.exp(-0.3 * 0)
NEG_BIG = -1e30
LOG2E = math.log2(math.e)
BIAS_TERMS = 3
SCORE_BOUND_MAX = 60.0
EXP2_ZERO_BELOW = -150.0

ROW_TILE = 512
GATE_CHUNK = 512
FFN_CHUNK = 1024
MERGE_CHUNK = 256
ATTN_TILE = 512
DIAG_UNROLL = 3
BELOW_UNROLL = 4
VMEM_LIMIT = 56 * 1024 * 1024


def _dot(a, b):
    return jnp.dot(a, b, preferred_element_type=F32)


def _sigmoid(x):
    return 1.0 / (1.0 + jnp.exp(-x))


def _inproj_kernel(x_ref, g1_ref, w_ref, gb_ref, gq_ref, gk_ref,
                   u_ref, qt_ref, k_ref, v_ref, g_ref, wbf_ref, *, widths, scale):
    pw, aw, vw, gw = widths

    @pl.when(pl.program_id(0) == 0)
    def _():
        wbf_ref[...] = w_ref[...].astype(BF16)

    x = x_ref[...]
    ms = jnp.mean(x * x, axis=-1, keepdims=True)
    h = (x * lax.rsqrt(ms + EPS) * g1_ref[...]).astype(BF16)

    def proj(lo, width):
        return _dot(h, wbf_ref[:, lo:lo + width])

    def group_rms_t(y, gain_col):
        yt = y.T
        parts = []
        for g in range(y.shape[1] // HEAD_DIM):
            blk = yt[g * HEAD_DIM:(g + 1) * HEAD_DIM, :]
            parts.append(blk * lax.rsqrt(jnp.mean(blk * blk, axis=0, keepdims=True) + EPS))
        return jnp.concatenate(parts, axis=0) * gain_col

    qt = (group_rms_t(proj(pw, aw), gq_ref[...]) * scale).astype(BF16)
    for hd in range(N_HEADS):
        qt_ref[0, hd, 0] = qt[hd * 2 * HEAD_DIM:(hd + 1) * 2 * HEAD_DIM, :]
    k_ref[...] = group_rms_t(proj(pw + aw, aw), gk_ref[...]).astype(BF16).T
    g0 = pw + 2 * aw + vw
    chunk = GATE_CHUNK
    for c in range(gw // chunk):
        gl = proj(g0 + c * chunk, chunk) + gb_ref[:, c * chunk:(c + 1) * chunk]
        g_ref[:, c * chunk:(c + 1) * chunk] = _sigmoid(gl).astype(BF16)
    v_ref[...] = proj(pw + 2 * aw, vw).astype(BF16)
    u_ref[...] = proj(0, pw)


def _attn_kernel(plan_ref, qt_ref, k_ref, v_ref, qc_ref, kc_ref, lq1_ref, lk1_ref, lq2_ref, lk2_ref,
                 sg_ref, o_ref, p_ref, m_ref, l_ref, acc_ref, *, tile, n_tiles):
    h = pl.program_id(1)
    bounded = plan_ref[0] == 1
    n_reach = plan_ref[1 + h]
    spare = n_tiles

    lane1 = lax.broadcasted_iota(jnp.int32, (1, LANES), 1)
    dist_lane = (lane1 >= 2 * BIAS_TERMS) & (lane1 < 3 * BIAS_TERMS)

    whole = (0, tile)

    def scores(qi, j, masked, keys=whole, queries=whole):
        k0, nk = keys
        q0, nq = queries
        qt = qt_ref[0, 0, qi, :, q0:q0 + nq]
        zero = jnp.zeros_like(qt)
        dist = jnp.where(dist_lane, jnp.asarray(j - qi).astype(F32), 0.0).astype(BF16)
        ka = jnp.concatenate([k_ref[0, pl.ds(pl.multiple_of(j * tile, tile) + k0, nk), :],
                              kc_ref[0, k0:k0 + nk, :] + dist], axis=1)
        if masked:
            future = (lax.broadcasted_iota(jnp.int32, (nk, nq), 0) + k0
                      > lax.broadcasted_iota(jnp.int32, (nk, nq), 1) + q0)
        channel = lax.broadcasted_iota(jnp.int32, (2 * HEAD_DIM, nq), 0)
        out = []
        for c in range(2):
            qc = jnp.where(channel >= HEAD_DIM if c else channel < HEAD_DIM, qt, zero)
            qa = jnp.concatenate([qc, qc_ref[0, :, q0:q0 + nq]], axis=0)
            s = _dot(ka, qa)
            out.append(jnp.where(future, NEG_BIG, s) if masked else s)
        return out

    def values_t_probs(j, p, keys=whole):
        start = pl.multiple_of(j * tile, tile) + keys[0]
        return lax.dot_general(v_ref[0, pl.ds(start, keys[1]), :], p, (((0,), (0,)), ((), ())),
                               preferred_element_type=F32)

    l_ref[spare] = jnp.zeros(l_ref.shape[1:], F32)
    acc_ref[spare] = jnp.zeros(acc_ref.shape[1:], F32)

    @pl.when(bounded)
    def _():
        def probs(item, slot, masked):
            qi, j, state = item
            for c, s in enumerate(scores(qi, j, masked)):
                p = jnp.exp2(s)
                l_ref[state, c] += jnp.sum(p, axis=0, keepdims=True)
                p_ref[slot, c] = p.astype(BF16)

        def accumulate(item, slot):
            _, j, state = item
            for c in range(2):
                acc_ref[state, c] += values_t_probs(j, p_ref[slot, c])

        half = tile // 2
        first_keys, last_keys, last_queries = (0, half), (half, half), (half, half)

        def first_half_probs(i):
            for c, s in enumerate(scores(i, i, True, keys=first_keys)):
                p = jnp.exp2(s)
                l_ref[i, c] = jnp.sum(p, axis=0, keepdims=True)
                p_ref[0, c, 0:half, :] = p.astype(BF16)

        def first_half_accumulate(i):
            for c in range(2):
                acc_ref[i, c] = values_t_probs(i, p_ref[0, c, 0:half, :], keys=first_keys)

        def last_half_probs(i):
            for c, s in enumerate(scores(i, i, True, keys=last_keys, queries=last_queries)):
                p = jnp.exp2(s)
                part = jnp.sum(p, axis=0, keepdims=True)
                l_ref[i, c] += jnp.concatenate([jnp.zeros_like(part), part], axis=1)
                p_ref[1, c, 0:half, 0:half] = p.astype(BF16)

        def last_half_accumulate(i):
            for c in range(2):
                acc_ref[i, c, :, half:] += values_t_probs(i, p_ref[1, c, 0:half, 0:half], keys=last_keys)

        def below(cursor):
            d, qi = cursor
            live = d <= reach
            return jnp.where(live, qi, 0), jnp.where(live, qi - d, 0), jnp.where(live, qi, spare)

        def advance(cursor):
            d, qi = cursor
            wrap = qi + 1 >= n_tiles
            d = jnp.where(wrap, d + 1, d)
            return d, jnp.where(wrap, d, qi + 1)

        reach = jnp.minimum(n_reach, n_tiles - 1)
        n_below = reach * n_tiles - lax.shift_right_logical(reach * (reach + 1), 1)

        first_half_probs(0)

        def diagonal_tile(i):
            last_half_probs(i)
            first_half_accumulate(i)
            first_half_probs(i + 1)
            last_half_accumulate(i)

        def diagonal_tiles(t, carry):
            for u in range(DIAG_UNROLL):
                diagonal_tile(DIAG_UNROLL * t + u)
            return carry

        n_looped = (n_tiles - 1) // DIAG_UNROLL * DIAG_UNROLL
        lax.fori_loop(0, n_looped // DIAG_UNROLL, diagonal_tiles, 0)
        for i in range(n_looped, n_tiles - 1):
            diagonal_tile(i)
        first = (jnp.int32(1), jnp.int32(1))
        last_half_probs(n_tiles - 1)
        first_half_accumulate(n_tiles - 1)
        probs(below(first), 0, False)
        last_half_accumulate(n_tiles - 1)

        def below_group(t, cursor):
            for u in range(BELOW_UNROLL):
                nxt = advance(cursor)
                probs(below(nxt), (u + 1) % 2, False)
                accumulate(below(cursor), u % 2)
                cursor = nxt
            return cursor

        lax.fori_loop(0, (n_below + BELOW_UNROLL - 1) // BELOW_UNROLL, below_group, first)

    @pl.when(jnp.logical_not(bounded))
    def _():
        def block(qi, j, masked):
            for c, s in enumerate(scores(qi, j, masked)):
                m_old = m_ref[c]
                m_new = jnp.maximum(m_old, jnp.max(s, axis=0, keepdims=True))
                p = jnp.exp2(s - m_new)
                alpha = jnp.exp2(m_old - m_new)
                l_ref[qi, c] = alpha * l_ref[qi, c] + jnp.sum(p, axis=0, keepdims=True)
                acc_ref[qi, c] = alpha * acc_ref[qi, c] + values_t_probs(j, p.astype(BF16))
                m_ref[c] = m_new

        def query_tile(qi, carry):
            m_ref[...] = jnp.full(m_ref.shape, NEG_BIG, F32)
            l_ref[qi] = jnp.zeros(l_ref.shape[1:], F32)
            acc_ref[qi] = jnp.zeros(acc_ref.shape[1:], F32)
            block(qi, qi, True)

            def body(t, c):
                block(qi, qi - 1 - t, False)
                return c

            lax.fori_loop(0, qi, body, 0)
            return carry

        lax.fori_loop(0, n_tiles, query_tile, 0)

    lam = (jnp.exp(jnp.sum(lq1_ref[...] * lk1_ref[...], axis=-1, keepdims=True))
           - jnp.exp(jnp.sum(lq2_ref[...] * lk2_ref[...], axis=-1, keepdims=True))
           + LAMBDA_INIT)

    def finish(qi, carry):
        o = acc_ref[qi, 0] * (1.0 / l_ref[qi, 0]) - acc_ref[qi, 1] * (lam / l_ref[qi, 1])
        msq = jnp.mean(o * o, axis=0, keepdims=True)
        y = o * lax.rsqrt(msq + EPS) * (sg_ref[...] * (1.0 - LAMBDA_INIT))
        o_ref[0, pl.ds(pl.multiple_of(qi * tile, tile), tile), :] = y.astype(BF16).T
        return carry

    lax.fori_loop(0, n_tiles, finish, 0)


def _merge_ffn_kernel(x_ref, u_ref, halo_ref, a_ref, g_ref, ps_ref, g2_ref,
                      pw_hbm, wa_hbm, wb_hbm, wo_hbm, w1_hbm, w2_hbm, o_ref,
                      ext_ref, mid_ref, pw_ref, wa_ref, wb_ref, wo_ref, w1_ref, w2_ref, stage_ref, sem,
                      *, seq, tm, chunk, ff_chunk):
    i = pl.program_id(0)
    n_ff = w1_ref.shape[1] // ff_chunk

    pieces = [(pw_hbm, pw_ref), (wa_hbm, wa_ref), (wb_hbm, wb_ref), (wo_hbm, wo_ref)]
    for c in range(n_ff):
        cols = slice(c * ff_chunk, (c + 1) * ff_chunk)
        pieces.append((w1_hbm.at[:, cols], w1_ref.at[:, cols]))
        pieces.append((w2_hbm.at[cols, :], w2_ref.at[cols, :]))

    def staged(c):
        rows, cols = pieces[c][0].shape
        return stage_ref.at[c % 2, 0:rows, 0:cols]

    def fetch(c):
        return pltpu.make_async_copy(pieces[c][0], staged(c), sem.at[c % 2])

    @pl.when(i == 0)
    def _():
        fetch(0).start()
        for c in range(len(pieces)):
            if c + 1 < len(pieces):
                fetch(c + 1).start()
            fetch(c).wait()
            pieces[c][1][...] = staged(c)[...].astype(BF16)

    t0 = (i * tm) % seq
    halo = halo_ref[...]
    ext_ref[0:HALO, :] = jnp.where(t0 == 0, jnp.zeros_like(halo), halo)
    ext_ref[HALO:, :] = u_ref[...]
    d = o_ref.shape[1]

    for r0 in range(0, tm, chunk):
        rows = slice(r0, r0 + chunk)
        t = t0 + r0 + lax.broadcasted_iota(jnp.int32, (chunk, 1), 0)
        mixed = []
        for g, w in enumerate(POOL_WINDOWS):
            y = ext_ref[r0:r0 + chunk + HALO, g * POOL_GROUP:(g + 1) * POOL_GROUP]
            tot = y
            span = 1
            while span < w:
                tot = tot + pltpu.roll(tot, span, axis=0)
                span *= 2
            cnt = jnp.minimum(t + 1, w).astype(F32)
            pooled = tot[HALO:] / cnt - y[HALO:]
            mixed.append(_dot(pooled.astype(BF16), pw_ref[g * POOL_GROUP:(g + 1) * POOL_GROUP, :]))
        mixed = jnp.concatenate(mixed, axis=1) * ps_ref[...]
        ya = _dot(mixed.astype(BF16), wa_ref[...])
        yb = _dot(a_ref[rows, :], wb_ref[...])
        merged = g_ref[rows, 0:d].astype(F32) * ya + g_ref[rows, d:2 * d].astype(F32) * yb
        mid_ref[rows, :] = x_ref[rows, :] + _dot(merged.astype(BF16), wo_ref[...])

    x = mid_ref[...]
    ms = jnp.mean(x * x, axis=-1, keepdims=True)
    h = (x * lax.rsqrt(ms + EPS) * g2_ref[...]).astype(BF16)
    acc = x
    for c in range(n_ff):
        cols = slice(c * ff_chunk, (c + 1) * ff_chunk)
        a = jnp.maximum(_dot(h, w1_ref[:, cols]), 0.0)
        acc = acc + _dot((a * a).astype(BF16), w2_ref[cols, :])
    o_ref[...] = acc


def _alibi_columns(tile):
    slopes2 = np.array([2.0 ** (-8.0 * (h + 1) / N_HEADS) for h in range(N_HEADS)]) * LOG2E
    pos = np.arange(tile, dtype=np.float64)

    def terms(x):
        out = []
        for _ in range(BIAS_TERMS):
            part = x.astype(BF16)
            out.append(part)
            x = x - part.astype(np.float64)
        return out

    kcols = np.zeros((N_HEADS, tile, LANES), BF16)
    qcols = np.zeros((N_HEADS, tile, LANES), BF16)
    k_terms = terms(slopes2[:, None] * pos[None, :])
    q_terms = terms(-slopes2[:, None] * pos[None, :])
    t_terms = terms(np.broadcast_to(slopes2[:, None] * tile, (N_HEADS, tile)))
    for t in range(BIAS_TERMS):
        kcols[:, :, t] = k_terms[t]
        kcols[:, :, BIAS_TERMS + t] = 1.0
        qcols[:, :, t] = 1.0
        qcols[:, :, BIAS_TERMS + t] = q_terms[t]
        qcols[:, :, 2 * BIAS_TERMS + t] = t_terms[t]
    return jnp.asarray(qcols.transpose(0, 2, 1)), jnp.asarray(kcols), slopes2.astype(np.float32)


def _const_spec(shape):
    return pl.BlockSpec(shape, lambda *_: (0,) * len(shape))


def _resident_spec(shape):
    return pl.BlockSpec(shape, lambda *_: (0,) * len(shape), pipeline_mode=pl.Buffered(1))


def _params(n_axes):
    return pltpu.CompilerParams(dimension_semantics=("arbitrary",) * n_axes,
                                vmem_limit_bytes=VMEM_LIMIT)


def kernel(x, norm1_g, w_in, gate_b, pool_w, pool_scale, q_norm_g, k_norm_g, lambda_q1, lambda_k1,
           lambda_q2, lambda_k2, subln_g, w_branch_a, w_branch_b, w_out, norm2_g, w_ff1, w_ff2):
    B, S, D = x.shape
    depth = norm1_g.shape[0]
    assert depth == 1
    M = B * S
    pw = pool_scale.shape[1]
    aw = 2 * N_HEADS * HEAD_DIM
    vw = N_HEADS * V_DIM
    gw = 2 * D
    d_ff = w_ff1.shape[2]
    tm = ROW_TILE
    ta = ATTN_TILE
    assert S % tm == 0 and S % ta == 0 and tm % HALO == 0
    assert D == FFN_CHUNK and d_ff % FFN_CHUNK == 0 and max(pw, vw) <= D
    assert pw == len(POOL_WINDOWS) * POOL_GROUP and w_in.shape[2] == pw + 2 * aw + vw + gw

    x2 = x.reshape(M, D)
    row = lambda width: pl.BlockSpec((tm, width), lambda i: (i, 0))

    assert tm == ta
    n_tiles = S // ta
    reps = aw // HEAD_DIM
    u, qt, k, v, gates = pl.pallas_call(
        functools.partial(_inproj_kernel, widths=(pw, aw, vw, gw), scale=LOG2E / math.sqrt(HEAD_DIM)),
        grid=(M // tm,),
        in_specs=[row(D), _const_spec((1, D)), _resident_spec(w_in.shape[1:]), _const_spec((1, gw)),
                  _const_spec((aw, 1)), _const_spec((aw, 1))],
        out_specs=[row(pw),
                   pl.BlockSpec((1, N_HEADS, 1, 2 * HEAD_DIM, ta),
                                lambda i: (i // n_tiles, 0, i % n_tiles, 0, 0)),
                   row(aw), row(vw), row(gw)],
        out_shape=[jax.ShapeDtypeStruct((M, pw), F32),
                   jax.ShapeDtypeStruct((B, N_HEADS, n_tiles, 2 * HEAD_DIM, ta), BF16),
                   jax.ShapeDtypeStruct((M, aw), BF16), jax.ShapeDtypeStruct((M, vw), BF16),
                   jax.ShapeDtypeStruct((M, gw), BF16)],
        scratch_shapes=[pltpu.VMEM(w_in.shape[1:], BF16)],
        compiler_params=_params(1),
        name="inproj",
    )(x2, norm1_g, w_in[0], gate_b, jnp.tile(q_norm_g, (1, reps)).reshape(aw, 1),
      jnp.tile(k_norm_g, (1, reps)).reshape(aw, 1))

    qcols, kcols, slopes2 = _alibi_columns(ta)

    gain_product = jnp.max(jnp.abs(q_norm_g.reshape(-1, 1) * k_norm_g.reshape(1, -1)))
    bound = (1.01 * LOG2E * math.sqrt(HEAD_DIM)) * gain_product
    bounded = bound <= SCORE_BOUND_MAX
    reach = jnp.floor(((bound - EXP2_ZERO_BELOW) / jnp.asarray(slopes2, F32) - 1.0) / ta) + 1.0
    reach = jnp.clip(jnp.where(bounded, reach, S // ta), 0, S // ta)
    plan = jnp.concatenate([bounded.astype(jnp.int32)[None], reach.astype(jnp.int32)])

    head_seq = pl.BlockSpec((1, S, LANES), lambda b, h: (b, 0, h))
    attn = pl.pallas_call(
        functools.partial(_attn_kernel, tile=ta, n_tiles=n_tiles),
        grid=(B, N_HEADS),
        in_specs=[pl.BlockSpec(memory_space=pltpu.SMEM),
                  pl.BlockSpec((1, 1, n_tiles, 2 * HEAD_DIM, ta), lambda b, h: (b, h, 0, 0, 0)),
                  head_seq, head_seq,
                  pl.BlockSpec((1, LANES, ta), lambda b, h: (h, 0, 0)),
                  pl.BlockSpec((1, ta, LANES), lambda b, h: (h, 0, 0)),
                  _const_spec((1, HEAD_DIM)), _const_spec((1, HEAD_DIM)),
                  _const_spec((1, HEAD_DIM)), _const_spec((1, HEAD_DIM)), _const_spec((V_DIM, 1))],
        out_specs=head_seq,
        out_shape=jax.ShapeDtypeStruct((B, S, vw), BF16),
        scratch_shapes=[pltpu.VMEM((2, 2, ta, ta), BF16),
                        pltpu.VMEM((2, 1, ta), F32),
                        pltpu.VMEM((n_tiles + 1, 2, 1, ta), F32),
                        pltpu.VMEM((n_tiles + 1, 2, V_DIM, ta), F32)],
        compiler_params=_params(2),
        name="diffattn",
    )(plan, qt, k.reshape(B, S, aw), v.reshape(B, S, vw), qcols, kcols,
      lambda_q1, lambda_k1, lambda_q2, lambda_k2, subln_g.reshape(V_DIM, 1))

    in_hbm = pl.BlockSpec(memory_space=pl.ANY)
    out = pl.pallas_call(
        functools.partial(_merge_ffn_kernel, seq=S, tm=tm, chunk=MERGE_CHUNK, ff_chunk=FFN_CHUNK),
        grid=(M // tm,),
        in_specs=[row(D), row(pw),
                  pl.BlockSpec((HALO, pw), lambda i: (jnp.maximum(i * (tm // HALO) - 1, 0), 0)),
                  row(vw), row(gw), _const_spec((1, pw)), _const_spec((1, D)),
                  in_hbm, in_hbm, in_hbm, in_hbm, in_hbm, in_hbm],
        out_specs=row(D),
        out_shape=jax.ShapeDtypeStruct((M, D), F32),
        scratch_shapes=[pltpu.VMEM((HALO + tm, pw), F32),
                        pltpu.VMEM((tm, D), F32),
                        pltpu.VMEM((pw, POOL_GROUP), BF16), pltpu.VMEM((pw, D), BF16),
                        pltpu.VMEM((vw, D), BF16), pltpu.VMEM((D, D), BF16),
                        pltpu.VMEM((D, d_ff), BF16), pltpu.VMEM((d_ff, D), BF16),
                        pltpu.VMEM((2, D, FFN_CHUNK), F32),
                        pltpu.SemaphoreType.DMA((2,))],
        compiler_params=_params(1),
        name="merge_ffn",
    )(x2, u, u, attn.reshape(M, vw), gates, pool_scale, norm2_g,
      pool_w[0].reshape(pw, POOL_GROUP), w_branch_a[0], w_branch_b[0], w_out[0], w_ff1[0], w_ff2[0])
    return out.reshape(B, S, D)
```

```python
import functools
import math

import numpy as np
import jax
import jax.numpy as jnp
from jax import lax
from jax.experimental import pallas as pl
from jax.experimental.pallas import tpu as pltpu

F32 = jnp.float32
BF16 = jnp.bfloat16
EPS = 1e-6

POOL_WINDOWS = (2, 4, 8, 16)
POOL_GROUP = 128
HALO = max(POOL_WINDOWS)
N_HEADS = 4
HEAD_DIM = 64
V_DIM = 2 * HEAD_DIM
LANES = 128
MXU_DIM = 256
LAMBDA_INIT = 0.8 - 0.6 * math.exp(-0.3 * 0)
NEG_BIG = -1e30
LOG2E = math.log2(math.e)
BIAS_TERMS = 3
SCORE_BOUND_MAX = 60.0
EXP2_ZERO_BELOW = -150.0

ROW_TILE = 512
GATE_CHUNK = 512
FFN_CHUNK = 1024
MERGE_CHUNK = 256
ATTN_TILE = 512
DIAG_UNROLL = 3
BELOW_UNROLL = 4
VMEM_LIMIT = 56 * 1024 * 1024


def _dot(a, b):
    return jnp.dot(a, b, preferred_element_type=F32)


def _sigmoid(x):
    return 1.0 / (1.0 + jnp.exp(-x))


def _inproj_kernel(x_ref, g1_ref, w_ref, gb_ref, gq_ref, gk_ref,
                   u_ref, qt_ref, k_ref, vt_ref, g_ref, wbf_ref, *, widths, scale):
    pw, aw, vw, gw = widths

    @pl.when(pl.program_id(0) == 0)
    def _():
        wbf_ref[...] = w_ref[...].astype(BF16)

    x = x_ref[...]
    ms = jnp.mean(x * x, axis=-1, keepdims=True)
    h = (x * lax.rsqrt(ms + EPS) * g1_ref[...]).astype(BF16)

    def proj(lo, width):
        return _dot(h, wbf_ref[:, lo:lo + width])

    def group_rms_t(y, gain_col):
        yt = y.T
        parts = []
        for g in range(y.shape[1] // HEAD_DIM):
            blk = yt[g * HEAD_DIM:(g + 1) * HEAD_DIM, :]
            parts.append(blk * lax.rsqrt(jnp.mean(blk * blk, axis=0, keepdims=True) + EPS))
        return jnp.concatenate(parts, axis=0) * gain_col

    qt = (group_rms_t(proj(pw, aw), gq_ref[...]) * scale).astype(BF16)
    for hd in range(N_HEADS):
        qt_ref[0, hd, 0] = qt[hd * 2 * HEAD_DIM:(hd + 1) * 2 * HEAD_DIM, :]
    k_ref[...] = group_rms_t(proj(pw + aw, aw), gk_ref[...]).astype(BF16).T
    g0 = pw + 2 * aw + vw
    chunk = GATE_CHUNK
    for c in range(gw // chunk):
        gl = proj(g0 + c * chunk, chunk) + gb_ref[:, c * chunk:(c + 1) * chunk]
        g_ref[:, c * chunk:(c + 1) * chunk] = _sigmoid(gl).astype(BF16)
    vt = proj(pw + 2 * aw, vw).astype(BF16).T
    for hd in range(N_HEADS):
        vt_ref[0, hd, 0] = vt[hd * V_DIM:(hd + 1) * V_DIM, :]
    u_ref[...] = proj(0, pw)


def _attn_kernel(plan_ref, qt_ref, k_ref, vt_ref, qc_ref, kc_ref, lq1_ref, lk1_ref, lq2_ref, lk2_ref,
                 sg_ref, o_ref, p_ref, m_ref, l_ref, acc_ref, *, tile, n_tiles):
    h = pl.program_id(1)
    bounded = plan_ref[0] == 1
    n_reach = plan_ref[1 + h]
    spare = n_tiles

    lane1 = lax.broadcasted_iota(jnp.int32, (1, LANES), 1)
    dist_lane = (lane1 >= 2 * BIAS_TERMS) & (lane1 < 3 * BIAS_TERMS)

    whole = (0, tile)

    def scores(qi, j, masked, keys=whole, queries=whole):
        k0, nk = keys
        q0, nq = queries
        qt = qt_ref[0, 0, qi, :, q0:q0 + nq]
        zero = jnp.zeros_like(qt)
        dist = jnp.where(dist_lane, jnp.asarray(j - qi).astype(F32), 0.0).astype(BF16)
        ka = jnp.concatenate([k_ref[0, pl.ds(pl.multiple_of(j * tile, tile) + k0, nk), :],
                              kc_ref[0, k0:k0 + nk, :] + dist], axis=1)
        if masked:
            future = (lax.broadcasted_iota(jnp.int32, (nk, nq), 0) + k0
                      > lax.broadcasted_iota(jnp.int32, (nk, nq), 1) + q0)
        channel = lax.broadcasted_iota(jnp.int32, (2 * HEAD_DIM, nq), 0)
        out = []
        for c in range(2):
            qc = jnp.where(channel >= HEAD_DIM if c else channel < HEAD_DIM, qt, zero)
            qa = jnp.concatenate([qc, qc_ref[0, :, q0:q0 + nq]], axis=0)
            s = _dot(ka, qa)
            out.append(jnp.where(future, NEG_BIG, s) if masked else s)
        return out

    def values_t_probs(j, p, keys=whole):
        return _dot(vt_ref[0, 0, j, :, keys[0]:keys[0] + keys[1]], p)

    l_ref[spare] = jnp.zeros(l_ref.shape[1:], F32)
    acc_ref[spare] = jnp.zeros(acc_ref.shape[1:], F32)

    @pl.when(bounded)
    def _():
        def probs(item, slot, masked):
            qi, j, state = item
            for c, s in enumerate(scores(qi, j, masked)):
                p = jnp.exp2(s)
                l_ref[state, c] += jnp.sum(p, axis=0, keepdims=True)
                p_ref[slot, c] = p.astype(BF16)

        def accumulate(item, slot):
            _, j, state = item
            for c in range(2):
                acc_ref[state, c] += values_t_probs(j, p_ref[slot, c])

        half = tile // 2
        first_keys, last_keys, last_queries = (0, half), (half, half), (half, half)

        def first_half_probs(i):
            for c, s in enumerate(scores(i, i, True, keys=first_keys)):
                p = jnp.exp2(s)
                l_ref[i, c] = jnp.sum(p, axis=0, keepdims=True)
                p_ref[0, c, 0:half, :] = p.astype(BF16)

        def first_half_accumulate(i):
            for c in range(2):
                acc_ref[i, c] = values_t_probs(i, p_ref[0, c, 0:half, :], keys=first_keys)

        def last_half_probs(i):
            for c, s in enumerate(scores(i, i, True, keys=last_keys, queries=last_queries)):
                p = jnp.exp2(s)
                part = jnp.sum(p, axis=0, keepdims=True)
                l_ref[i, c] += jnp.concatenate([jnp.zeros_like(part), part], axis=1)
                p_ref[1, c, 0:half, 0:half] = p.astype(BF16)

        def last_half_accumulate(i):
            for c in range(2):
                acc_ref[i, c, :, half:] += values_t_probs(i, p_ref[1, c, 0:half, 0:half], keys=last_keys)

        def below(cursor):
            d, qi = cursor
            live = d <= reach
            return jnp.where(live, qi, 0), jnp.where(live, qi - d, 0), jnp.where(live, qi, spare)

        def advance(cursor):
            d, qi = cursor
            wrap = qi + 1 >= n_tiles
            d = jnp.where(wrap, d + 1, d)
            return d, jnp.where(wrap, d, qi + 1)

        reach = jnp.minimum(n_reach, n_tiles - 1)
        n_below = reach * n_tiles - lax.shift_right_logical(reach * (reach + 1), 1)

        first_half_probs(0)

        def diagonal_tile(i):
            last_half_probs(i)
            first_half_accumulate(i)
            first_half_probs(i + 1)
            last_half_accumulate(i)

        def diagonal_tiles(t, carry):
            for u in range(DIAG_UNROLL):
                diagonal_tile(DIAG_UNROLL * t + u)
            return carry

        n_looped = (n_tiles - 1) // DIAG_UNROLL * DIAG_UNROLL
        lax.fori_loop(0, n_looped // DIAG_UNROLL, diagonal_tiles, 0)
        for i in range(n_looped, n_tiles - 1):
            diagonal_tile(i)
        first = (jnp.int32(1), jnp.int32(1))
        last_half_probs(n_tiles - 1)
        first_half_accumulate(n_tiles - 1)
        probs(below(first), 0, False)
        last_half_accumulate(n_tiles - 1)

        def below_group(t, cursor):
            for u in range(BELOW_UNROLL):
                nxt = advance(cursor)
                probs(below(nxt), (u + 1) % 2, False)
                accumulate(below(cursor), u % 2)
                cursor = nxt
            return cursor

        lax.fori_loop(0, (n_below + BELOW_UNROLL - 1) // BELOW_UNROLL, below_group, first)

    @pl.when(jnp.logical_not(bounded))
    def _():
        def block(qi, j, masked):
            for c, s in enumerate(scores(qi, j, masked)):
                m_old = m_ref[c]
                m_new = jnp.maximum(m_old, jnp.max(s, axis=0, keepdims=True))
                p = jnp.exp2(s - m_new)
                alpha = jnp.exp2(m_old - m_new)
                l_ref[qi, c] = alpha * l_ref[qi, c] + jnp.sum(p, axis=0, keepdims=True)
                acc_ref[qi, c] = alpha * acc_ref[qi, c] + values_t_probs(j, p.astype(BF16))
                m_ref[c] = m_new

        def query_tile(qi, carry):
            m_ref[...] = jnp.full(m_ref.shape, NEG_BIG, F32)
            l_ref[qi] = jnp.zeros(l_ref.shape[1:], F32)
            acc_ref[qi] = jnp.zeros(acc_ref.shape[1:], F32)
            block(qi, qi, True)

            def body(t, c):
                block(qi, qi - 1 - t, False)
                return c

            lax.fori_loop(0, qi, body, 0)
            return carry

        lax.fori_loop(0, n_tiles, query_tile, 0)

    lam = (jnp.exp(jnp.sum(lq1_ref[...] * lk1_ref[...], axis=-1, keepdims=True))
           - jnp.exp(jnp.sum(lq2_ref[...] * lk2_ref[...], axis=-1, keepdims=True))
           + LAMBDA_INIT)

    def finish(qi, carry):
        o = acc_ref[qi, 0] * (1.0 / l_ref[qi, 0]) - acc_ref[qi, 1] * (lam / l_ref[qi, 1])
        msq = jnp.mean(o * o, axis=0, keepdims=True)
        y = o * lax.rsqrt(msq + EPS) * (sg_ref[...] * (1.0 - LAMBDA_INIT))
        o_ref[0, 0, qi] = y.astype(BF16)
        return carry

    lax.fori_loop(0, n_tiles, finish, 0)


def _merge_ffn_kernel(x_ref, u_ref, halo_ref, a_ref, g_ref, ps_ref, g2_ref,
                      pw_hbm, wa_hbm, wb_hbm, wo_hbm, w1_hbm, w2_hbm, o_ref,
                      ext_ref, mid_ref, pw_ref, wa_ref, wb_ref, wo_ref, w1_ref, w2_ref, stage_ref, sem,
                      *, seq, tm, chunk, ff_chunk):
    i = pl.program_id(0)
    n_ff = w1_ref.shape[1] // ff_chunk

    pieces = [(pw_hbm, pw_ref), (wa_hbm, wa_ref), (wb_hbm, wb_ref), (wo_hbm, wo_ref)]
    for c in range(n_ff):
        cols = slice(c * ff_chunk, (c + 1) * ff_chunk)
        pieces.append((w1_hbm.at[:, cols], w1_ref.at[:, cols]))
        pieces.append((w2_hbm.at[cols, :], w2_ref.at[cols, :]))

    def staged(c):
        rows, cols = pieces[c][0].shape
        return stage_ref.at[c % 2, 0:rows, 0:cols]

    def fetch(c):
        return pltpu.make_async_copy(pieces[c][0], staged(c), sem.at[c % 2])

    @pl.when(i == 0)
    def _():
        fetch(0).start()
        for c in range(len(pieces)):
            if c + 1 < len(pieces):
                fetch(c + 1).start()
            fetch(c).wait()
            pieces[c][1][...] = staged(c)[...].astype(BF16)

    t0 = (i * tm) % seq
    halo = halo_ref[...]
    ext_ref[0:HALO, :] = jnp.where(t0 == 0, jnp.zeros_like(halo), halo)
    ext_ref[HALO:, :] = u_ref[...]
    d = o_ref.shape[1]

    for r0 in range(0, tm, chunk):
        rows = slice(r0, r0 + chunk)
        t = t0 + r0 + lax.broadcasted_iota(jnp.int32, (chunk, 1), 0)
        mixed = []
        for g, w in enumerate(POOL_WINDOWS):
            y = ext_ref[r0:r0 + chunk + HALO, g * POOL_GROUP:(g + 1) * POOL_GROUP]
            tot = y
            span = 1
            while span < w:
                tot = tot + pltpu.roll(tot, span, axis=0)
                span *= 2
            cnt = jnp.minimum(t + 1, w).astype(F32)
            pooled = tot[HALO:] / cnt - y[HALO:]
            mixed.append(_dot(pooled.astype(BF16), pw_ref[g * POOL_GROUP:(g + 1) * POOL_GROUP, :]))
        mixed = jnp.concatenate(mixed, axis=1) * ps_ref[...]
        ya = _dot(mixed.astype(BF16), wa_ref[...])
        at = jnp.concatenate([a_ref[0, hd, 0, :, rows] for hd in range(N_HEADS)], axis=0)
        yb = lax.dot_general(at, wb_ref[...], (((0,), (0,)), ((), ())), preferred_element_type=F32)
        merged = g_ref[rows, 0:d].astype(F32) * ya + g_ref[rows, d:2 * d].astype(F32) * yb
        mid_ref[rows, :] = x_ref[rows, :] + _dot(merged.astype(BF16), wo_ref[...])

    x = mid_ref[...]
    ms = jnp.mean(x * x, axis=-1, keepdims=True)
    h = (x * lax.rsqrt(ms + EPS) * g2_ref[...]).astype(BF16)
    acc = x
    for c in range(n_ff):
        cols = slice(c * ff_chunk, (c + 1) * ff_chunk)
        a = jnp.maximum(_dot(h, w1_ref[:, cols]), 0.0)
        acc = acc + _dot((a * a).astype(BF16), w2_ref[cols, :])
    o_ref[...] = acc


def _alibi_columns(tile):
    slopes2 = np.array([2.0 ** (-8.0 * (h + 1) / N_HEADS) for h in range(N_HEADS)]) * LOG2E
    pos = np.arange(tile, dtype=np.float64)

    def terms(x):
        out = []
        for _ in range(BIAS_TERMS):
            part = x.astype(BF16)
            out.append(part)
            x = x - part.astype(np.float64)
        return out

    kcols = np.zeros((N_HEADS, tile, LANES), BF16)
    qcols = np.zeros((N_HEADS, tile, LANES), BF16)
    k_terms = terms(slopes2[:, None] * pos[None, :])
    q_terms = terms(-slopes2[:, None] * pos[None, :])
    t_terms = terms(np.broadcast_to(slopes2[:, None] * tile, (N_HEADS, tile)))
    for t in range(BIAS_TERMS):
        kcols[:, :, t] = k_terms[t]
        kcols[:, :, BIAS_TERMS + t] = 1.0
        qcols[:, :, t] = 1.0
        qcols[:, :, BIAS_TERMS + t] = q_terms[t]
        qcols[:, :, 2 * BIAS_TERMS + t] = t_terms[t]
    return jnp.asarray(qcols.transpose(0, 2, 1)), jnp.asarray(kcols), slopes2.astype(np.float32)


def _const_spec(shape):
    return pl.BlockSpec(shape, lambda *_: (0,) * len(shape))


def _resident_spec(shape):
    return pl.BlockSpec(shape, lambda *_: (0,) * len(shape), pipeline_mode=pl.Buffered(1))


def _params(n_axes):
    return pltpu.CompilerParams(dimension_semantics=("arbitrary",) * n_axes,
                                vmem_limit_bytes=VMEM_LIMIT)


def kernel(x, norm1_g, w_in, gate_b, pool_w, pool_scale, q_norm_g, k_norm_g, lambda_q1, lambda_k1,
           lambda_q2, lambda_k2, subln_g, w_branch_a, w_branch_b, w_out, norm2_g, w_ff1, w_ff2):
    B, S, D = x.shape
    depth = norm1_g.shape[0]
    assert depth == 1
    M = B * S
    pw = pool_scale.shape[1]
    aw = 2 * N_HEADS * HEAD_DIM
    vw = N_HEADS * V_DIM
    gw = 2 * D
    d_ff = w_ff1.shape[2]
    tm = ROW_TILE
    ta = ATTN_TILE
    assert S % tm == 0 and S % ta == 0 and tm % HALO == 0
    assert D == FFN_CHUNK and d_ff % FFN_CHUNK == 0 and max(pw, vw) <= D
    assert pw == len(POOL_WINDOWS) * POOL_GROUP and w_in.shape[2] == pw + 2 * aw + vw + gw

    x2 = x.reshape(M, D)
    row = lambda width: pl.BlockSpec((tm, width), lambda i: (i, 0))

    assert tm == ta
    n_tiles = S // ta
    reps = aw // HEAD_DIM
    assert V_DIM == 2 * HEAD_DIM
    slabs = jax.ShapeDtypeStruct((B, N_HEADS, n_tiles, V_DIM, ta), BF16)
    slab_out = pl.BlockSpec((1, N_HEADS, 1, V_DIM, ta), lambda i: (i // n_tiles, 0, i % n_tiles, 0, 0))
    u, qt, k, vt, gates = pl.pallas_call(
        functools.partial(_inproj_kernel, widths=(pw, aw, vw, gw), scale=LOG2E / math.sqrt(HEAD_DIM)),
        grid=(M // tm,),
        in_specs=[row(D), _const_spec((1, D)), _resident_spec(w_in.shape[1:]), _const_spec((1, gw)),
                  _const_spec((aw, 1)), _const_spec((aw, 1))],
        out_specs=[row(pw), slab_out, row(aw), slab_out, row(gw)],
        out_shape=[jax.ShapeDtypeStruct((M, pw), F32), slabs, jax.ShapeDtypeStruct((M, aw), BF16), slabs,
                   jax.ShapeDtypeStruct((M, gw), BF16)],
        scratch_shapes=[pltpu.VMEM(w_in.shape[1:], BF16)],
        compiler_params=_params(1),
        name="inproj",
    )(x2, norm1_g, w_in[0], gate_b, jnp.tile(q_norm_g, (1, reps)).reshape(aw, 1),
      jnp.tile(k_norm_g, (1, reps)).reshape(aw, 1))

    qcols, kcols, slopes2 = _alibi_columns(ta)

    gain_product = jnp.max(jnp.abs(q_norm_g.reshape(-1, 1) * k_norm_g.reshape(1, -1)))
    bound = (1.01 * LOG2E * math.sqrt(HEAD_DIM)) * gain_product
    bounded = bound <= SCORE_BOUND_MAX
    reach = jnp.floor(((bound - EXP2_ZERO_BELOW) / jnp.asarray(slopes2, F32) - 1.0) / ta) + 1.0
    reach = jnp.clip(jnp.where(bounded, reach, S // ta), 0, S // ta)
    plan = jnp.concatenate([bounded.astype(jnp.int32)[None], reach.astype(jnp.int32)])

    head_seq = pl.BlockSpec((1, S, LANES), lambda b, h: (b, 0, h))
    head_slabs = pl.BlockSpec((1, 1, n_tiles, V_DIM, ta), lambda b, h: (b, h, 0, 0, 0))
    attn = pl.pallas_call(
        functools.partial(_attn_kernel, tile=ta, n_tiles=n_tiles),
        grid=(B, N_HEADS),
        in_specs=[pl.BlockSpec(memory_space=pltpu.SMEM),
                  head_slabs, head_seq, head_slabs,
                  pl.BlockSpec((1, LANES, ta), lambda b, h: (h, 0, 0)),
                  pl.BlockSpec((1, ta, LANES), lambda b, h: (h, 0, 0)),
                  _const_spec((1, HEAD_DIM)), _const_spec((1, HEAD_DIM)),
                  _const_spec((1, HEAD_DIM)), _const_spec((1, HEAD_DIM)), _const_spec((V_DIM, 1))],
        out_specs=head_slabs,
        out_shape=slabs,
        scratch_shapes=[pltpu.VMEM((2, 2, ta, ta), BF16),
                        pltpu.VMEM((2, 1, ta), F32),
                        pltpu.VMEM((n_tiles + 1, 2, 1, ta), F32),
                        pltpu.VMEM((n_tiles + 1, 2, V_DIM, ta), F32)],
        compiler_params=_params(2),
        name="diffattn",
    )(plan, qt, k.reshape(B, S, aw), vt, qcols, kcols,
      lambda_q1, lambda_k1, lambda_q2, lambda_k2, subln_g.reshape(V_DIM, 1))

    in_hbm = pl.BlockSpec(memory_space=pl.ANY)
    out = pl.pallas_call(
        functools.partial(_merge_ffn_kernel, seq=S, tm=tm, chunk=MERGE_CHUNK, ff_chunk=FFN_CHUNK),
        grid=(M // tm,),
        in_specs=[row(D), row(pw),
                  pl.BlockSpec((HALO, pw), lambda i: (jnp.maximum(i * (tm // HALO) - 1, 0), 0)),
                  slab_out, row(gw), _const_spec((1, pw)), _const_spec((1, D)),
                  in_hbm, in_hbm, in_hbm, in_hbm, in_hbm, in_hbm],
        out_specs=row(D),
        out_shape=jax.ShapeDtypeStruct((M, D), F32),
        scratch_shapes=[pltpu.VMEM((HALO + tm, pw), F32),
                        pltpu.VMEM((tm, D), F32),
                        pltpu.VMEM((pw, POOL_GROUP), BF16), pltpu.VMEM((pw, D), BF16),
                        pltpu.VMEM((vw, D), BF16), pltpu.VMEM((D, D), BF16),
                        pltpu.VMEM((D, d_ff), BF16), pltpu.VMEM((d_ff, D), BF16),
                        pltpu.VMEM((2, D, FFN_CHUNK), F32),
                        pltpu.SemaphoreType.DMA((2,))],
        compiler_params=_params(1),
        name="merge_ffn",
    )(x2, u, u, attn, gates, pool_scale, norm2_g,
      pool_w[0].reshape(pw, POOL_GROUP), w_branch_a[0], w_branch_b[0], w_out[0], w_ff1[0], w_ff2[0])
    return out.reshape(B, S, D)
```

```python
import functools
import math

import numpy as np
import jax
import jax.numpy as jnp
from jax import lax
from jax.experimental import pallas as pl
from jax.experimental.pallas import tpu as pltpu

F32 = jnp.float32
BF16 = jnp.bfloat16
EPS = 1e-6

POOL_WINDOWS = (2, 4, 8, 16)
POOL_GROUP = 128
HALO = max(POOL_WINDOWS)
N_HEADS = 4
HEAD_DIM = 64
V_DIM = 2 * HEAD_DIM
LANES = 128
BF16_SUBLANES = 16
LAMBDA_INIT = 0.8 - 0.6 * math.exp(-0.3 * 0)
NEG_BIG = -1e30
LOG2E = math.log2(math.e)
BIAS_TERMS = 3
SCORE_BOUND_MAX = 60.0
EXP2_ZERO_BELOW = -150.0

ROW_TILE = 512
GATE_CHUNK = 512
FFN_CHUNK = 1024
MERGE_CHUNK = 256
ATTN_TILE = 512
DIAG_UNROLL = 3
BELOW_UNROLL = 8
VMEM_LIMIT = 56 * 1024 * 1024


def _dot(a, b):
    return jnp.dot(a, b, preferred_element_type=F32)


def _sigmoid(x):
    return 1.0 / (1.0 + jnp.exp(-x))


def _inproj_kernel(x_ref, g1_ref, w_ref, gb_ref, gq_ref, gk_ref, *rest, widths, scale, n_later):
    later_f32, rest = rest[:n_later], rest[n_later:]
    u_ref, qt_ref, k_ref, v_ref, g_ref = rest[:5]
    later_bf16, wbf_ref = rest[5:5 + n_later], rest[5 + n_later]
    pw, aw, vw, gw = widths

    for src, dst in zip(later_f32, later_bf16):
        dst[...] = src[...].astype(BF16)

    @pl.when(pl.program_id(0) == 0)
    def _():
        wbf_ref[...] = w_ref[...].astype(BF16)

    x = x_ref[...]
    ms = jnp.mean(x * x, axis=-1, keepdims=True)
    h = (x * lax.rsqrt(ms + EPS) * g1_ref[...]).astype(BF16)

    def proj(lo, width):
        return _dot(h, wbf_ref[:, lo:lo + width])

    def group_rms_t(y, gain_col):
        yt = y.T
        parts = []
        for g in range(y.shape[1] // HEAD_DIM):
            blk = yt[g * HEAD_DIM:(g + 1) * HEAD_DIM, :]
            parts.append(blk * lax.rsqrt(jnp.mean(blk * blk, axis=0, keepdims=True) + EPS))
        return jnp.concatenate(parts, axis=0) * gain_col

    qt = (group_rms_t(proj(pw, aw), gq_ref[...]) * scale).astype(BF16)
    for hd in range(N_HEADS):
        qt_ref[0, hd, 0] = qt[hd * 2 * HEAD_DIM:(hd + 1) * 2 * HEAD_DIM, :]
    k_ref[...] = group_rms_t(proj(pw + aw, aw), gk_ref[...]).astype(BF16).T
    g0 = pw + 2 * aw + vw
    chunk = GATE_CHUNK
    for c in range(gw // chunk):
        gl = proj(g0 + c * chunk, chunk) + gb_ref[:, c * chunk:(c + 1) * chunk]
        g_ref[:, c * chunk:(c + 1) * chunk] = _sigmoid(gl).astype(BF16)
    v_ref[...] = proj(pw + 2 * aw, vw).astype(BF16)
    u_ref[...] = proj(0, pw)


def _attn_kernel(plan_ref, qt_ref, k_ref, v_ref, qc_ref, kc_ref, lq1_ref, lk1_ref, lq2_ref, lk2_ref,
                 sg_ref, o_ref, p_ref, m_ref, l_ref, acc_ref, *, tile, n_tiles):
    h = pl.program_id(1)
    bounded = plan_ref[0] == 1
    n_reach = plan_ref[1 + h]
    spare = n_tiles

    lane1 = lax.broadcasted_iota(jnp.int32, (1, LANES), 1)
    dist_lane = (lane1 >= 2 * BIAS_TERMS) & (lane1 < 3 * BIAS_TERMS)

    whole = (0, tile)

    def scores(qi, j, masked, keys=whole, queries=whole):
        k0, nk = keys
        q0, nq = queries
        qt = qt_ref[0, 0, qi, :, q0:q0 + nq]
        zero = jnp.zeros_like(qt)
        dist = jnp.where(dist_lane, jnp.asarray(j - qi).astype(F32), 0.0).astype(BF16)
        ka = jnp.concatenate([k_ref[0, pl.ds(pl.multiple_of(j * tile, tile) + k0, nk), :],
                              kc_ref[0, k0:k0 + nk, :] + dist], axis=1)
        if masked:
            future = (lax.broadcasted_iota(jnp.int32, (nk, nq), 0) + k0
                      > lax.broadcasted_iota(jnp.int32, (nk, nq), 1) + q0)
        channel = lax.broadcasted_iota(jnp.int32, (2 * HEAD_DIM, nq), 0)
        out = []
        for c in range(2):
            qc = jnp.where(channel >= HEAD_DIM if c else channel < HEAD_DIM, qt, zero)
            qa = jnp.concatenate([qc, qc_ref[0, :, q0:q0 + nq]], axis=0)
            s = _dot(ka, qa)
            out.append(jnp.where(future, NEG_BIG, s) if masked else s)
        return out

    def values_t_probs(j, p, keys=whole):
        start = pl.multiple_of(j * tile, tile) + keys[0]
        return lax.dot_general(v_ref[0, pl.ds(start, keys[1]), :], p, (((0,), (0,)), ((), ())),
                               preferred_element_type=F32)

    l_ref[spare] = jnp.zeros(l_ref.shape[1:], F32)
    acc_ref[spare] = jnp.zeros(acc_ref.shape[1:], F32)

    @pl.when(bounded)
    def _():
        def probs(item, slot, masked):
            qi, j, state = item
            for c, s in enumerate(scores(qi, j, masked)):
                p = jnp.exp2(s)
                l_ref[state, c] += jnp.sum(p, axis=0, keepdims=True)
                p_ref[slot, c] = p.astype(BF16)

        def accumulate(item, slot):
            _, j, state = item
            for c in range(2):
                acc_ref[state, c] += values_t_probs(j, p_ref[slot, c])

        half = tile // 2
        first_keys, last_keys, last_queries = (0, half), (half, half), (half, half)

        def first_half_probs(i):
            for c, s in enumerate(scores(i, i, True, keys=first_keys)):
                p = jnp.exp2(s)
                l_ref[i, c] = jnp.sum(p, axis=0, keepdims=True)
                p_ref[0, c, 0:half, :] = p.astype(BF16)

        def first_half_accumulate(i):
            for c in range(2):
                acc_ref[i, c] = values_t_probs(i, p_ref[0, c, 0:half, :], keys=first_keys)

        def last_half_probs(i):
            for c, s in enumerate(scores(i, i, True, keys=last_keys, queries=last_queries)):
                p = jnp.exp2(s)
                part = jnp.sum(p, axis=0, keepdims=True)
                l_ref[i, c] += jnp.concatenate([jnp.zeros_like(part), part], axis=1)
                p_ref[1, c, 0:half, 0:half] = p.astype(BF16)

        def last_half_accumulate(i):
            for c in range(2):
                acc_ref[i, c, :, half:] += values_t_probs(i, p_ref[1, c, 0:half, 0:half], keys=last_keys)

        def below(cursor):
            d, qi = cursor
            live = d <= reach
            return jnp.where(live, qi, 0), jnp.where(live, qi - d, 0), jnp.where(live, qi, spare)

        def advance(cursor):
            d, qi = cursor
            wrap = qi + 1 >= n_tiles
            d = jnp.where(wrap, d + 1, d)
            return d, jnp.where(wrap, d, qi + 1)

        reach = jnp.minimum(n_reach, n_tiles - 1)
        n_below = reach * n_tiles - lax.shift_right_logical(reach * (reach + 1), 1)

        first_half_probs(0)

        def diagonal_tile(i):
            last_half_probs(i)
            first_half_accumulate(i)
            first_half_probs(i + 1)
            last_half_accumulate(i)

        def diagonal_tiles(t, carry):
            for u in range(DIAG_UNROLL):
                diagonal_tile(DIAG_UNROLL * t + u)
            return carry

        n_looped = (n_tiles - 1) // DIAG_UNROLL * DIAG_UNROLL
        lax.fori_loop(0, n_looped // DIAG_UNROLL, diagonal_tiles, 0)
        for i in range(n_looped, n_tiles - 1):
            diagonal_tile(i)
        first = (jnp.int32(1), jnp.int32(1))
        last_half_probs(n_tiles - 1)
        first_half_accumulate(n_tiles - 1)
        probs(below(first), 0, False)
        last_half_accumulate(n_tiles - 1)

        def below_group(t, cursor):
            for u in range(BELOW_UNROLL):
                nxt = advance(cursor)
                probs(below(nxt), (u + 1) % 2, False)
                accumulate(below(cursor), u % 2)
                cursor = nxt
            return cursor

        lax.fori_loop(0, (n_below + BELOW_UNROLL - 1) // BELOW_UNROLL, below_group, first)

    @pl.when(jnp.logical_not(bounded))
    def _():
        def block(qi, j, masked):
            for c, s in enumerate(scores(qi, j, masked)):
                m_old = m_ref[c]
                m_new = jnp.maximum(m_old, jnp.max(s, axis=0, keepdims=True))
                p = jnp.exp2(s - m_new)
                alpha = jnp.exp2(m_old - m_new)
                l_ref[qi, c] = alpha * l_ref[qi, c] + jnp.sum(p, axis=0, keepdims=True)
                acc_ref[qi, c] = alpha * acc_ref[qi, c] + values_t_probs(j, p.astype(BF16))
                m_ref[c] = m_new

        def query_tile(qi, carry):
            m_ref[...] = jnp.full(m_ref.shape, NEG_BIG, F32)
            l_ref[qi] = jnp.zeros(l_ref.shape[1:], F32)
            acc_ref[qi] = jnp.zeros(acc_ref.shape[1:], F32)
            block(qi, qi, True)

            def body(t, c):
                block(qi, qi - 1 - t, False)
                return c

            lax.fori_loop(0, qi, body, 0)
            return carry

        lax.fori_loop(0, n_tiles, query_tile, 0)

    lam = (jnp.exp(jnp.sum(lq1_ref[...] * lk1_ref[...], axis=-1, keepdims=True))
           - jnp.exp(jnp.sum(lq2_ref[...] * lk2_ref[...], axis=-1, keepdims=True))
           + LAMBDA_INIT)

    def finish(qi, carry):
        o = acc_ref[qi, 0] * (1.0 / l_ref[qi, 0]) - acc_ref[qi, 1] * (lam / l_ref[qi, 1])
        msq = jnp.mean(o * o, axis=0, keepdims=True)
        y = o * lax.rsqrt(msq + EPS) * (sg_ref[...] * (1.0 - LAMBDA_INIT))
        o_ref[0, pl.ds(pl.multiple_of(qi * tile, tile), tile), :] = y.astype(BF16).T
        return carry

    lax.fori_loop(0, n_tiles, finish, 0)


def _merge_ffn_kernel(x_ref, u_ref, halo_ref, a_ref, g_ref, ps_ref, g2_ref,
                      pw_ref, wa_ref, wb_ref, wo_ref, w1_ref, w2_ref, o_ref, ext_ref, mid_ref,
                      *, seq, tm, chunk, ff_chunk):
    i = pl.program_id(0)
    n_ff = w1_ref.shape[1] // ff_chunk
    t0 = (i * tm) % seq
    halo = halo_ref[...]
    ext_ref[0:HALO, :] = jnp.where(t0 == 0, jnp.zeros_like(halo), halo)
    ext_ref[HALO:, :] = u_ref[...]
    d = o_ref.shape[1]

    for r0 in range(0, tm, chunk):
        rows = slice(r0, r0 + chunk)
        t = t0 + r0 + lax.broadcasted_iota(jnp.int32, (chunk, 1), 0)
        mixed = []
        for g, w in enumerate(POOL_WINDOWS):
            y = ext_ref[r0:r0 + chunk + HALO, g * POOL_GROUP:(g + 1) * POOL_GROUP]
            tot = y
            span = 1
            while span < w:
                tot = tot + pltpu.roll(tot, span, axis=0)
                span *= 2
            cnt = jnp.minimum(t + 1, w).astype(F32)
            pooled = tot[HALO:] / cnt - y[HALO:]
            mixed.append(_dot(pooled.astype(BF16), pw_ref[g * POOL_GROUP:(g + 1) * POOL_GROUP, :]))
        mixed = jnp.concatenate(mixed, axis=1) * ps_ref[...]
        ya = _dot(mixed.astype(BF16), wa_ref[...])
        yb = _dot(a_ref[rows, :], wb_ref[...])
        merged = g_ref[rows, 0:d].astype(F32) * ya + g_ref[rows, d:2 * d].astype(F32) * yb
        mid_ref[rows, :] = x_ref[rows, :] + _dot(merged.astype(BF16), wo_ref[...])

    x = mid_ref[...]
    ms = jnp.mean(x * x, axis=-1, keepdims=True)
    h = (x * lax.rsqrt(ms + EPS) * g2_ref[...]).astype(BF16)
    acc = x
    for c in range(n_ff):
        cols = slice(c * ff_chunk, (c + 1) * ff_chunk)
        a = jnp.maximum(_dot(h, w1_ref[:, cols]), 0.0)
        acc = acc + _dot((a * a).astype(BF16), w2_ref[cols, :])
    o_ref[...] = acc


def _alibi_columns(tile):
    slopes2 = np.array([2.0 ** (-8.0 * (h + 1) / N_HEADS) for h in range(N_HEADS)]) * LOG2E
    pos = np.arange(tile, dtype=np.float64)

    def terms(x):
        out = []
        for _ in range(BIAS_TERMS):
            part = x.astype(BF16)
            out.append(part)
            x = x - part.astype(np.float64)
        return out

    kcols = np.zeros((N_HEADS, tile, LANES), BF16)
    qcols = np.zeros((N_HEADS, tile, LANES), BF16)
    k_terms = terms(slopes2[:, None] * pos[None, :])
    q_terms = terms(-slopes2[:, None] * pos[None, :])
    t_terms = terms(np.broadcast_to(slopes2[:, None] * tile, (N_HEADS, tile)))
    for t in range(BIAS_TERMS):
        kcols[:, :, t] = k_terms[t]
        kcols[:, :, BIAS_TERMS + t] = 1.0
        qcols[:, :, t] = 1.0
        qcols[:, :, BIAS_TERMS + t] = q_terms[t]
        qcols[:, :, 2 * BIAS_TERMS + t] = t_terms[t]
    return jnp.asarray(qcols.transpose(0, 2, 1)), jnp.asarray(kcols), slopes2.astype(np.float32)


def _const_spec(shape):
    return pl.BlockSpec(shape, lambda *_: (0,) * len(shape))


def _resident_spec(shape):
    return pl.BlockSpec(shape, lambda *_: (0,) * len(shape), pipeline_mode=pl.Buffered(1))


def _params(n_axes):
    return pltpu.CompilerParams(dimension_semantics=("arbitrary",) * n_axes,
                                vmem_limit_bytes=VMEM_LIMIT)


def kernel(x, norm1_g, w_in, gate_b, pool_w, pool_scale, q_norm_g, k_norm_g, lambda_q1, lambda_k1,
           lambda_q2, lambda_k2, subln_g, w_branch_a, w_branch_b, w_out, norm2_g, w_ff1, w_ff2):
    B, S, D = x.shape
    depth = norm1_g.shape[0]
    assert depth == 1
    M = B * S
    pw = pool_scale.shape[1]
    aw = 2 * N_HEADS * HEAD_DIM
    vw = N_HEADS * V_DIM
    gw = 2 * D
    d_ff = w_ff1.shape[2]
    tm = ROW_TILE
    ta = ATTN_TILE
    assert S % tm == 0 and S % ta == 0 and tm % HALO == 0
    assert d_ff % FFN_CHUNK == 0
    assert pw == len(POOL_WINDOWS) * POOL_GROUP and w_in.shape[2] == pw + 2 * aw + vw + gw

    x2 = x.reshape(M, D)
    row = lambda width: pl.BlockSpec((tm, width), lambda i: (i, 0))

    assert tm == ta
    n_tiles = S // ta
    reps = aw // HEAD_DIM
    steps = M // tm
    later = [pool_w[0].reshape(pw, POOL_GROUP), w_branch_a[0], w_branch_b[0], w_out[0], w_ff1[0], w_ff2[0]]
    assert all(w.shape[0] % (steps * BF16_SUBLANES) == 0 for w in later)
    row_slice = lambda w: pl.BlockSpec((w.shape[0] // steps, w.shape[1]), lambda i: (i, 0))
    u, qt, k, v, gates, *later_bf16 = pl.pallas_call(
        functools.partial(_inproj_kernel, widths=(pw, aw, vw, gw), scale=LOG2E / math.sqrt(HEAD_DIM),
                          n_later=len(later)),
        grid=(steps,),
        in_specs=[row(D), _const_spec((1, D)), _resident_spec(w_in.shape[1:]), _const_spec((1, gw)),
                  _const_spec((aw, 1)), _const_spec((aw, 1))] + [row_slice(w) for w in later],
        out_specs=[row(pw),
                   pl.BlockSpec((1, N_HEADS, 1, 2 * HEAD_DIM, ta),
                                lambda i: (i // n_tiles, 0, i % n_tiles, 0, 0)),
                   row(aw), row(vw), row(gw)] + [row_slice(w) for w in later],
        out_shape=[jax.ShapeDtypeStruct((M, pw), F32),
                   jax.ShapeDtypeStruct((B, N_HEADS, n_tiles, 2 * HEAD_DIM, ta), BF16),
                   jax.ShapeDtypeStruct((M, aw), BF16), jax.ShapeDtypeStruct((M, vw), BF16),
                   jax.ShapeDtypeStruct((M, gw), BF16)]
                  + [jax.ShapeDtypeStruct(w.shape, BF16) for w in later],
        scratch_shapes=[pltpu.VMEM(w_in.shape[1:], BF16)],
        compiler_params=_params(1),
        name="inproj",
    )(x2, norm1_g, w_in[0], gate_b, jnp.tile(q_norm_g, (1, reps)).reshape(aw, 1),
      jnp.tile(k_norm_g, (1, reps)).reshape(aw, 1), *later)

    qcols, kcols, slopes2 = _alibi_columns(ta)

    gain_product = jnp.max(jnp.abs(q_norm_g.reshape(-1, 1) * k_norm_g.reshape(1, -1)))
    bound = (1.01 * LOG2E * math.sqrt(HEAD_DIM)) * gain_product
    bounded = bound <= SCORE_BOUND_MAX
    reach = jnp.floor(((bound - EXP2_ZERO_BELOW) / jnp.asarray(slopes2, F32) - 1.0) / ta) + 1.0
    reach = jnp.clip(jnp.where(bounded, reach, S // ta), 0, S // ta)
    plan = jnp.concatenate([bounded.astype(jnp.int32)[None], reach.astype(jnp.int32)])

    head_seq = pl.BlockSpec((1, S, LANES), lambda b, h: (b, 0, h))
    attn = pl.pallas_call(
        functools.partial(_attn_kernel, tile=ta, n_tiles=n_tiles),
        grid=(B, N_HEADS),
        in_specs=[pl.BlockSpec(memory_space=pltpu.SMEM),
                  pl.BlockSpec((1, 1, n_tiles, 2 * HEAD_DIM, ta), lambda b, h: (b, h, 0, 0, 0)),
                  head_seq, head_seq,
                  pl.BlockSpec((1, LANES, ta), lambda b, h: (h, 0, 0)),
                  pl.BlockSpec((1, ta, LANES), lambda b, h: (h, 0, 0)),
                  _const_spec((1, HEAD_DIM)), _const_spec((1, HEAD_DIM)),
                  _const_spec((1, HEAD_DIM)), _const_spec((1, HEAD_DIM)), _const_spec((V_DIM, 1))],
        out_specs=head_seq,
        out_shape=jax.ShapeDtypeStruct((B, S, vw), BF16),
        scratch_shapes=[pltpu.VMEM((2, 2, ta, ta), BF16),
                        pltpu.VMEM((2, 1, ta), F32),
                        pltpu.VMEM((n_tiles + 1, 2, 1, ta), F32),
                        pltpu.VMEM((n_tiles + 1, 2, V_DIM, ta), F32)],
        compiler_params=_params(2),
        name="diffattn",
    )(plan, qt, k.reshape(B, S, aw), v.reshape(B, S, vw), qcols, kcols,
      lambda_q1, lambda_k1, lambda_q2, lambda_k2, subln_g.reshape(V_DIM, 1))

    out = pl.pallas_call(
        functools.partial(_merge_ffn_kernel, seq=S, tm=tm, chunk=MERGE_CHUNK, ff_chunk=FFN_CHUNK),
        grid=(M // tm,),
        in_specs=[row(D), row(pw),
                  pl.BlockSpec((HALO, pw), lambda i: (jnp.maximum(i * (tm // HALO) - 1, 0), 0)),
                  row(vw), row(gw), _const_spec((1, pw)), _const_spec((1, D))]
                 + [_resident_spec(w.shape) for w in later_bf16],
        out_specs=row(D),
        out_shape=jax.ShapeDtypeStruct((M, D), F32),
        scratch_shapes=[pltpu.VMEM((HALO + tm, pw), F32),
                        pltpu.VMEM((tm, D), F32)],
        compiler_params=_params(1),
        name="merge_ffn",
    )(x2, u, u, attn.reshape(M, vw), gates, pool_scale, norm2_g, *later_bf16)
    return out.reshape(B, S, D)
```

```python
import functools
import math

import numpy as np
import jax
import jax.numpy as jnp
from jax import lax
from jax.experimental import pallas as pl
from jax.experimental.pallas import tpu as pltpu

F32 = jnp.float32
BF16 = jnp.bfloat16
EPS = 1e-6

POOL_WINDOWS = (2, 4, 8, 16)
POOL_GROUP = 128
HALO = max(POOL_WINDOWS)
N_HEADS = 4
HEAD_DIM = 64
V_DIM = 2 * HEAD_DIM
LANES = 128
BF16_SUBLANES = 16
LAMBDA_INIT = 0.8 - 0.6 * math.exp(-0.3 * 0)
NEG_BIG = -1e30
LOG2E = math.log2(math.e)
BIAS_TERMS = 3
SCORE_BOUND_MAX = 60.0
EXP2_ZERO_BELOW = -150.0

ROW_TILE = 512
GATE_CHUNK = 512
FFN_CHUNK = 1024
MERGE_CHUNK = 256
ATTN_TILE = 512
DIAG_UNROLL = 5
FINISH_UNROLL = 4
BELOW_UNROLL = 8
VMEM_LIMIT = 56 * 1024 * 1024


def _dot(a, b):
    return jnp.dot(a, b, preferred_element_type=F32)


def _sigmoid(x):
    return 0.5 * jnp.tanh(0.5 * x) + 0.5


def _inproj_kernel(x_ref, g1_ref, w_ref, gb_ref, gq_ref, gk_ref, *rest, widths, scale, n_later):
    later_f32, rest = rest[:n_later], rest[n_later:]
    u_ref, qt_ref, k_ref, v_ref, g_ref = rest[:5]
    later_bf16, wbf_ref = rest[5:5 + n_later], rest[5 + n_later]
    pw, aw, vw, gw = widths

    for src, dst in zip(later_f32, later_bf16):
        dst[...] = src[...].astype(BF16)

    @pl.when(pl.program_id(0) == 0)
    def _():
        wbf_ref[...] = w_ref[...].astype(BF16)

    x = x_ref[...]
    ms = jnp.mean(x * x, axis=-1, keepdims=True)
    h = (x * lax.rsqrt(ms + EPS) * g1_ref[...]).astype(BF16)

    def proj(lo, width):
        return _dot(h, wbf_ref[:, lo:lo + width])

    def group_rms_t(y, gain_col):
        yt = y.T
        parts = []
        for g in range(y.shape[1] // HEAD_DIM):
            blk = yt[g * HEAD_DIM:(g + 1) * HEAD_DIM, :]
            parts.append(blk * lax.rsqrt(jnp.mean(blk * blk, axis=0, keepdims=True) + EPS))
        return jnp.concatenate(parts, axis=0) * gain_col

    qt = (group_rms_t(proj(pw, aw), gq_ref[...]) * scale).astype(BF16)
    for hd in range(N_HEADS):
        qt_ref[0, hd, 0] = qt[hd * 2 * HEAD_DIM:(hd + 1) * 2 * HEAD_DIM, :]
    k_ref[...] = group_rms_t(proj(pw + aw, aw), gk_ref[...]).astype(BF16).T
    g0 = pw + 2 * aw + vw
    chunk = GATE_CHUNK
    for c in range(gw // chunk):
        gl = proj(g0 + c * chunk, chunk) + gb_ref[:, c * chunk:(c + 1) * chunk]
        g_ref[:, c * chunk:(c + 1) * chunk] = _sigmoid(gl).astype(BF16)
    v_ref[...] = proj(pw + 2 * aw, vw).astype(BF16)
    u_ref[...] = proj(0, pw)


def _attn_kernel(plan_ref, qt_ref, k_ref, v_ref, qc_ref, kc_ref, lq1_ref, lk1_ref, lq2_ref, lk2_ref,
                 sg_ref, o_ref, p_ref, m_ref, l_ref, acc_ref, *, tile, n_tiles):
    h = pl.program_id(1)
    bounded = plan_ref[0] == 1
    n_reach = plan_ref[1 + h]
    spare = n_tiles

    lane1 = lax.broadcasted_iota(jnp.int32, (1, LANES), 1)
    dist_lane = (lane1 >= 2 * BIAS_TERMS) & (lane1 < 3 * BIAS_TERMS)

    whole = (0, tile)

    def scores(qi, j, masked, keys=whole, queries=whole):
        k0, nk = keys
        q0, nq = queries
        qt = qt_ref[0, 0, qi, :, q0:q0 + nq]
        zero = jnp.zeros_like(qt)
        dist = jnp.where(dist_lane, jnp.asarray(j - qi).astype(F32), 0.0).astype(BF16)
        ka = jnp.concatenate([k_ref[0, pl.ds(pl.multiple_of(j * tile, tile) + k0, nk), :],
                              kc_ref[0, k0:k0 + nk, :] + dist], axis=1)
        if masked:
            future = (lax.broadcasted_iota(jnp.int32, (nk, nq), 0) + k0
                      > lax.broadcasted_iota(jnp.int32, (nk, nq), 1) + q0)
        channel = lax.broadcasted_iota(jnp.int32, (2 * HEAD_DIM, nq), 0)
        out = []
        for c in range(2):
            qc = jnp.where(channel >= HEAD_DIM if c else channel < HEAD_DIM, qt, zero)
            qa = jnp.concatenate([qc, qc_ref[0, :, q0:q0 + nq]], axis=0)
            s = _dot(ka, qa)
            out.append(jnp.where(future, NEG_BIG, s) if masked else s)
        return out

    def values_t_probs(j, p, keys=whole):
        start = pl.multiple_of(j * tile, tile) + keys[0]
        return lax.dot_general(v_ref[0, pl.ds(start, keys[1]), :], p, (((0,), (0,)), ((), ())),
                               preferred_element_type=F32)

    l_ref[spare] = jnp.zeros(l_ref.shape[1:], F32)
    acc_ref[spare] = jnp.zeros(acc_ref.shape[1:], F32)

    @pl.when(bounded)
    def _():
        def probs(item, slot, masked):
            qi, j, state = item
            for c, s in enumerate(scores(qi, j, masked)):
                p = jnp.exp2(s)
                l_ref[state, c] += jnp.sum(p, axis=0, keepdims=True)
                p_ref[slot, c] = p.astype(BF16)

        def accumulate(item, slot):
            _, j, state = item
            for c in range(2):
                acc_ref[state, c] += values_t_probs(j, p_ref[slot, c])

        half = tile // 2
        first_keys, last_keys, last_queries = (0, half), (half, half), (half, half)

        def first_half_probs(i):
            for c, s in enumerate(scores(i, i, True, keys=first_keys)):
                p = jnp.exp2(s)
                l_ref[i, c] = jnp.sum(p, axis=0, keepdims=True)
                p_ref[0, c, 0:half, :] = p.astype(BF16)

        def first_half_accumulate(i):
            for c in range(2):
                acc_ref[i, c] = values_t_probs(i, p_ref[0, c, 0:half, :], keys=first_keys)

        def last_half_probs(i):
            for c, s in enumerate(scores(i, i, True, keys=last_keys, queries=last_queries)):
                p = jnp.exp2(s)
                part = jnp.sum(p, axis=0, keepdims=True)
                l_ref[i, c] += jnp.concatenate([jnp.zeros_like(part), part], axis=1)
                p_ref[1, c, 0:half, 0:half] = p.astype(BF16)

        def last_half_accumulate(i):
            for c in range(2):
                acc_ref[i, c, :, half:] += values_t_probs(i, p_ref[1, c, 0:half, 0:half], keys=last_keys)

        def below(cursor):
            d, qi = cursor
            live = d <= reach
            return jnp.where(live, qi, 0), jnp.where(live, qi - d, 0), jnp.where(live, qi, spare)

        def advance(cursor):
            d, qi = cursor
            wrap = qi + 1 >= n_tiles
            d = jnp.where(wrap, d + 1, d)
            return d, jnp.where(wrap, d, qi + 1)

        reach = jnp.minimum(n_reach, n_tiles - 1)
        n_below = reach * n_tiles - lax.shift_right_logical(reach * (reach + 1), 1)

        first_half_probs(0)

        def diagonal_tile(i):
            last_half_probs(i)
            first_half_accumulate(i)
            first_half_probs(i + 1)
            last_half_accumulate(i)

        def diagonal_tiles(t, carry):
            for u in range(DIAG_UNROLL):
                diagonal_tile(DIAG_UNROLL * t + u)
            return carry

        n_looped = (n_tiles - 1) // DIAG_UNROLL * DIAG_UNROLL
        lax.fori_loop(0, n_looped // DIAG_UNROLL, diagonal_tiles, 0)
        for i in range(n_looped, n_tiles - 1):
            diagonal_tile(i)
        first = (jnp.int32(1), jnp.int32(1))
        last_half_probs(n_tiles - 1)
        first_half_accumulate(n_tiles - 1)
        probs(below(first), 0, False)
        last_half_accumulate(n_tiles - 1)

        def below_group(t, cursor):
            for u in range(BELOW_UNROLL):
                nxt = advance(cursor)
                probs(below(nxt), (u + 1) % 2, False)
                accumulate(below(cursor), u % 2)
                cursor = nxt
            return cursor

        lax.fori_loop(0, (n_below + BELOW_UNROLL - 1) // BELOW_UNROLL, below_group, first)

    @pl.when(jnp.logical_not(bounded))
    def _():
        def block(qi, j, masked):
            for c, s in enumerate(scores(qi, j, masked)):
                m_old = m_ref[c]
                m_new = jnp.maximum(m_old, jnp.max(s, axis=0, keepdims=True))
                p = jnp.exp2(s - m_new)
                alpha = jnp.exp2(m_old - m_new)
                l_ref[qi, c] = alpha * l_ref[qi, c] + jnp.sum(p, axis=0, keepdims=True)
                acc_ref[qi, c] = alpha * acc_ref[qi, c] + values_t_probs(j, p.astype(BF16))
                m_ref[c] = m_new

        def query_tile(qi, carry):
            m_ref[...] = jnp.full(m_ref.shape, NEG_BIG, F32)
            l_ref[qi] = jnp.zeros(l_ref.shape[1:], F32)
            acc_ref[qi] = jnp.zeros(acc_ref.shape[1:], F32)
            block(qi, qi, True)

            def body(t, c):
                block(qi, qi - 1 - t, False)
                return c

            lax.fori_loop(0, qi, body, 0)
            return carry

        lax.fori_loop(0, n_tiles, query_tile, 0)

    lam = (jnp.exp(jnp.sum(lq1_ref[...] * lk1_ref[...], axis=-1, keepdims=True))
           - jnp.exp(jnp.sum(lq2_ref[...] * lk2_ref[...], axis=-1, keepdims=True))
           + LAMBDA_INIT)

    def finish(qi):
        o = acc_ref[qi, 0] * (1.0 / l_ref[qi, 0]) - acc_ref[qi, 1] * (lam / l_ref[qi, 1])
        msq = jnp.mean(o * o, axis=0, keepdims=True)
        y = o * lax.rsqrt(msq + EPS) * (sg_ref[...] * (1.0 - LAMBDA_INIT))
        o_ref[0, pl.ds(pl.multiple_of(qi * tile, tile), tile), :] = y.astype(BF16).T

    def finish_group(t, carry):
        for u in range(FINISH_UNROLL):
            finish(FINISH_UNROLL * t + u)
        return carry

    assert n_tiles % FINISH_UNROLL == 0
    lax.fori_loop(0, n_tiles // FINISH_UNROLL, finish_group, 0)


def _merge_ffn_kernel(x_ref, u_ref, halo_ref, a_ref, g_ref, ps_ref, g2_ref,
                      pw_ref, wa_ref, wb_ref, wo_ref, w1_ref, w2_ref, o_ref, ext_ref, mid_ref,
                      *, seq, tm, chunk, ff_chunk):
    i = pl.program_id(0)
    n_ff = w1_ref.shape[1] // ff_chunk
    t0 = (i * tm) % seq
    halo = halo_ref[...]
    ext_ref[0:HALO, :] = jnp.where(t0 == 0, jnp.zeros_like(halo), halo)
    ext_ref[HALO:, :] = u_ref[...]
    d = o_ref.shape[1]

    for r0 in range(0, tm, chunk):
        rows = slice(r0, r0 + chunk)
        t = t0 + r0 + lax.broadcasted_iota(jnp.int32, (chunk, 1), 0)
        mixed = []
        for g, w in enumerate(POOL_WINDOWS):
            y = ext_ref[r0:r0 + chunk + HALO, g * POOL_GROUP:(g + 1) * POOL_GROUP]
            tot = y
            span = 1
            while span < w:
                tot = tot + pltpu.roll(tot, span, axis=0)
                span *= 2
            cnt = jnp.minimum(t + 1, w).astype(F32)
            pooled = tot[HALO:] / cnt - y[HALO:]
            mixed.append(_dot(pooled.astype(BF16), pw_ref[g * POOL_GROUP:(g + 1) * POOL_GROUP, :]))
        mixed = jnp.concatenate(mixed, axis=1) * ps_ref[...]
        ya = _dot(mixed.astype(BF16), wa_ref[...])
        yb = _dot(a_ref[rows, :], wb_ref[...])
        merged = g_ref[rows, 0:d].astype(F32) * ya + g_ref[rows, d:2 * d].astype(F32) * yb
        mid_ref[rows, :] = x_ref[rows, :] + _dot(merged.astype(BF16), wo_ref[...])

    x = mid_ref[...]
    ms = jnp.mean(x * x, axis=-1, keepdims=True)
    h = (x * lax.rsqrt(ms + EPS) * g2_ref[...]).astype(BF16)
    acc = x
    for c in range(n_ff):
        cols = slice(c * ff_chunk, (c + 1) * ff_chunk)
        a = jnp.maximum(_dot(h, w1_ref[:, cols]), 0.0)
        acc = acc + _dot((a * a).astype(BF16), w2_ref[cols, :])
    o_ref[...] = acc


def _alibi_columns(tile):
    slopes2 = np.array([2.0 ** (-8.0 * (h + 1) / N_HEADS) for h in range(N_HEADS)]) * LOG2E
    pos = np.arange(tile, dtype=np.float64)

    def terms(x):
        out = []
        for _ in range(BIAS_TERMS):
            part = x.astype(BF16)
            out.append(part)
            x = x - part.astype(np.float64)
        return out

    kcols = np.zeros((N_HEADS, tile, LANES), BF16)
    qcols = np.zeros((N_HEADS, tile, LANES), BF16)
    k_terms = terms(slopes2[:, None] * pos[None, :])
    q_terms = terms(-slopes2[:, None] * pos[None, :])
    t_terms = terms(np.broadcast_to(slopes2[:, None] * tile, (N_HEADS, tile)))
    for t in range(BIAS_TERMS):
        kcols[:, :, t] = k_terms[t]
        kcols[:, :, BIAS_TERMS + t] = 1.0
        qcols[:, :, t] = 1.0
        qcols[:, :, BIAS_TERMS + t] = q_terms[t]
        qcols[:, :, 2 * BIAS_TERMS + t] = t_terms[t]
    return jnp.asarray(qcols.transpose(0, 2, 1)), jnp.asarray(kcols), slopes2.astype(np.float32)


def _const_spec(shape):
    return pl.BlockSpec(shape, lambda *_: (0,) * len(shape))


def _resident_spec(shape):
    return pl.BlockSpec(shape, lambda *_: (0,) * len(shape), pipeline_mode=pl.Buffered(1))


def _params(n_axes):
    return pltpu.CompilerParams(dimension_semantics=("arbitrary",) * n_axes,
                                vmem_limit_bytes=VMEM_LIMIT)


def kernel(x, norm1_g, w_in, gate_b, pool_w, pool_scale, q_norm_g, k_norm_g, lambda_q1, lambda_k1,
           lambda_q2, lambda_k2, subln_g, w_branch_a, w_branch_b, w_out, norm2_g, w_ff1, w_ff2):
    B, S, D = x.shape
    depth = norm1_g.shape[0]
    assert depth == 1
    M = B * S
    pw = pool_scale.shape[1]
    aw = 2 * N_HEADS * HEAD_DIM
    vw = N_HEADS * V_DIM
    gw = 2 * D
    d_ff = w_ff1.shape[2]
    tm = ROW_TILE
    ta = ATTN_TILE
    assert S % tm == 0 and S % ta == 0 and tm % HALO == 0
    assert d_ff % FFN_CHUNK == 0
    assert pw == len(POOL_WINDOWS) * POOL_GROUP and w_in.shape[2] == pw + 2 * aw + vw + gw

    x2 = x.reshape(M, D)
    row = lambda width: pl.BlockSpec((tm, width), lambda i: (i, 0))

    assert tm == ta
    n_tiles = S // ta
    reps = aw // HEAD_DIM
    steps = M // tm
    later = [pool_w[0].reshape(pw, POOL_GROUP), w_branch_a[0], w_branch_b[0], w_out[0], w_ff1[0], w_ff2[0]]
    assert all(w.shape[0] % (steps * BF16_SUBLANES) == 0 for w in later)
    row_slice = lambda w: pl.BlockSpec((w.shape[0] // steps, w.shape[1]), lambda i: (i, 0))
    u, qt, k, v, gates, *later_bf16 = pl.pallas_call(
        functools.partial(_inproj_kernel, widths=(pw, aw, vw, gw), scale=LOG2E / math.sqrt(HEAD_DIM),
                          n_later=len(later)),
        grid=(steps,),
        in_specs=[row(D), _const_spec((1, D)), _resident_spec(w_in.shape[1:]), _const_spec((1, gw)),
                  _const_spec((aw, 1)), _const_spec((aw, 1))] + [row_slice(w) for w in later],
        out_specs=[row(pw),
                   pl.BlockSpec((1, N_HEADS, 1, 2 * HEAD_DIM, ta),
                                lambda i: (i // n_tiles, 0, i % n_tiles, 0, 0)),
                   row(aw), row(vw), row(gw)] + [row_slice(w) for w in later],
        out_shape=[jax.ShapeDtypeStruct((M, pw), F32),
                   jax.ShapeDtypeStruct((B, N_HEADS, n_tiles, 2 * HEAD_DIM, ta), BF16),
                   jax.ShapeDtypeStruct((M, aw), BF16), jax.ShapeDtypeStruct((M, vw), BF16),
                   jax.ShapeDtypeStruct((M, gw), BF16)]
                  + [jax.ShapeDtypeStruct(w.shape, BF16) for w in later],
        scratch_shapes=[pltpu.VMEM(w_in.shape[1:], BF16)],
        compiler_params=_params(1),
        name="inproj",
    )(x2, norm1_g, w_in[0], gate_b, jnp.tile(q_norm_g, (1, reps)).reshape(aw, 1),
      jnp.tile(k_norm_g, (1, reps)).reshape(aw, 1), *later)

    qcols, kcols, slopes2 = _alibi_columns(ta)

    gain_product = jnp.max(jnp.abs(q_norm_g.reshape(-1, 1) * k_norm_g.reshape(1, -1)))
    bound = (1.01 * LOG2E * math.sqrt(HEAD_DIM)) * gain_product
    bounded = bound <= SCORE_BOUND_MAX
    reach = jnp.floor(((bound - EXP2_ZERO_BELOW) / jnp.asarray(slopes2, F32) - 1.0) / ta) + 1.0
    reach = jnp.clip(jnp.where(bounded, reach, S // ta), 0, S // ta)
    plan = jnp.concatenate([bounded.astype(jnp.int32)[None], reach.astype(jnp.int32)])

    head_seq = pl.BlockSpec((1, S, LANES), lambda b, h: (b, 0, h))
    attn = pl.pallas_call(
        functools.partial(_attn_kernel, tile=ta, n_tiles=n_tiles),
        grid=(B, N_HEADS),
        in_specs=[pl.BlockSpec(memory_space=pltpu.SMEM),
                  pl.BlockSpec((1, 1, n_tiles, 2 * HEAD_DIM, ta), lambda b, h: (b, h, 0, 0, 0)),
                  head_seq, head_seq,
                  pl.BlockSpec((1, LANES, ta), lambda b, h: (h, 0, 0)),
                  pl.BlockSpec((1, ta, LANES), lambda b, h: (h, 0, 0)),
                  _const_spec((1, HEAD_DIM)), _const_spec((1, HEAD_DIM)),
                  _const_spec((1, HEAD_DIM)), _const_spec((1, HEAD_DIM)), _const_spec((V_DIM, 1))],
        out_specs=head_seq,
        out_shape=jax.ShapeDtypeStruct((B, S, vw), BF16),
        scratch_shapes=[pltpu.VMEM((2, 2, ta, ta), BF16),
                        pltpu.VMEM((2, 1, ta), F32),
                        pltpu.VMEM((n_tiles + 1, 2, 1, ta), F32),
                        pltpu.VMEM((n_tiles + 1, 2, V_DIM, ta), F32)],
        compiler_params=_params(2),
        name="diffattn",
    )(plan, qt, k.reshape(B, S, aw), v.reshape(B, S, vw), qcols, kcols,
      lambda_q1, lambda_k1, lambda_q2, lambda_k2, subln_g.reshape(V_DIM, 1))

    out = pl.pallas_call(
        functools.partial(_merge_ffn_kernel, seq=S, tm=tm, chunk=MERGE_CHUNK, ff_chunk=FFN_CHUNK),
        grid=(M // tm,),
        in_specs=[row(D), row(pw),
                  pl.BlockSpec((HALO, pw), lambda i: (jnp.maximum(i * (tm // HALO) - 1, 0), 0)),
                  row(vw), row(gw), _const_spec((1, pw)), _const_spec((1, D))]
                 + [_resident_spec(w.shape) for w in later_bf16],
        out_specs=row(D),
        out_shape=jax.ShapeDtypeStruct((M, D), F32),
        scratch_shapes=[pltpu.VMEM((HALO + tm, pw), F32),
                        pltpu.VMEM((tm, D), F32)],
        compiler_params=_params(1),
        name="merge_ffn",
    )(x2, u, u, attn.reshape(M, vw), gates, pool_scale, norm2_g, *later_bf16)
    return out.reshape(B, S, D)
```

```python
import functools
import math

import numpy as np
import jax
import jax.numpy as jnp
from jax import lax
from jax.experimental import pallas as pl
from jax.experimental.pallas import tpu as pltpu

F32 = jnp.float32
BF16 = jnp.bfloat16
EPS = 1e-6

POOL_WINDOWS = (2, 4, 8, 16)
POOL_GROUP = 128
HALO = max(POOL_WINDOWS)
N_HEADS = 4
HEAD_DIM = 64
V_DIM = 2 * HEAD_DIM
LANES = 128
BF16_SUBLANES = 16
LAMBDA_INIT = 0.8 - 0.6 * math.exp(-0.3 * 0)
NEG_BIG = -1e30
LOG2E = math.log2(math.e)
BIAS_TERMS = 3
SCORE_BOUND_MAX = 60.0
EXP2_ZERO_BELOW = -150.0

ROW_TILE = 512
GATE_CHUNK = 512
FFN_CHUNK = 1024
MERGE_CHUNK = 256
ATTN_TILE = 512
DIAG_UNROLL = 5
FINISH_UNROLL = 4
BELOW_UNROLL = 8
VMEM_LIMIT = 56 * 1024 * 1024


def _dot(a, b):
    return jnp.dot(a, b, preferred_element_type=F32)


def _sigmoid(x):
    return 0.5 * jnp.tanh(0.5 * x) + 0.5


def _split_bf16(a):
    hi = a.astype(BF16)
    return hi, (a - hi.astype(F32)).astype(BF16)


def _inproj_kernel(x_ref, g1_ref, w_ref, gb_ref, gq_ref, gk_ref, pw32_ref, ps_ref, wa32_ref, *rest,
                   widths, scale, n_later):
    later_f32, rest = rest[:n_later], rest[n_later:]
    u_ref, qt_ref, k_ref, v_ref, g_ref, wpa_ref = rest[:6]
    later_bf16, wbf_ref = rest[6:6 + n_later], rest[6 + n_later]
    pw, aw, vw, gw = widths

    for src, dst in zip(later_f32, later_bf16):
        dst[...] = src[...].astype(BF16)

    @pl.when(pl.program_id(0) == 0)
    def _():
        wbf_ref[...] = w_ref[...].astype(BF16)
        for g in range(len(POOL_WINDOWS)):
            rows = slice(g * POOL_GROUP, (g + 1) * POOL_GROUP)
            p_hi, p_lo = _split_bf16(pw32_ref[rows, :] * ps_ref[:, rows])
            a_hi, a_lo = _split_bf16(wa32_ref[rows, :])
            wpa_ref[rows, :] = (_dot(p_hi, a_hi) + _dot(p_hi, a_lo) + _dot(p_lo, a_hi)).astype(BF16)

    x = x_ref[...]
    ms = jnp.mean(x * x, axis=-1, keepdims=True)
    h = (x * lax.rsqrt(ms + EPS) * g1_ref[...]).astype(BF16)

    def proj(lo, width):
        return _dot(h, wbf_ref[:, lo:lo + width])

    def group_rms_t(y, gain_col):
        yt = y.T
        parts = []
        for g in range(y.shape[1] // HEAD_DIM):
            blk = yt[g * HEAD_DIM:(g + 1) * HEAD_DIM, :]
            parts.append(blk * lax.rsqrt(jnp.mean(blk * blk, axis=0, keepdims=True) + EPS))
        return jnp.concatenate(parts, axis=0) * gain_col

    qt = (group_rms_t(proj(pw, aw), gq_ref[...]) * scale).astype(BF16)
    for hd in range(N_HEADS):
        qt_ref[0, hd, 0] = qt[hd * 2 * HEAD_DIM:(hd + 1) * 2 * HEAD_DIM, :]
    k_ref[...] = group_rms_t(proj(pw + aw, aw), gk_ref[...]).astype(BF16).T
    g0 = pw + 2 * aw + vw
    chunk = GATE_CHUNK
    for c in range(gw // chunk):
        gl = proj(g0 + c * chunk, chunk) + gb_ref[:, c * chunk:(c + 1) * chunk]
        g_ref[:, c * chunk:(c + 1) * chunk] = _sigmoid(gl).astype(BF16)
    v_ref[...] = proj(pw + 2 * aw, vw).astype(BF16)
    u_ref[...] = proj(0, pw)


def _attn_kernel(plan_ref, qt_ref, k_ref, v_ref, qc_ref, kc_ref, lq1_ref, lk1_ref, lq2_ref, lk2_ref,
                 sg_ref, o_ref, p_ref, m_ref, l_ref, acc_ref, *, tile, n_tiles):
    h = pl.program_id(1)
    bounded = plan_ref[0] == 1
    n_reach = plan_ref[1 + h]
    spare = n_tiles

    lane1 = lax.broadcasted_iota(jnp.int32, (1, LANES), 1)
    dist_lane = (lane1 >= 2 * BIAS_TERMS) & (lane1 < 3 * BIAS_TERMS)

    whole = (0, tile)

    def scores(qi, j, masked, keys=whole, queries=whole):
        k0, nk = keys
        q0, nq = queries
        qt = qt_ref[0, 0, qi, :, q0:q0 + nq]
        zero = jnp.zeros_like(qt)
        dist = jnp.where(dist_lane, jnp.asarray(j - qi).astype(F32), 0.0).astype(BF16)
        ka = jnp.concatenate([k_ref[0, pl.ds(pl.multiple_of(j * tile, tile) + k0, nk), :],
                              kc_ref[0, k0:k0 + nk, :] + dist], axis=1)
        if masked:
            future = (lax.broadcasted_iota(jnp.int32, (nk, nq), 0) + k0
                      > lax.broadcasted_iota(jnp.int32, (nk, nq), 1) + q0)
        channel = lax.broadcasted_iota(jnp.int32, (2 * HEAD_DIM, nq), 0)
        out = []
        for c in range(2):
            qc = jnp.where(channel >= HEAD_DIM if c else channel < HEAD_DIM, qt, zero)
            qa = jnp.concatenate([qc, qc_ref[0, :, q0:q0 + nq]], axis=0)
            s = _dot(ka, qa)
            out.append(jnp.where(future, NEG_BIG, s) if masked else s)
        return out

    def values_t_probs(j, p, keys=whole):
        start = pl.multiple_of(j * tile, tile) + keys[0]
        return lax.dot_general(v_ref[0, pl.ds(start, keys[1]), :], p, (((0,), (0,)), ((), ())),
                               preferred_element_type=F32)

    l_ref[spare] = jnp.zeros(l_ref.shape[1:], F32)
    acc_ref[spare] = jnp.zeros(acc_ref.shape[1:], F32)

    @pl.when(bounded)
    def _():
        def probs(item, slot, masked):
            qi, j, state = item
            for c, s in enumerate(scores(qi, j, masked)):
                p = jnp.exp2(s)
                l_ref[state, c] += jnp.sum(p, axis=0, keepdims=True)
                p_ref[slot, c] = p.astype(BF16)

        def accumulate(item, slot):
            _, j, state = item
            for c in range(2):
                acc_ref[state, c] += values_t_probs(j, p_ref[slot, c])

        half = tile // 2
        first_keys, last_keys, last_queries = (0, half), (half, half), (half, half)

        def first_half_probs(i):
            for c, s in enumerate(scores(i, i, True, keys=first_keys)):
                p = jnp.exp2(s)
                l_ref[i, c] = jnp.sum(p, axis=0, keepdims=True)
                p_ref[0, c, 0:half, :] = p.astype(BF16)

        def first_half_accumulate(i):
            for c in range(2):
                acc_ref[i, c] = values_t_probs(i, p_ref[0, c, 0:half, :], keys=first_keys)

        def last_half_probs(i):
            for c, s in enumerate(scores(i, i, True, keys=last_keys, queries=last_queries)):
                p = jnp.exp2(s)
                part = jnp.sum(p, axis=0, keepdims=True)
                l_ref[i, c] += jnp.concatenate([jnp.zeros_like(part), part], axis=1)
                p_ref[1, c, 0:half, 0:half] = p.astype(BF16)

        def last_half_accumulate(i):
            for c in range(2):
                acc_ref[i, c, :, half:] += values_t_probs(i, p_ref[1, c, 0:half, 0:half], keys=last_keys)

        def below(cursor):
            d, qi = cursor
            live = d <= reach
            return jnp.where(live, qi, 0), jnp.where(live, qi - d, 0), jnp.where(live, qi, spare)

        def advance(cursor):
            d, qi = cursor
            wrap = qi + 1 >= n_tiles
            d = jnp.where(wrap, d + 1, d)
            return d, jnp.where(wrap, d, qi + 1)

        reach = jnp.minimum(n_reach, n_tiles - 1)
        n_below = reach * n_tiles - lax.shift_right_logical(reach * (reach + 1), 1)

        first_half_probs(0)

        def diagonal_tile(i):
            last_half_probs(i)
            first_half_accumulate(i)
            first_half_probs(i + 1)
            last_half_accumulate(i)

        def diagonal_tiles(t, carry):
            for u in range(DIAG_UNROLL):
                diagonal_tile(DIAG_UNROLL * t + u)
            return carry

        n_looped = (n_tiles - 1) // DIAG_UNROLL * DIAG_UNROLL
        lax.fori_loop(0, n_looped // DIAG_UNROLL, diagonal_tiles, 0)
        for i in range(n_looped, n_tiles - 1):
            diagonal_tile(i)
        first = (jnp.int32(1), jnp.int32(1))
        last_half_probs(n_tiles - 1)
        first_half_accumulate(n_tiles - 1)
        probs(below(first), 0, False)
        last_half_accumulate(n_tiles - 1)

        def below_group(t, cursor):
            for u in range(BELOW_UNROLL):
                nxt = advance(cursor)
                probs(below(nxt), (u + 1) % 2, False)
                accumulate(below(cursor), u % 2)
                cursor = nxt
            return cursor

        lax.fori_loop(0, (n_below + BELOW_UNROLL - 1) // BELOW_UNROLL, below_group, first)

    @pl.when(jnp.logical_not(bounded))
    def _():
        def block(qi, j, masked):
            for c, s in enumerate(scores(qi, j, masked)):
                m_old = m_ref[c]
                m_new = jnp.maximum(m_old, jnp.max(s, axis=0, keepdims=True))
                p = jnp.exp2(s - m_new)
                alpha = jnp.exp2(m_old - m_new)
                l_ref[qi, c] = alpha * l_ref[qi, c] + jnp.sum(p, axis=0, keepdims=True)
                acc_ref[qi, c] = alpha * acc_ref[qi, c] + values_t_probs(j, p.astype(BF16))
                m_ref[c] = m_new

        def query_tile(qi, carry):
            m_ref[...] = jnp.full(m_ref.shape, NEG_BIG, F32)
            l_ref[qi] = jnp.zeros(l_ref.shape[1:], F32)
            acc_ref[qi] = jnp.zeros(acc_ref.shape[1:], F32)
            block(qi, qi, True)

            def body(t, c):
                block(qi, qi - 1 - t, False)
                return c

            lax.fori_loop(0, qi, body, 0)
            return carry

        lax.fori_loop(0, n_tiles, query_tile, 0)

    lam = (jnp.exp(jnp.sum(lq1_ref[...] * lk1_ref[...], axis=-1, keepdims=True))
           - jnp.exp(jnp.sum(lq2_ref[...] * lk2_ref[...], axis=-1, keepdims=True))
           + LAMBDA_INIT)

    def finish(qi):
        o = acc_ref[qi, 0] * (1.0 / l_ref[qi, 0]) - acc_ref[qi, 1] * (lam / l_ref[qi, 1])
        msq = jnp.mean(o * o, axis=0, keepdims=True)
        y = o * lax.rsqrt(msq + EPS) * (sg_ref[...] * (1.0 - LAMBDA_INIT))
        o_ref[0, pl.ds(pl.multiple_of(qi * tile, tile), tile), :] = y.astype(BF16).T

    def finish_group(t, carry):
        for u in range(FINISH_UNROLL):
            finish(FINISH_UNROLL * t + u)
        return carry

    assert n_tiles % FINISH_UNROLL == 0
    lax.fori_loop(0, n_tiles // FINISH_UNROLL, finish_group, 0)


def _merge_ffn_kernel(x_ref, u_ref, halo_ref, a_ref, g_ref, g2_ref,
                      wpa_ref, wb_ref, wo_ref, w1_ref, w2_ref, o_ref, ext_ref, mid_ref,
                      *, seq, tm, chunk, ff_chunk):
    i = pl.program_id(0)
    n_ff = w1_ref.shape[1] // ff_chunk
    t0 = (i * tm) % seq
    halo = halo_ref[...]
    ext_ref[0:HALO, :] = jnp.where(t0 == 0, jnp.zeros_like(halo), halo)
    ext_ref[HALO:, :] = u_ref[...]
    d = o_ref.shape[1]

    for r0 in range(0, tm, chunk):
        rows = slice(r0, r0 + chunk)
        t = t0 + r0 + lax.broadcasted_iota(jnp.int32, (chunk, 1), 0)
        pooled = []
        for g, w in enumerate(POOL_WINDOWS):
            y = ext_ref[r0:r0 + chunk + HALO, g * POOL_GROUP:(g + 1) * POOL_GROUP]
            tot = y
            span = 1
            while span < w:
                tot = tot + pltpu.roll(tot, span, axis=0)
                span *= 2
            cnt = jnp.minimum(t + 1, w).astype(F32)
            pooled.append((tot[HALO:] / cnt - y[HALO:]).astype(BF16))
        ya = _dot(jnp.concatenate(pooled, axis=1), wpa_ref[...])
        yb = _dot(a_ref[rows, :], wb_ref[...])
        merged = g_ref[rows, 0:d].astype(F32) * ya + g_ref[rows, d:2 * d].astype(F32) * yb
        mid_ref[rows, :] = x_ref[rows, :] + _dot(merged.astype(BF16), wo_ref[...])

    x = mid_ref[...]
    ms = jnp.mean(x * x, axis=-1, keepdims=True)
    h = (x * lax.rsqrt(ms + EPS) * g2_ref[...]).astype(BF16)
    acc = x
    for c in range(n_ff):
        cols = slice(c * ff_chunk, (c + 1) * ff_chunk)
        a = jnp.maximum(_dot(h, w1_ref[:, cols]), 0.0)
        acc = acc + _dot((a * a).astype(BF16), w2_ref[cols, :])
    o_ref[...] = acc


def _alibi_columns(tile):
    slopes2 = np.array([2.0 ** (-8.0 * (h + 1) / N_HEADS) for h in range(N_HEADS)]) * LOG2E
    pos = np.arange(tile, dtype=np.float64)

    def terms(x):
        out = []
        for _ in range(BIAS_TERMS):
            part = x.astype(BF16)
            out.append(part)
            x = x - part.astype(np.float64)
        return out

    kcols = np.zeros((N_HEADS, tile, LANES), BF16)
    qcols = np.zeros((N_HEADS, tile, LANES), BF16)
    k_terms = terms(slopes2[:, None] * pos[None, :])
    q_terms = terms(-slopes2[:, None] * pos[None, :])
    t_terms = terms(np.broadcast_to(slopes2[:, None] * tile, (N_HEADS, tile)))
    for t in range(BIAS_TERMS):
        kcols[:, :, t] = k_terms[t]
        kcols[:, :, BIAS_TERMS + t] = 1.0
        qcols[:, :, t] = 1.0
        qcols[:, :, BIAS_TERMS + t] = q_terms[t]
        qcols[:, :, 2 * BIAS_TERMS + t] = t_terms[t]
    return jnp.asarray(qcols.transpose(0, 2, 1)), jnp.asarray(kcols), slopes2.astype(np.float32)


def _const_spec(shape):
    return pl.BlockSpec(shape, lambda *_: (0,) * len(shape))


def _resident_spec(shape):
    return pl.BlockSpec(shape, lambda *_: (0,) * len(shape), pipeline_mode=pl.Buffered(1))


def _params(n_axes):
    return pltpu.CompilerParams(dimension_semantics=("arbitrary",) * n_axes,
                                vmem_limit_bytes=VMEM_LIMIT)


def kernel(x, norm1_g, w_in, gate_b, pool_w, pool_scale, q_norm_g, k_norm_g, lambda_q1, lambda_k1,
           lambda_q2, lambda_k2, subln_g, w_branch_a, w_branch_b, w_out, norm2_g, w_ff1, w_ff2):
    B, S, D = x.shape
    depth = norm1_g.shape[0]
    assert depth == 1
    M = B * S
    pw = pool_scale.shape[1]
    aw = 2 * N_HEADS * HEAD_DIM
    vw = N_HEADS * V_DIM
    gw = 2 * D
    d_ff = w_ff1.shape[2]
    tm = ROW_TILE
    ta = ATTN_TILE
    assert S % tm == 0 and S % ta == 0 and tm % HALO == 0
    assert d_ff % FFN_CHUNK == 0
    assert pw == len(POOL_WINDOWS) * POOL_GROUP and w_in.shape[2] == pw + 2 * aw + vw + gw

    x2 = x.reshape(M, D)
    row = lambda width: pl.BlockSpec((tm, width), lambda i: (i, 0))

    assert tm == ta
    n_tiles = S // ta
    reps = aw // HEAD_DIM
    steps = M // tm
    later = [w_branch_b[0], w_out[0], w_ff1[0], w_ff2[0]]
    assert all(w.shape[0] % (steps * BF16_SUBLANES) == 0 for w in later)
    row_slice = lambda w: pl.BlockSpec((w.shape[0] // steps, w.shape[1]), lambda i: (i, 0))
    u, qt, k, v, gates, w_pool_a, *later_bf16 = pl.pallas_call(
        functools.partial(_inproj_kernel, widths=(pw, aw, vw, gw), scale=LOG2E / math.sqrt(HEAD_DIM),
                          n_later=len(later)),
        grid=(steps,),
        in_specs=[row(D), _const_spec((1, D)), _resident_spec(w_in.shape[1:]), _const_spec((1, gw)),
                  _const_spec((aw, 1)), _const_spec((aw, 1)),
                  _const_spec((pw, POOL_GROUP)), _const_spec((1, pw)), _const_spec((pw, D))]
                 + [row_slice(w) for w in later],
        out_specs=[row(pw),
                   pl.BlockSpec((1, N_HEADS, 1, 2 * HEAD_DIM, ta),
                                lambda i: (i // n_tiles, 0, i % n_tiles, 0, 0)),
                   row(aw), row(vw), row(gw), _const_spec((pw, D))] + [row_slice(w) for w in later],
        out_shape=[jax.ShapeDtypeStruct((M, pw), F32),
                   jax.ShapeDtypeStruct((B, N_HEADS, n_tiles, 2 * HEAD_DIM, ta), BF16),
                   jax.ShapeDtypeStruct((M, aw), BF16), jax.ShapeDtypeStruct((M, vw), BF16),
                   jax.ShapeDtypeStruct((M, gw), BF16), jax.ShapeDtypeStruct((pw, D), BF16)]
                  + [jax.ShapeDtypeStruct(w.shape, BF16) for w in later],
        scratch_shapes=[pltpu.VMEM(w_in.shape[1:], BF16)],
        compiler_params=_params(1),
        name="inproj",
    )(x2, norm1_g, w_in[0], gate_b, jnp.tile(q_norm_g, (1, reps)).reshape(aw, 1),
      jnp.tile(k_norm_g, (1, reps)).reshape(aw, 1),
      pool_w[0].reshape(pw, POOL_GROUP), pool_scale, w_branch_a[0], *later)

    qcols, kcols, slopes2 = _alibi_columns(ta)

    gain_product = jnp.max(jnp.abs(q_norm_g.reshape(-1, 1) * k_norm_g.reshape(1, -1)))
    bound = (1.01 * LOG2E * math.sqrt(HEAD_DIM)) * gain_product
    bounded = bound <= SCORE_BOUND_MAX
    reach = jnp.floor(((bound - EXP2_ZERO_BELOW) / jnp.asarray(slopes2, F32) - 1.0) / ta) + 1.0
    reach = jnp.clip(jnp.where(bounded, reach, S // ta), 0, S // ta)
    plan = jnp.concatenate([bounded.astype(jnp.int32)[None], reach.astype(jnp.int32)])

    head_seq = pl.BlockSpec((1, S, LANES), lambda b, h: (b, 0, h))
    attn = pl.pallas_call(
        functools.partial(_attn_kernel, tile=ta, n_tiles=n_tiles),
        grid=(B, N_HEADS),
        in_specs=[pl.BlockSpec(memory_space=pltpu.SMEM),
                  pl.BlockSpec((1, 1, n_tiles, 2 * HEAD_DIM, ta), lambda b, h: (b, h, 0, 0, 0)),
                  head_seq, head_seq,
                  pl.BlockSpec((1, LANES, ta), lambda b, h: (h, 0, 0)),
                  pl.BlockSpec((1, ta, LANES), lambda b, h: (h, 0, 0)),
                  _const_spec((1, HEAD_DIM)), _const_spec((1, HEAD_DIM)),
                  _const_spec((1, HEAD_DIM)), _const_spec((1, HEAD_DIM)), _const_spec((V_DIM, 1))],
        out_specs=head_seq,
        out_shape=jax.ShapeDtypeStruct((B, S, vw), BF16),
        scratch_shapes=[pltpu.VMEM((2, 2, ta, ta), BF16),
                        pltpu.VMEM((2, 1, ta), F32),
                        pltpu.VMEM((n_tiles + 1, 2, 1, ta), F32),
                        pltpu.VMEM((n_tiles + 1, 2, V_DIM, ta), F32)],
        compiler_params=_params(2),
        name="diffattn",
    )(plan, qt, k.reshape(B, S, aw), v.reshape(B, S, vw), qcols, kcols,
      lambda_q1, lambda_k1, lambda_q2, lambda_k2, subln_g.reshape(V_DIM, 1))

    out = pl.pallas_call(
        functools.partial(_merge_ffn_kernel, seq=S, tm=tm, chunk=MERGE_CHUNK, ff_chunk=FFN_CHUNK),
        grid=(M // tm,),
        in_specs=[row(D), row(pw),
                  pl.BlockSpec((HALO, pw), lambda i: (jnp.maximum(i * (tm // HALO) - 1, 0), 0)),
                  row(vw), row(gw), _const_spec((1, D))]
                 + [_resident_spec(w.shape) for w in [w_pool_a] + later_bf16],
        out_specs=row(D),
        out_shape=jax.ShapeDtypeStruct((M, D), F32),
        scratch_shapes=[pltpu.VMEM((HALO + tm, pw), F32),
                        pltpu.VMEM((tm, D), F32)],
        compiler_params=_params(1),
        name="merge_ffn",
    )(x2, u, u, attn.reshape(M, vw), gates, norm2_g, w_pool_a, *later_bf16)
    return out.reshape(B, S, D)
```

```python
import functools
import math

import numpy as np
import jax
import jax.numpy as jnp
from jax import lax
from jax.experimental import pallas as pl
from jax.experimental.pallas import tpu as pltpu

F32 = jnp.float32
BF16 = jnp.bfloat16
EPS = 1e-6

POOL_WINDOWS = (2, 4, 8, 16)
POOL_GROUP = 128
HALO = max(POOL_WINDOWS)
N_HEADS = 4
HEAD_DIM = 64
V_DIM = 2 * HEAD_DIM
LANES = 128
BF16_SUBLANES = 16
LAMBDA_INIT = 0.8 - 0.6 * math.exp(-0.3 * 0)
NEG_BIG = -1e30
LOG2E = math.log2(math.e)
BIAS_TERMS = 3
SCORE_BOUND_MAX = 60.0
EXP2_ZERO_BELOW = -150.0

ROW_TILE = 512
GATE_CHUNK = 512
FFN_CHUNK = 1024
MERGE_CHUNK = 256
ATTN_TILE = 512
DIAG_UNROLL = 5
FINISH_UNROLL = 4
BELOW_UNROLL = 8
VMEM_LIMIT = 56 * 1024 * 1024


def _dot(a, b):
    return jnp.dot(a, b, preferred_element_type=F32)


def _sigmoid(x):
    return 0.5 * jnp.tanh(0.5 * x) + 0.5


def _split_bf16(a):
    hi = a.astype(BF16)
    return hi, (a - hi.astype(F32)).astype(BF16)


def _inproj_kernel(x_ref, g1_ref, w_ref, gb_ref, gq_ref, gk_ref, pw32_ref, ps_ref, wa32_ref, *rest,
                   widths, scale, n_later):
    later_f32, rest = rest[:n_later], rest[n_later:]
    u_ref, qt_ref, k_ref, v_ref, g_ref, wpa_ref = rest[:6]
    later_bf16, wbf_ref = rest[6:6 + n_later], rest[6 + n_later]
    pw, aw, vw, gw = widths

    for src, dst in zip(later_f32, later_bf16):
        dst[...] = src[...].astype(BF16)

    @pl.when(pl.program_id(0) == 0)
    def _():
        wbf_ref[...] = w_ref[...].astype(BF16)
        for g in range(len(POOL_WINDOWS)):
            rows = slice(g * POOL_GROUP, (g + 1) * POOL_GROUP)
            p_hi, p_lo = _split_bf16(pw32_ref[rows, :] * ps_ref[:, rows])
            a_hi, a_lo = _split_bf16(wa32_ref[rows, :])
            wpa_ref[rows, :] = (_dot(p_hi, a_hi) + _dot(p_hi, a_lo) + _dot(p_lo, a_hi)).astype(BF16)

    x = x_ref[...]
    ms = jnp.mean(x * x, axis=-1, keepdims=True)
    h = (x * lax.rsqrt(ms + EPS) * g1_ref[...]).astype(BF16)

    def proj(lo, width):
        return _dot(h, wbf_ref[:, lo:lo + width])

    def group_rms_t(y, gain_col):
        yt = y.T
        parts = []
        for g in range(y.shape[1] // HEAD_DIM):
            blk = yt[g * HEAD_DIM:(g + 1) * HEAD_DIM, :]
            parts.append(blk * lax.rsqrt(jnp.mean(blk * blk, axis=0, keepdims=True) + EPS))
        return jnp.concatenate(parts, axis=0) * gain_col

    qt = (group_rms_t(proj(pw, aw), gq_ref[...]) * scale).astype(BF16)
    for hd in range(N_HEADS):
        qt_ref[0, hd, 0] = qt[hd * 2 * HEAD_DIM:(hd + 1) * 2 * HEAD_DIM, :]
    k_ref[...] = group_rms_t(proj(pw + aw, aw), gk_ref[...]).astype(BF16).T
    g0 = pw + 2 * aw + vw
    chunk = GATE_CHUNK
    for c in range(gw // chunk):
        gl = proj(g0 + c * chunk, chunk) + gb_ref[:, c * chunk:(c + 1) * chunk]
        g_ref[:, c * chunk:(c + 1) * chunk] = _sigmoid(gl).astype(BF16)
    v_ref[...] = proj(pw + 2 * aw, vw).astype(BF16)
    u_ref[...] = proj(0, pw)


def _attn_kernel(plan_ref, qt_ref, k_ref, v_ref, qc_ref, kc_ref, lq1_ref, lk1_ref, lq2_ref, lk2_ref,
                 sg_ref, o_ref, p_ref, m_ref, l_ref, acc_ref, *, tile, n_tiles):
    h = pl.program_id(1)
    bounded = plan_ref[0] == 1
    n_reach = plan_ref[1 + h]
    spare = n_tiles

    lane1 = lax.broadcasted_iota(jnp.int32, (1, LANES), 1)
    dist_lane = (lane1 >= 2 * BIAS_TERMS) & (lane1 < 3 * BIAS_TERMS)

    whole = (0, tile)

    def scores(qi, j, masked, keys=whole, queries=whole):
        k0, nk = keys
        q0, nq = queries
        qt = qt_ref[0, 0, qi, :, q0:q0 + nq]
        zero = jnp.zeros_like(qt)
        dist = jnp.where(dist_lane, jnp.asarray(j - qi).astype(F32), 0.0).astype(BF16)
        ka = jnp.concatenate([k_ref[0, pl.ds(pl.multiple_of(j * tile, tile) + k0, nk), :],
                              kc_ref[0, k0:k0 + nk, :] + dist], axis=1)
        if masked:
            future = (lax.broadcasted_iota(jnp.int32, (nk, nq), 0) + k0
                      > lax.broadcasted_iota(jnp.int32, (nk, nq), 1) + q0)
        channel = lax.broadcasted_iota(jnp.int32, (2 * HEAD_DIM, nq), 0)
        out = []
        for c in range(2):
            qc = jnp.where(channel >= HEAD_DIM if c else channel < HEAD_DIM, qt, zero)
            qa = jnp.concatenate([qc, qc_ref[0, :, q0:q0 + nq]], axis=0)
            s = _dot(ka, qa)
            out.append(jnp.where(future, NEG_BIG, s) if masked else s)
        return out

    def values_t_probs(j, p, keys=whole):
        start = pl.multiple_of(j * tile, tile) + keys[0]
        return lax.dot_general(v_ref[0, pl.ds(start, keys[1]), :], p, (((0,), (0,)), ((), ())),
                               preferred_element_type=F32)

    l_ref[spare] = jnp.zeros(l_ref.shape[1:], F32)
    acc_ref[spare] = jnp.zeros(acc_ref.shape[1:], F32)

    @pl.when(bounded)
    def _():
        def probs(item, slot, masked):
            qi, j, state = item
            for c, s in enumerate(scores(qi, j, masked)):
                p = jnp.exp2(s)
                l_ref[state, c] += jnp.sum(p, axis=0, keepdims=True)
                p_ref[slot, c] = p.astype(BF16)

        def accumulate(item, slot):
            _, j, state = item
            for c in range(2):
                acc_ref[state, c] += values_t_probs(j, p_ref[slot, c])

        half = tile // 2
        first_keys, last_keys, last_queries = (0, half), (half, half), (half, half)

        def first_half_probs(i):
            for c, s in enumerate(scores(i, i, True, keys=first_keys)):
                p = jnp.exp2(s)
                l_ref[i, c] = jnp.sum(p, axis=0, keepdims=True)
                p_ref[0, c, 0:half, :] = p.astype(BF16)

        def first_half_accumulate(i):
            for c in range(2):
                acc_ref[i, c] = values_t_probs(i, p_ref[0, c, 0:half, :], keys=first_keys)

        def last_half_probs(i):
            for c, s in enumerate(scores(i, i, True, keys=last_keys, queries=last_queries)):
                p = jnp.exp2(s)
                part = jnp.sum(p, axis=0, keepdims=True)
                l_ref[i, c] += jnp.concatenate([jnp.zeros_like(part), part], axis=1)
                p_ref[1, c, 0:half, 0:half] = p.astype(BF16)

        def last_half_accumulate(i):
            for c in range(2):
                acc_ref[i, c, :, half:] += values_t_probs(i, p_ref[1, c, 0:half, 0:half], keys=last_keys)

        def below(cursor):
            d, qi = cursor
            live = d <= reach
            return jnp.where(live, qi, 0), jnp.where(live, qi - d, 0), jnp.where(live, qi, spare)

        def advance(cursor):
            d, qi = cursor
            wrap = qi + 1 >= n_tiles
            d = jnp.where(wrap, d + 1, d)
            return d, jnp.where(wrap, d, qi + 1)

        reach = jnp.minimum(n_reach, n_tiles - 1)
        n_below = reach * n_tiles - lax.shift_right_logical(reach * (reach + 1), 1)

        first_half_probs(0)

        def diagonal_tile(i):
            last_half_probs(i)
            first_half_accumulate(i)
            first_half_probs(i + 1)
            last_half_accumulate(i)

        def diagonal_tiles(t, carry):
            for u in range(DIAG_UNROLL):
                diagonal_tile(DIAG_UNROLL * t + u)
            return carry

        n_looped = (n_tiles - 1) // DIAG_UNROLL * DIAG_UNROLL
        lax.fori_loop(0, n_looped // DIAG_UNROLL, diagonal_tiles, 0)
        for i in range(n_looped, n_tiles - 1):
            diagonal_tile(i)
        first = (jnp.int32(1), jnp.int32(1))
        last_half_probs(n_tiles - 1)
        first_half_accumulate(n_tiles - 1)
        probs(below(first), 0, False)
        last_half_accumulate(n_tiles - 1)

        def below_group(t, cursor):
            for u in range(BELOW_UNROLL):
                nxt = advance(cursor)
                probs(below(nxt), (u + 1) % 2, False)
                accumulate(below(cursor), u % 2)
                cursor = nxt
            return cursor

        lax.fori_loop(0, (n_below + BELOW_UNROLL - 1) // BELOW_UNROLL, below_group, first)

    @pl.when(jnp.logical_not(bounded))
    def _():
        def block(qi, j, masked):
            for c, s in enumerate(scores(qi, j, masked)):
                m_old = m_ref[c]
                m_new = jnp.maximum(m_old, jnp.max(s, axis=0, keepdims=True))
                p = jnp.exp2(s - m_new)
                alpha = jnp.exp2(m_old - m_new)
                l_ref[qi, c] = alpha * l_ref[qi, c] + jnp.sum(p, axis=0, keepdims=True)
                acc_ref[qi, c] = alpha * acc_ref[qi, c] + values_t_probs(j, p.astype(BF16))
                m_ref[c] = m_new

        def query_tile(qi, carry):
            m_ref[...] = jnp.full(m_ref.shape, NEG_BIG, F32)
            l_ref[qi] = jnp.zeros(l_ref.shape[1:], F32)
            acc_ref[qi] = jnp.zeros(acc_ref.shape[1:], F32)
            block(qi, qi, True)

            def body(t, c):
                block(qi, qi - 1 - t, False)
                return c

            lax.fori_loop(0, qi, body, 0)
            return carry

        lax.fori_loop(0, n_tiles, query_tile, 0)

    lam = (jnp.exp(jnp.sum(lq1_ref[...] * lk1_ref[...], axis=-1, keepdims=True))
           - jnp.exp(jnp.sum(lq2_ref[...] * lk2_ref[...], axis=-1, keepdims=True))
           + LAMBDA_INIT)

    def finish(qi):
        o = acc_ref[qi, 0] * (1.0 / l_ref[qi, 0]) - acc_ref[qi, 1] * (lam / l_ref[qi, 1])
        msq = jnp.mean(o * o, axis=0, keepdims=True)
        y = o * lax.rsqrt(msq + EPS) * (sg_ref[...] * (1.0 - LAMBDA_INIT))
        o_ref[0, pl.ds(pl.multiple_of(qi * tile, tile), tile), :] = y.astype(BF16).T

    def finish_group(t, carry):
        for u in range(FINISH_UNROLL):
            finish(FINISH_UNROLL * t + u)
        return carry

    assert n_tiles % FINISH_UNROLL == 0
    lax.fori_loop(0, n_tiles // FINISH_UNROLL, finish_group, 0)


def _merge_ffn_kernel(x_ref, u_ref, halo_ref, a_ref, g_ref, g2_ref,
                      wpa_ref, wb_ref, wo_ref, w1_ref, w2_ref, o_ref, mid_ref,
                      *, seq, tm, chunk, ff_chunk):
    i = pl.program_id(0)
    n_ff = w1_ref.shape[1] // ff_chunk
    t0 = (i * tm) % seq
    halo = halo_ref[...]
    halo = jnp.where(t0 == 0, jnp.zeros_like(halo), halo)
    d = o_ref.shape[1]

    for r0 in range(0, tm, chunk):
        rows = slice(r0, r0 + chunk)
        t = t0 + r0 + lax.broadcasted_iota(jnp.int32, (chunk, 1), 0)
        pooled = []
        for g, w in enumerate(POOL_WINDOWS):
            lanes = slice(g * POOL_GROUP, (g + 1) * POOL_GROUP)
            history = halo[:, lanes] if r0 == 0 else u_ref[r0 - HALO:r0, lanes]
            y = jnp.concatenate([history, u_ref[rows, lanes]], axis=0)
            tot = y
            span = 1
            while span < w:
                tot = tot + pltpu.roll(tot, span, axis=0)
                span *= 2
            cnt = jnp.minimum(t + 1, w).astype(F32)
            pooled.append((tot[HALO:] / cnt - y[HALO:]).astype(BF16))
        ya = _dot(jnp.concatenate(pooled, axis=1), wpa_ref[...])
        yb = _dot(a_ref[rows, :], wb_ref[...])
        merged = g_ref[rows, 0:d].astype(F32) * ya + g_ref[rows, d:2 * d].astype(F32) * yb
        mid_ref[rows, :] = x_ref[rows, :] + _dot(merged.astype(BF16), wo_ref[...])

    x = mid_ref[...]
    ms = jnp.mean(x * x, axis=-1, keepdims=True)
    h = (x * lax.rsqrt(ms + EPS) * g2_ref[...]).astype(BF16)
    acc = x
    for c in range(n_ff):
        cols = slice(c * ff_chunk, (c + 1) * ff_chunk)
        a = jnp.maximum(_dot(h, w1_ref[:, cols]), 0.0)
        acc = acc + _dot((a * a).astype(BF16), w2_ref[cols, :])
    o_ref[...] = acc


def _alibi_columns(tile):
    slopes2 = np.array([2.0 ** (-8.0 * (h + 1) / N_HEADS) for h in range(N_HEADS)]) * LOG2E
    pos = np.arange(tile, dtype=np.float64)

    def terms(x):
        out = []
        for _ in range(BIAS_TERMS):
            part = x.astype(BF16)
            out.append(part)
            x = x - part.astype(np.float64)
        return out

    kcols = np.zeros((N_HEADS, tile, LANES), BF16)
    qcols = np.zeros((N_HEADS, tile, LANES), BF16)
    k_terms = terms(slopes2[:, None] * pos[None, :])
    q_terms = terms(-slopes2[:, None] * pos[None, :])
    t_terms = terms(np.broadcast_to(slopes2[:, None] * tile, (N_HEADS, tile)))
    for t in range(BIAS_TERMS):
        kcols[:, :, t] = k_terms[t]
        kcols[:, :, BIAS_TERMS + t] = 1.0
        qcols[:, :, t] = 1.0
        qcols[:, :, BIAS_TERMS + t] = q_terms[t]
        qcols[:, :, 2 * BIAS_TERMS + t] = t_terms[t]
    return jnp.asarray(qcols.transpose(0, 2, 1)), jnp.asarray(kcols), slopes2.astype(np.float32)


def _const_spec(shape):
    return pl.BlockSpec(shape, lambda *_: (0,) * len(shape))


def _resident_spec(shape):
    return pl.BlockSpec(shape, lambda *_: (0,) * len(shape), pipeline_mode=pl.Buffered(1))


def _params(n_axes):
    return pltpu.CompilerParams(dimension_semantics=("arbitrary",) * n_axes,
                                vmem_limit_bytes=VMEM_LIMIT)


def kernel(x, norm1_g, w_in, gate_b, pool_w, pool_scale, q_norm_g, k_norm_g, lambda_q1, lambda_k1,
           lambda_q2, lambda_k2, subln_g, w_branch_a, w_branch_b, w_out, norm2_g, w_ff1, w_ff2):
    B, S, D = x.shape
    depth = norm1_g.shape[0]
    assert depth == 1
    M = B * S
    pw = pool_scale.shape[1]
    aw = 2 * N_HEADS * HEAD_DIM
    vw = N_HEADS * V_DIM
    gw = 2 * D
    d_ff = w_ff1.shape[2]
    tm = ROW_TILE
    ta = ATTN_TILE
    assert S % tm == 0 and S % ta == 0 and tm % HALO == 0
    assert d_ff % FFN_CHUNK == 0
    assert pw == len(POOL_WINDOWS) * POOL_GROUP and w_in.shape[2] == pw + 2 * aw + vw + gw

    x2 = x.reshape(M, D)
    row = lambda width: pl.BlockSpec((tm, width), lambda i: (i, 0))

    assert tm == ta
    n_tiles = S // ta
    reps = aw // HEAD_DIM
    steps = M // tm
    later = [w_branch_b[0], w_out[0], w_ff1[0], w_ff2[0]]
    assert all(w.shape[0] % (steps * BF16_SUBLANES) == 0 for w in later)
    row_slice = lambda w: pl.BlockSpec((w.shape[0] // steps, w.shape[1]), lambda i: (i, 0))
    u, qt, k, v, gates, w_pool_a, *later_bf16 = pl.pallas_call(
        functools.partial(_inproj_kernel, widths=(pw, aw, vw, gw), scale=LOG2E / math.sqrt(HEAD_DIM),
                          n_later=len(later)),
        grid=(steps,),
        in_specs=[row(D), _const_spec((1, D)), _resident_spec(w_in.shape[1:]), _const_spec((1, gw)),
                  _const_spec((aw, 1)), _const_spec((aw, 1)),
                  _const_spec((pw, POOL_GROUP)), _const_spec((1, pw)), _const_spec((pw, D))]
                 + [row_slice(w) for w in later],
        out_specs=[row(pw),
                   pl.BlockSpec((1, N_HEADS, 1, 2 * HEAD_DIM, ta),
                                lambda i: (i // n_tiles, 0, i % n_tiles, 0, 0)),
                   row(aw), row(vw), row(gw), _const_spec((pw, D))] + [row_slice(w) for w in later],
        out_shape=[jax.ShapeDtypeStruct((M, pw), F32),
                   jax.ShapeDtypeStruct((B, N_HEADS, n_tiles, 2 * HEAD_DIM, ta), BF16),
                   jax.ShapeDtypeStruct((M, aw), BF16), jax.ShapeDtypeStruct((M, vw), BF16),
                   jax.ShapeDtypeStruct((M, gw), BF16), jax.ShapeDtypeStruct((pw, D), BF16)]
                  + [jax.ShapeDtypeStruct(w.shape, BF16) for w in later],
        scratch_shapes=[pltpu.VMEM(w_in.shape[1:], BF16)],
        compiler_params=_params(1),
        name="inproj",
    )(x2, norm1_g, w_in[0], gate_b, jnp.tile(q_norm_g.reshape(HEAD_DIM, 1), (reps, 1)),
      jnp.tile(k_norm_g.reshape(HEAD_DIM, 1), (reps, 1)),
      pool_w[0].reshape(pw, POOL_GROUP), pool_scale, w_branch_a[0], *later)

    qcols, kcols, slopes2 = _alibi_columns(ta)

    gain_product = jnp.max(jnp.abs(q_norm_g.reshape(-1, 1) * k_norm_g.reshape(1, -1)))
    bound = (1.01 * LOG2E * math.sqrt(HEAD_DIM)) * gain_product
    bounded = bound <= SCORE_BOUND_MAX
    reach = jnp.floor(((bound - EXP2_ZERO_BELOW) / jnp.asarray(slopes2, F32) - 1.0) / ta) + 1.0
    reach = jnp.clip(jnp.where(bounded, reach, S // ta), 0, S // ta)
    plan = jnp.concatenate([bounded.astype(jnp.int32)[None], reach.astype(jnp.int32)])

    head_seq = pl.BlockSpec((1, S, LANES), lambda b, h: (b, 0, h))
    attn = pl.pallas_call(
        functools.partial(_attn_kernel, tile=ta, n_tiles=n_tiles),
        grid=(B, N_HEADS),
        in_specs=[pl.BlockSpec(memory_space=pltpu.SMEM),
                  pl.BlockSpec((1, 1, n_tiles, 2 * HEAD_DIM, ta), lambda b, h: (b, h, 0, 0, 0)),
                  head_seq, head_seq,
                  pl.BlockSpec((1, LANES, ta), lambda b, h: (h, 0, 0)),
                  pl.BlockSpec((1, ta, LANES), lambda b, h: (h, 0, 0)),
                  _const_spec((1, HEAD_DIM)), _const_spec((1, HEAD_DIM)),
                  _const_spec((1, HEAD_DIM)), _const_spec((1, HEAD_DIM)), _const_spec((V_DIM, 1))],
        out_specs=head_seq,
        out_shape=jax.ShapeDtypeStruct((B, S, vw), BF16),
        scratch_shapes=[pltpu.VMEM((2, 2, ta, ta), BF16),
                        pltpu.VMEM((2, 1, ta), F32),
                        pltpu.VMEM((n_tiles + 1, 2, 1, ta), F32),
                        pltpu.VMEM((n_tiles + 1, 2, V_DIM, ta), F32)],
        compiler_params=_params(2),
        name="diffattn",
    )(plan, qt, k.reshape(B, S, aw), v.reshape(B, S, vw), qcols, kcols,
      lambda_q1, lambda_k1, lambda_q2, lambda_k2, subln_g.reshape(V_DIM, 1))

    out = pl.pallas_call(
        functools.partial(_merge_ffn_kernel, seq=S, tm=tm, chunk=MERGE_CHUNK, ff_chunk=FFN_CHUNK),
        grid=(M // tm,),
        in_specs=[row(D), row(pw),
                  pl.BlockSpec((HALO, pw), lambda i: (jnp.maximum(i * (tm // HALO) - 1, 0), 0)),
                  row(vw), row(gw), _const_spec((1, D))]
                 + [_resident_spec(w.shape) for w in [w_pool_a] + later_bf16],
        out_specs=row(D),
        out_shape=jax.ShapeDtypeStruct((M, D), F32),
        scratch_shapes=[pltpu.VMEM((tm, D), F32)],
        compiler_params=_params(1),
        name="merge_ffn",
    )(x2, u, u, attn.reshape(M, vw), gates, norm2_g, w_pool_a, *later_bf16)
    return out.reshape(B, S, D)
```

```python
import functools
import math

import numpy as np
import jax
import jax.numpy as jnp
from jax import lax
from jax.experimental import pallas as pl
from jax.experimental.pallas import tpu as pltpu

F32 = jnp.float32
BF16 = jnp.bfloat16
EPS = 1e-6

POOL_WINDOWS = (2, 4, 8, 16)
POOL_GROUP = 128
HALO = max(POOL_WINDOWS)
N_HEADS = 4
HEAD_DIM = 64
V_DIM = 2 * HEAD_DIM
LANES = 128
BF16_SUBLANES = 16
LAMBDA_INIT = 0.8 - 0.6 * math.exp(-0.3 * 0)
NEG_BIG = -1e30
LOG2E = math.log2(math.e)
BIAS_TERMS = 3
SCORE_BOUND_MAX = 60.0
EXP2_ZERO_BELOW = -150.0

ROW_TILE = 512
GATE_CHUNK = 512
FFN_CHUNK = 1024
LATER_STEPS = 16
MERGE_CHUNK = 256
ATTN_TILE = 512
DIAG_UNROLL = 5
FINISH_UNROLL = 4
BELOW_UNROLL = 8
VMEM_LIMIT = 56 * 1024 * 1024


def _dot(a, b):
    return jnp.dot(a, b, preferred_element_type=F32)


def _sigmoid(x):
    return 0.5 * jnp.tanh(0.5 * x) + 0.5


def _split_bf16(a):
    hi = a.astype(BF16)
    return hi, (a - hi.astype(F32)).astype(BF16)


def _inproj_kernel(x_ref, g1_ref, w_ref, gb_ref, gq_ref, gk_ref, pw32_ref, ps_ref, wa32_ref, *rest,
                   widths, scale, n_later):
    later_f32, rest = rest[:n_later], rest[n_later:]
    u_ref, qt_ref, k_ref, v_ref, g_ref, wpa_ref = rest[:6]
    later_bf16, wbf_ref = rest[6:6 + n_later], rest[6 + n_later]
    pw, aw, vw, gw = widths

    @pl.when(pl.program_id(0) < LATER_STEPS)
    def _():
        for src, dst in zip(later_f32, later_bf16):
            dst[...] = src[...].astype(BF16)

    @pl.when(pl.program_id(0) == 0)
    def _():
        wbf_ref[...] = w_ref[...].astype(BF16)
        for g in range(len(POOL_WINDOWS)):
            rows = slice(g * POOL_GROUP, (g + 1) * POOL_GROUP)
            p_hi, p_lo = _split_bf16(pw32_ref[rows, :] * ps_ref[:, rows])
            a_hi, a_lo = _split_bf16(wa32_ref[rows, :])
            wpa_ref[rows, :] = (_dot(p_hi, a_hi) + _dot(p_hi, a_lo) + _dot(p_lo, a_hi)).astype(BF16)

    x = x_ref[...]
    ms = jnp.mean(x * x, axis=-1, keepdims=True)
    h = (x * lax.rsqrt(ms + EPS) * g1_ref[...]).astype(BF16)

    def proj(lo, width):
        return _dot(h, wbf_ref[:, lo:lo + width])

    def group_rms_t(y, gain_col):
        yt = y.T
        parts = []
        for g in range(y.shape[1] // HEAD_DIM):
            blk = yt[g * HEAD_DIM:(g + 1) * HEAD_DIM, :]
            parts.append(blk * lax.rsqrt(jnp.mean(blk * blk, axis=0, keepdims=True) + EPS))
        return jnp.concatenate(parts, axis=0) * gain_col

    qt = (group_rms_t(proj(pw, aw), gq_ref[...]) * scale).astype(BF16)
    for hd in range(N_HEADS):
        qt_ref[0, hd, 0] = qt[hd * 2 * HEAD_DIM:(hd + 1) * 2 * HEAD_DIM, :]
    k_ref[...] = group_rms_t(proj(pw + aw, aw), gk_ref[...]).astype(BF16).T
    g0 = pw + 2 * aw + vw
    chunk = GATE_CHUNK
    for c in range(gw // chunk):
        gl = proj(g0 + c * chunk, chunk) + gb_ref[:, c * chunk:(c + 1) * chunk]
        g_ref[:, c * chunk:(c + 1) * chunk] = _sigmoid(gl).astype(BF16)
    v_ref[...] = proj(pw + 2 * aw, vw).astype(BF16)
    u_ref[...] = proj(0, pw)


def _attn_kernel(plan_ref, qt_ref, k_ref, v_ref, qc_ref, kc_ref, lq1_ref, lk1_ref, lq2_ref, lk2_ref,
                 sg_ref, o_ref, p_ref, m_ref, l_ref, acc_ref, *, tile, n_tiles):
    h = pl.program_id(1)
    bounded = plan_ref[0] == 1
    n_reach = plan_ref[1 + h]
    spare = n_tiles

    lane1 = lax.broadcasted_iota(jnp.int32, (1, LANES), 1)
    dist_lane = (lane1 >= 2 * BIAS_TERMS) & (lane1 < 3 * BIAS_TERMS)

    whole = (0, tile)

    def scores(qi, j, masked, keys=whole, queries=whole):
        k0, nk = keys
        q0, nq = queries
        qt = qt_ref[0, 0, qi, :, q0:q0 + nq]
        zero = jnp.zeros_like(qt)
        dist = jnp.where(dist_lane, jnp.asarray(j - qi).astype(F32), 0.0).astype(BF16)
        ka = jnp.concatenate([k_ref[0, pl.ds(pl.multiple_of(j * tile, tile) + k0, nk), :],
                              kc_ref[0, k0:k0 + nk, :] + dist], axis=1)
        if masked:
            future = (lax.broadcasted_iota(jnp.int32, (nk, nq), 0) + k0
                      > lax.broadcasted_iota(jnp.int32, (nk, nq), 1) + q0)
        channel = lax.broadcasted_iota(jnp.int32, (2 * HEAD_DIM, nq), 0)
        out = []
        for c in range(2):
            qc = jnp.where(channel >= HEAD_DIM if c else channel < HEAD_DIM, qt, zero)
            qa = jnp.concatenate([qc, qc_ref[0, :, q0:q0 + nq]], axis=0)
            s = _dot(ka, qa)
            out.append(jnp.where(future, NEG_BIG, s) if masked else s)
        return out

    def values_t_probs(j, p, keys=whole):
        start = pl.multiple_of(j * tile, tile) + keys[0]
        return lax.dot_general(v_ref[0, pl.ds(start, keys[1]), :], p, (((0,), (0,)), ((), ())),
                               preferred_element_type=F32)

    l_ref[spare] = jnp.zeros(l_ref.shape[1:], F32)
    acc_ref[spare] = jnp.zeros(acc_ref.shape[1:], F32)

    @pl.when(bounded)
    def _():
        def probs(item, slot, masked):
            qi, j, state = item
            for c, s in enumerate(scores(qi, j, masked)):
                p = jnp.exp2(s)
                l_ref[state, c] += jnp.sum(p, axis=0, keepdims=True)
                p_ref[slot, c] = p.astype(BF16)

        def accumulate(item, slot):
            _, j, state = item
            for c in range(2):
                acc_ref[state, c] += values_t_probs(j, p_ref[slot, c])

        half = tile // 2
        first_keys, last_keys, last_queries = (0, half), (half, half), (half, half)

        def first_half_probs(i):
            for c, s in enumerate(scores(i, i, True, keys=first_keys)):
                p = jnp.exp2(s)
                l_ref[i, c] = jnp.sum(p, axis=0, keepdims=True)
                p_ref[0, c, 0:half, :] = p.astype(BF16)

        def first_half_accumulate(i):
            for c in range(2):
                acc_ref[i, c] = values_t_probs(i, p_ref[0, c, 0:half, :], keys=first_keys)

        def last_half_probs(i):
            for c, s in enumerate(scores(i, i, True, keys=last_keys, queries=last_queries)):
                p = jnp.exp2(s)
                part = jnp.sum(p, axis=0, keepdims=True)
                l_ref[i, c] += jnp.concatenate([jnp.zeros_like(part), part], axis=1)
                p_ref[1, c, 0:half, 0:half] = p.astype(BF16)

        def last_half_accumulate(i):
            for c in range(2):
                acc_ref[i, c, :, half:] += values_t_probs(i, p_ref[1, c, 0:half, 0:half], keys=last_keys)

        def below(cursor):
            d, qi = cursor
            live = d <= reach
            return jnp.where(live, qi, 0), jnp.where(live, qi - d, 0), jnp.where(live, qi, spare)

        def advance(cursor):
            d, qi = cursor
            wrap = qi + 1 >= n_tiles
            d = jnp.where(wrap, d + 1, d)
            return d, jnp.where(wrap, d, qi + 1)

        reach = jnp.minimum(n_reach, n_tiles - 1)
        n_below = reach * n_tiles - lax.shift_right_logical(reach * (reach + 1), 1)

        first_half_probs(0)

        def diagonal_tile(i):
            last_half_probs(i)
            first_half_accumulate(i)
            first_half_probs(i + 1)
            last_half_accumulate(i)

        def diagonal_tiles(t, carry):
            for u in range(DIAG_UNROLL):
                diagonal_tile(DIAG_UNROLL * t + u)
            return carry

        n_looped = (n_tiles - 1) // DIAG_UNROLL * DIAG_UNROLL
        lax.fori_loop(0, n_looped // DIAG_UNROLL, diagonal_tiles, 0)
        for i in range(n_looped, n_tiles - 1):
            diagonal_tile(i)
        first = (jnp.int32(1), jnp.int32(1))
        last_half_probs(n_tiles - 1)
        first_half_accumulate(n_tiles - 1)
        probs(below(first), 0, False)
        last_half_accumulate(n_tiles - 1)

        def below_group(t, cursor):
            for u in range(BELOW_UNROLL):
                nxt = advance(cursor)
                probs(below(nxt), (u + 1) % 2, False)
                accumulate(below(cursor), u % 2)
                cursor = nxt
            return cursor

        lax.fori_loop(0, (n_below + BELOW_UNROLL - 1) // BELOW_UNROLL, below_group, first)

    @pl.when(jnp.logical_not(bounded))
    def _():
        def block(qi, j, masked):
            for c, s in enumerate(scores(qi, j, masked)):
                m_old = m_ref[c]
                m_new = jnp.maximum(m_old, jnp.max(s, axis=0, keepdims=True))
                p = jnp.exp2(s - m_new)
                alpha = jnp.exp2(m_old - m_new)
                l_ref[qi, c] = alpha * l_ref[qi, c] + jnp.sum(p, axis=0, keepdims=True)
                acc_ref[qi, c] = alpha * acc_ref[qi, c] + values_t_probs(j, p.astype(BF16))
                m_ref[c] = m_new

        def query_tile(qi, carry):
            m_ref[...] = jnp.full(m_ref.shape, NEG_BIG, F32)
            l_ref[qi] = jnp.zeros(l_ref.shape[1:], F32)
            acc_ref[qi] = jnp.zeros(acc_ref.shape[1:], F32)
            block(qi, qi, True)

            def body(t, c):
                block(qi, qi - 1 - t, False)
                return c

            lax.fori_loop(0, qi, body, 0)
            return carry

        lax.fori_loop(0, n_tiles, query_tile, 0)

    lam = (jnp.exp(jnp.sum(lq1_ref[...] * lk1_ref[...], axis=-1, keepdims=True))
           - jnp.exp(jnp.sum(lq2_ref[...] * lk2_ref[...], axis=-1, keepdims=True))
           + LAMBDA_INIT)

    def finish(qi):
        o = acc_ref[qi, 0] * (1.0 / l_ref[qi, 0]) - acc_ref[qi, 1] * (lam / l_ref[qi, 1])
        msq = jnp.mean(o * o, axis=0, keepdims=True)
        y = o * lax.rsqrt(msq + EPS) * (sg_ref[...] * (1.0 - LAMBDA_INIT))
        o_ref[0, pl.ds(pl.multiple_of(qi * tile, tile), tile), :] = y.astype(BF16).T

    def finish_group(t, carry):
        for u in range(FINISH_UNROLL):
            finish(FINISH_UNROLL * t + u)
        return carry

    assert n_tiles % FINISH_UNROLL == 0
    lax.fori_loop(0, n_tiles // FINISH_UNROLL, finish_group, 0)


def _merge_ffn_kernel(x_ref, u_ref, halo_ref, a_ref, g_ref, g2_ref,
                      wpa_ref, wb_ref, wo_ref, w1_ref, w2_ref, o_ref, ext_ref, mid_ref,
                      *, seq, tm, chunk, ff_chunk):
    i = pl.program_id(0)
    n_ff = w1_ref.shape[1] // ff_chunk
    t0 = (i * tm) % seq
    halo = halo_ref[...]
    ext_ref[0:HALO, :] = jnp.where(t0 == 0, jnp.zeros_like(halo), halo)
    ext_ref[HALO:, :] = u_ref[...]
    d = o_ref.shape[1]

    for r0 in range(0, tm, chunk):
        rows = slice(r0, r0 + chunk)
        t = t0 + r0 + lax.broadcasted_iota(jnp.int32, (chunk, 1), 0)
        pooled = []
        for g, w in enumerate(POOL_WINDOWS):
            y = ext_ref[r0:r0 + chunk + HALO, g * POOL_GROUP:(g + 1) * POOL_GROUP]
            tot = y
            span = 1
            while span < w:
                tot = tot + pltpu.roll(tot, span, axis=0)
                span *= 2
            cnt = jnp.minimum(t + 1, w).astype(F32)
            pooled.append((tot[HALO:] / cnt - y[HALO:]).astype(BF16))
        ya = _dot(jnp.concatenate(pooled, axis=1), wpa_ref[...])
        yb = _dot(a_ref[rows, :], wb_ref[...])
        merged = g_ref[rows, 0:d].astype(F32) * ya + g_ref[rows, d:2 * d].astype(F32) * yb
        mid_ref[rows, :] = x_ref[rows, :] + _dot(merged.astype(BF16), wo_ref[...])

    x = mid_ref[...]
    ms = jnp.mean(x * x, axis=-1, keepdims=True)
    h = (x * lax.rsqrt(ms + EPS) * g2_ref[...]).astype(BF16)
    acc = x
    for c in range(n_ff):
        cols = slice(c * ff_chunk, (c + 1) * ff_chunk)
        a = jnp.maximum(_dot(h, w1_ref[:, cols]), 0.0)
        acc = acc + _dot((a * a).astype(BF16), w2_ref[cols, :])
    o_ref[...] = acc


def _alibi_columns(tile):
    slopes2 = np.array([2.0 ** (-8.0 * (h + 1) / N_HEADS) for h in range(N_HEADS)]) * LOG2E
    pos = np.arange(tile, dtype=np.float64)

    def terms(x):
        out = []
        for _ in range(BIAS_TERMS):
            part = x.astype(BF16)
            out.append(part)
            x = x - part.astype(np.float64)
        return out

    kcols = np.zeros((N_HEADS, tile, LANES), BF16)
    qcols = np.zeros((N_HEADS, tile, LANES), BF16)
    k_terms = terms(slopes2[:, None] * pos[None, :])
    q_terms = terms(-slopes2[:, None] * pos[None, :])
    t_terms = terms(np.broadcast_to(slopes2[:, None] * tile, (N_HEADS, tile)))
    for t in range(BIAS_TERMS):
        kcols[:, :, t] = k_terms[t]
        kcols[:, :, BIAS_TERMS + t] = 1.0
        qcols[:, :, t] = 1.0
        qcols[:, :, BIAS_TERMS + t] = q_terms[t]
        qcols[:, :, 2 * BIAS_TERMS + t] = t_terms[t]
    return jnp.asarray(qcols.transpose(0, 2, 1)), jnp.asarray(kcols), slopes2.astype(np.float32)


def _const_spec(shape):
    return pl.BlockSpec(shape, lambda *_: (0,) * len(shape))


def _resident_spec(shape):
    return pl.BlockSpec(shape, lambda *_: (0,) * len(shape), pipeline_mode=pl.Buffered(1))


def _params(n_axes):
    return pltpu.CompilerParams(dimension_semantics=("arbitrary",) * n_axes,
                                vmem_limit_bytes=VMEM_LIMIT)


def kernel(x, norm1_g, w_in, gate_b, pool_w, pool_scale, q_norm_g, k_norm_g, lambda_q1, lambda_k1,
           lambda_q2, lambda_k2, subln_g, w_branch_a, w_branch_b, w_out, norm2_g, w_ff1, w_ff2):
    B, S, D = x.shape
    depth = norm1_g.shape[0]
    assert depth == 1
    M = B * S
    pw = pool_scale.shape[1]
    aw = 2 * N_HEADS * HEAD_DIM
    vw = N_HEADS * V_DIM
    gw = 2 * D
    d_ff = w_ff1.shape[2]
    tm = ROW_TILE
    ta = ATTN_TILE
    assert S % tm == 0 and S % ta == 0 and tm % HALO == 0
    assert d_ff % FFN_CHUNK == 0
    assert pw == len(POOL_WINDOWS) * POOL_GROUP and w_in.shape[2] == pw + 2 * aw + vw + gw

    x2 = x.reshape(M, D)
    row = lambda width: pl.BlockSpec((tm, width), lambda i: (i, 0))

    assert tm == ta
    n_tiles = S // ta
    reps = aw // HEAD_DIM
    steps = M // tm
    later = [w_branch_b[0], w_out[0], w_ff1[0], w_ff2[0]]
    assert steps >= LATER_STEPS and all(w.shape[0] % (LATER_STEPS * BF16_SUBLANES) == 0 for w in later)
    row_slice = lambda w: pl.BlockSpec((w.shape[0] // LATER_STEPS, w.shape[1]),
                                       lambda i: (jnp.minimum(i, LATER_STEPS - 1), 0))
    u, qt, k, v, gates, w_pool_a, *later_bf16 = pl.pallas_call(
        functools.partial(_inproj_kernel, widths=(pw, aw, vw, gw), scale=LOG2E / math.sqrt(HEAD_DIM),
                          n_later=len(later)),
        grid=(steps,),
        in_specs=[row(D), _const_spec((1, D)), _resident_spec(w_in.shape[1:]), _const_spec((1, gw)),
                  _const_spec((aw, 1)), _const_spec((aw, 1)),
                  _const_spec((pw, POOL_GROUP)), _const_spec((1, pw)), _const_spec((pw, D))]
                 + [row_slice(w) for w in later],
        out_specs=[row(pw),
                   pl.BlockSpec((1, N_HEADS, 1, 2 * HEAD_DIM, ta),
                                lambda i: (i // n_tiles, 0, i % n_tiles, 0, 0)),
                   row(aw), row(vw), row(gw), _const_spec((pw, D))] + [row_slice(w) for w in later],
        out_shape=[jax.ShapeDtypeStruct((M, pw), F32),
                   jax.ShapeDtypeStruct((B, N_HEADS, n_tiles, 2 * HEAD_DIM, ta), BF16),
                   jax.ShapeDtypeStruct((M, aw), BF16), jax.ShapeDtypeStruct((M, vw), BF16),
                   jax.ShapeDtypeStruct((M, gw), BF16), jax.ShapeDtypeStruct((pw, D), BF16)]
                  + [jax.ShapeDtypeStruct(w.shape, BF16) for w in later],
        scratch_shapes=[pltpu.VMEM(w_in.shape[1:], BF16)],
        compiler_params=_params(1),
        name="inproj",
    )(x2, norm1_g, w_in[0], gate_b, jnp.tile(q_norm_g, (1, reps)).reshape(aw, 1),
      jnp.tile(k_norm_g, (1, reps)).reshape(aw, 1),
      pool_w[0].reshape(pw, POOL_GROUP), pool_scale, w_branch_a[0], *later)

    qcols, kcols, slopes2 = _alibi_columns(ta)

    gain_product = jnp.max(jnp.abs(q_norm_g.reshape(-1, 1) * k_norm_g.reshape(1, -1)))
    bound = (1.01 * LOG2E * math.sqrt(HEAD_DIM)) * gain_product
    bounded = bound <= SCORE_BOUND_MAX
    reach = jnp.floor(((bound - EXP2_ZERO_BELOW) / jnp.asarray(slopes2, F32) - 1.0) / ta) + 1.0
    reach = jnp.clip(jnp.where(bounded, reach, S // ta), 0, S // ta)
    plan = jnp.concatenate([bounded.astype(jnp.int32)[None], reach.astype(jnp.int32)])

    head_seq = pl.BlockSpec((1, S, LANES), lambda b, h: (b, 0, h))
    attn = pl.pallas_call(
        functools.partial(_attn_kernel, tile=ta, n_tiles=n_tiles),
        grid=(B, N_HEADS),
        in_specs=[pl.BlockSpec(memory_space=pltpu.SMEM),
                  pl.BlockSpec((1, 1, n_tiles, 2 * HEAD_DIM, ta), lambda b, h: (b, h, 0, 0, 0)),
                  head_seq, head_seq,
                  pl.BlockSpec((1, LANES, ta), lambda b, h: (h, 0, 0)),
                  pl.BlockSpec((1, ta, LANES), lambda b, h: (h, 0, 0)),
                  _const_spec((1, HEAD_DIM)), _const_spec((1, HEAD_DIM)),
                  _const_spec((1, HEAD_DIM)), _const_spec((1, HEAD_DIM)), _const_spec((V_DIM, 1))],
        out_specs=head_seq,
        out_shape=jax.ShapeDtypeStruct((B, S, vw), BF16),
        scratch_shapes=[pltpu.VMEM((2, 2, ta, ta), BF16),
                        pltpu.VMEM((2, 1, ta), F32),
                        pltpu.VMEM((n_tiles + 1, 2, 1, ta), F32),
                        pltpu.VMEM((n_tiles + 1, 2, V_DIM, ta), F32)],
        compiler_params=_params(2),
        name="diffattn",
    )(plan, qt, k.reshape(B, S, aw), v.reshape(B, S, vw), qcols, kcols,
      lambda_q1, lambda_k1, lambda_q2, lambda_k2, subln_g.reshape(V_DIM, 1))

    out = pl.pallas_call(
        functools.partial(_merge_ffn_kernel, seq=S, tm=tm, chunk=MERGE_CHUNK, ff_chunk=FFN_CHUNK),
        grid=(M // tm,),
        in_specs=[row(D), row(pw),
                  pl.BlockSpec((HALO, pw), lambda i: (jnp.maximum(i * (tm // HALO) - 1, 0), 0)),
                  row(vw), row(gw), _const_spec((1, D))]
                 + [_resident_spec(w.shape) for w in [w_pool_a] + later_bf16],
        out_specs=row(D),
        out_shape=jax.ShapeDtypeStruct((M, D), F32),
        scratch_shapes=[pltpu.VMEM((HALO + tm, pw), F32),
                        pltpu.VMEM((tm, D), F32)],
        compiler_params=_params(1),
        name="merge_ffn",
    )(x2, u, u, attn.reshape(M, vw), gates, norm2_g, w_pool_a, *later_bf16)
    return out.reshape(B, S, D)
```

```python
import functools
import math

import numpy as np
import jax
import jax.numpy as jnp
from jax import lax
from jax.experimental import pallas as pl
from jax.experimental.pallas import tpu as pltpu

F32 = jnp.float32
BF16 = jnp.bfloat16
EPS = 1e-6

POOL_WINDOWS = (2, 4, 8, 16)
POOL_GROUP = 128
HALO = max(POOL_WINDOWS)
N_HEADS = 4
HEAD_DIM = 64
V_DIM = 2 * HEAD_DIM
LANES = 128
BF16_SUBLANES = 16
LAMBDA_INIT = 0.8 - 0.6 * math.exp(-0.3 * 0)
NEG_BIG = -1e30
LOG2E = math.log2(math.e)
BIAS_TERMS = 3
SCORE_BOUND_MAX = 60.0
EXP2_ZERO_BELOW = -150.0

ROW_TILE = 512
GATE_CHUNK = 512
FFN_CHUNK = 1024
LATER_STEPS = 16
MERGE_CHUNK = 256
ATTN_TILE = 512
DIAG_UNROLL = 5
FINISH_UNROLL = 4
BELOW_UNROLL = 8
VMEM_LIMIT = 56 * 1024 * 1024


def _dot(a, b):
    return jnp.dot(a, b, preferred_element_type=F32)


def _sigmoid(x):
    return 0.5 * jnp.tanh(0.5 * x) + 0.5


def _split_bf16(a):
    hi = a.astype(BF16)
    return hi, (a - hi.astype(F32)).astype(BF16)


def _inproj_kernel(x_ref, g1_ref, w_ref, gb_ref, gq_ref, gk_ref, pw32_ref, ps_ref, wa32_ref, *rest,
                   widths, scale, n_later):
    later_f32, rest = rest[:n_later], rest[n_later:]
    u_ref, qt_ref, k_ref, v_ref, g_ref, wpa_ref = rest[:6]
    later_bf16, wbf_ref = rest[6:6 + n_later], rest[6 + n_later]
    pw, aw, vw, gw = widths

    @pl.when(pl.program_id(0) < LATER_STEPS)
    def _():
        for src, dst in zip(later_f32, later_bf16):
            dst[...] = src[...].astype(BF16)

    @pl.when(pl.program_id(0) == 0)
    def _():
        wbf_ref[...] = w_ref[...].astype(BF16)
        for g in range(len(POOL_WINDOWS)):
            rows = slice(g * POOL_GROUP, (g + 1) * POOL_GROUP)
            p_hi, p_lo = _split_bf16(pw32_ref[rows, :] * ps_ref[:, rows])
            a_hi, a_lo = _split_bf16(wa32_ref[rows, :])
            wpa_ref[rows, :] = (_dot(p_hi, a_hi) + _dot(p_hi, a_lo) + _dot(p_lo, a_hi)).astype(BF16)

    x = x_ref[...]
    ms = jnp.mean(x * x, axis=-1, keepdims=True)
    h = (x * lax.rsqrt(ms + EPS) * g1_ref[...]).astype(BF16)

    def proj(lo, width):
        return _dot(h, wbf_ref[:, lo:lo + width])

    def group_rms_t(y):
        yt = y.T
        parts = []
        for g in range(y.shape[1] // HEAD_DIM):
            blk = yt[g * HEAD_DIM:(g + 1) * HEAD_DIM, :]
            parts.append(blk * lax.rsqrt(jnp.mean(blk * blk, axis=0, keepdims=True) + EPS))
        return jnp.concatenate(parts, axis=0)

    qt = (group_rms_t(proj(pw, aw)) * scale).astype(BF16)
    for hd in range(N_HEADS):
        qt_ref[0, hd, 0] = qt[hd * 2 * HEAD_DIM:(hd + 1) * 2 * HEAD_DIM, :]
    gain_row = jnp.concatenate([gq_ref[...] * gk_ref[...]] * (aw // HEAD_DIM), axis=1)
    k_ref[...] = (group_rms_t(proj(pw + aw, aw)).T * gain_row).astype(BF16)
    g0 = pw + 2 * aw + vw
    chunk = GATE_CHUNK
    for c in range(gw // chunk):
        gl = proj(g0 + c * chunk, chunk) + gb_ref[:, c * chunk:(c + 1) * chunk]
        g_ref[:, c * chunk:(c + 1) * chunk] = _sigmoid(gl).astype(BF16)
    v_ref[...] = proj(pw + 2 * aw, vw).astype(BF16)
    u_ref[...] = proj(0, pw)


def _attn_kernel(plan_ref, qt_ref, k_ref, v_ref, qc_ref, kc_ref, lq1_ref, lk1_ref, lq2_ref, lk2_ref,
                 sg_ref, o_ref, p_ref, m_ref, l_ref, acc_ref, *, tile, n_tiles):
    h = pl.program_id(1)
    bounded = plan_ref[0] == 1
    n_reach = plan_ref[1 + h]
    spare = n_tiles

    lane1 = lax.broadcasted_iota(jnp.int32, (1, LANES), 1)
    dist_lane = (lane1 >= 2 * BIAS_TERMS) & (lane1 < 3 * BIAS_TERMS)

    whole = (0, tile)
    qcols_t = qc_ref[0].T

    def scores(qi, j, masked, keys=whole, queries=whole):
        k0, nk = keys
        q0, nq = queries
        qt = qt_ref[0, 0, qi, :, q0:q0 + nq]
        zero = jnp.zeros_like(qt)
        dist = jnp.where(dist_lane, jnp.asarray(j - qi).astype(F32), 0.0).astype(BF16)
        ka = jnp.concatenate([k_ref[0, pl.ds(pl.multiple_of(j * tile, tile) + k0, nk), :],
                              kc_ref[0, k0:k0 + nk, :] + dist], axis=1)
        if masked:
            future = (lax.broadcasted_iota(jnp.int32, (nk, nq), 0) + k0
                      > lax.broadcasted_iota(jnp.int32, (nk, nq), 1) + q0)
        channel = lax.broadcasted_iota(jnp.int32, (2 * HEAD_DIM, nq), 0)
        out = []
        for c in range(2):
            qc = jnp.where(channel >= HEAD_DIM if c else channel < HEAD_DIM, qt, zero)
            qa = jnp.concatenate([qc, qcols_t[:, q0:q0 + nq]], axis=0)
            s = _dot(ka, qa)
            out.append(jnp.where(future, NEG_BIG, s) if masked else s)
        return out

    def values_t_probs(j, p, keys=whole):
        start = pl.multiple_of(j * tile, tile) + keys[0]
        return lax.dot_general(v_ref[0, pl.ds(start, keys[1]), :], p, (((0,), (0,)), ((), ())),
                               preferred_element_type=F32)

    l_ref[spare] = jnp.zeros(l_ref.shape[1:], F32)
    acc_ref[spare] = jnp.zeros(acc_ref.shape[1:], F32)

    @pl.when(bounded)
    def _():
        def probs(item, slot, masked):
            qi, j, state = item
            for c, s in enumerate(scores(qi, j, masked)):
                p = jnp.exp2(s)
                l_ref[state, c] += jnp.sum(p, axis=0, keepdims=True)
                p_ref[slot, c] = p.astype(BF16)

        def accumulate(item, slot):
            _, j, state = item
            for c in range(2):
                acc_ref[state, c] += values_t_probs(j, p_ref[slot, c])

        half = tile // 2
        first_keys, last_keys, last_queries = (0, half), (half, half), (half, half)

        def first_half_probs(i):
            for c, s in enumerate(scores(i, i, True, keys=first_keys)):
                p = jnp.exp2(s)
                l_ref[i, c] = jnp.sum(p, axis=0, keepdims=True)
                p_ref[0, c, 0:half, :] = p.astype(BF16)

        def first_half_accumulate(i):
            for c in range(2):
                acc_ref[i, c] = values_t_probs(i, p_ref[0, c, 0:half, :], keys=first_keys)

        def last_half_probs(i):
            for c, s in enumerate(scores(i, i, True, keys=last_keys, queries=last_queries)):
                p = jnp.exp2(s)
                part = jnp.sum(p, axis=0, keepdims=True)
                l_ref[i, c] += jnp.concatenate([jnp.zeros_like(part), part], axis=1)
                p_ref[1, c, 0:half, 0:half] = p.astype(BF16)

        def last_half_accumulate(i):
            for c in range(2):
                acc_ref[i, c, :, half:] += values_t_probs(i, p_ref[1, c, 0:half, 0:half], keys=last_keys)

        def below(cursor):
            d, qi = cursor
            live = d <= reach
            return jnp.where(live, qi, 0), jnp.where(live, qi - d, 0), jnp.where(live, qi, spare)

        def advance(cursor):
            d, qi = cursor
            wrap = qi + 1 >= n_tiles
            d = jnp.where(wrap, d + 1, d)
            return d, jnp.where(wrap, d, qi + 1)

        reach = jnp.minimum(n_reach, n_tiles - 1)
        n_below = reach * n_tiles - lax.shift_right_logical(reach * (reach + 1), 1)

        first_half_probs(0)

        def diagonal_tile(i):
            last_half_probs(i)
            first_half_accumulate(i)
            first_half_probs(i + 1)
            last_half_accumulate(i)

        def diagonal_tiles(t, carry):
            for u in range(DIAG_UNROLL):
                diagonal_tile(DIAG_UNROLL * t + u)
            return carry

        n_looped = (n_tiles - 1) // DIAG_UNROLL * DIAG_UNROLL
        lax.fori_loop(0, n_looped // DIAG_UNROLL, diagonal_tiles, 0)
        for i in range(n_looped, n_tiles - 1):
            diagonal_tile(i)
        first = (jnp.int32(1), jnp.int32(1))
        last_half_probs(n_tiles - 1)
        first_half_accumulate(n_tiles - 1)
        probs(below(first), 0, False)
        last_half_accumulate(n_tiles - 1)

        def below_group(t, cursor):
            for u in range(BELOW_UNROLL):
                nxt = advance(cursor)
                probs(below(nxt), (u + 1) % 2, False)
                accumulate(below(cursor), u % 2)
                cursor = nxt
            return cursor

        lax.fori_loop(0, (n_below + BELOW_UNROLL - 1) // BELOW_UNROLL, below_group, first)

    @pl.when(jnp.logical_not(bounded))
    def _():
        def block(qi, j, masked):
            for c, s in enumerate(scores(qi, j, masked)):
                m_old = m_ref[c]
                m_new = jnp.maximum(m_old, jnp.max(s, axis=0, keepdims=True))
                p = jnp.exp2(s - m_new)
                alpha = jnp.exp2(m_old - m_new)
                l_ref[qi, c] = alpha * l_ref[qi, c] + jnp.sum(p, axis=0, keepdims=True)
                acc_ref[qi, c] = alpha * acc_ref[qi, c] + values_t_probs(j, p.astype(BF16))
                m_ref[c] = m_new

        def query_tile(qi, carry):
            m_ref[...] = jnp.full(m_ref.shape, NEG_BIG, F32)
            l_ref[qi] = jnp.zeros(l_ref.shape[1:], F32)
            acc_ref[qi] = jnp.zeros(acc_ref.shape[1:], F32)
            block(qi, qi, True)

            def body(t, c):
                block(qi, qi - 1 - t, False)
                return c

            lax.fori_loop(0, qi, body, 0)
            return carry

        lax.fori_loop(0, n_tiles, query_tile, 0)

    lam = (jnp.exp(jnp.sum(lq1_ref[...] * lk1_ref[...], axis=-1, keepdims=True))
           - jnp.exp(jnp.sum(lq2_ref[...] * lk2_ref[...], axis=-1, keepdims=True))
           + LAMBDA_INIT)
    diagonal = (lax.broadcasted_iota(jnp.int32, (V_DIM, V_DIM), 0)
                == lax.broadcasted_iota(jnp.int32, (V_DIM, V_DIM), 1))
    gain_col = jnp.sum(jnp.where(diagonal, sg_ref[...], 0.0), axis=1, keepdims=True) * (1.0 - LAMBDA_INIT)

    def finish(qi):
        o = acc_ref[qi, 0] * (1.0 / l_ref[qi, 0]) - acc_ref[qi, 1] * (lam / l_ref[qi, 1])
        msq = jnp.mean(o * o, axis=0, keepdims=True)
        y = o * lax.rsqrt(msq + EPS) * gain_col
        o_ref[0, pl.ds(pl.multiple_of(qi * tile, tile), tile), :] = y.astype(BF16).T

    def finish_group(t, carry):
        for u in range(FINISH_UNROLL):
            finish(FINISH_UNROLL * t + u)
        return carry

    assert n_tiles % FINISH_UNROLL == 0
    lax.fori_loop(0, n_tiles // FINISH_UNROLL, finish_group, 0)


def _merge_ffn_kernel(x_ref, u_ref, halo_ref, a_ref, g_ref, g2_ref,
                      wpa_ref, wb_ref, wo_ref, w1_ref, w2_ref, o_ref, ext_ref, mid_ref,
                      *, seq, tm, chunk, ff_chunk):
    i = pl.program_id(0)
    n_ff = w1_ref.shape[1] // ff_chunk
    t0 = (i * tm) % seq
    halo = halo_ref[...]
    ext_ref[0:HALO, :] = jnp.where(t0 == 0, jnp.zeros_like(halo), halo)
    ext_ref[HALO:, :] = u_ref[...]
    d = o_ref.shape[1]

    for r0 in range(0, tm, chunk):
        rows = slice(r0, r0 + chunk)
        t = t0 + r0 + lax.broadcasted_iota(jnp.int32, (chunk, 1), 0)
        pooled = []
        for g, w in enumerate(POOL_WINDOWS):
            y = ext_ref[r0:r0 + chunk + HALO, g * POOL_GROUP:(g + 1) * POOL_GROUP]
            tot = y
            span = 1
            while span < w:
                tot = tot + pltpu.roll(tot, span, axis=0)
                span *= 2
            cnt = jnp.minimum(t + 1, w).astype(F32)
            pooled.append((tot[HALO:] / cnt - y[HALO:]).astype(BF16))
        ya = _dot(jnp.concatenate(pooled, axis=1), wpa_ref[...])
        yb = _dot(a_ref[rows, :], wb_ref[...])
        merged = g_ref[rows, 0:d].astype(F32) * ya + g_ref[rows, d:2 * d].astype(F32) * yb
        mid_ref[rows, :] = x_ref[rows, :] + _dot(merged.astype(BF16), wo_ref[...])

    x = mid_ref[...]
    ms = jnp.mean(x * x, axis=-1, keepdims=True)
    h = (x * lax.rsqrt(ms + EPS) * g2_ref[...]).astype(BF16)
    acc = x
    for c in range(n_ff):
        cols = slice(c * ff_chunk, (c + 1) * ff_chunk)
        a = jnp.maximum(_dot(h, w1_ref[:, cols]), 0.0)
        acc = acc + _dot((a * a).astype(BF16), w2_ref[cols, :])
    o_ref[...] = acc


def _alibi_columns(tile):
    slopes2 = np.array([2.0 ** (-8.0 * (h + 1) / N_HEADS) for h in range(N_HEADS)]) * LOG2E
    pos = np.arange(tile, dtype=np.float64)

    def terms(x):
        out = []
        for _ in range(BIAS_TERMS):
            part = x.astype(BF16)
            out.append(part)
            x = x - part.astype(np.float64)
        return out

    kcols = np.zeros((N_HEADS, tile, LANES), BF16)
    qcols = np.zeros((N_HEADS, tile, LANES), BF16)
    k_terms = terms(slopes2[:, None] * pos[None, :])
    q_terms = terms(-slopes2[:, None] * pos[None, :])
    t_terms = terms(np.broadcast_to(slopes2[:, None] * tile, (N_HEADS, tile)))
    for t in range(BIAS_TERMS):
        kcols[:, :, t] = k_terms[t]
        kcols[:, :, BIAS_TERMS + t] = 1.0
        qcols[:, :, t] = 1.0
        qcols[:, :, BIAS_TERMS + t] = q_terms[t]
        qcols[:, :, 2 * BIAS_TERMS + t] = t_terms[t]
    return jnp.asarray(qcols), jnp.asarray(kcols), slopes2.astype(np.float32)


def _const_spec(shape):
    return pl.BlockSpec(shape, lambda *_: (0,) * len(shape))


def _resident_spec(shape):
    return pl.BlockSpec(shape, lambda *_: (0,) * len(shape), pipeline_mode=pl.Buffered(1))


def _params(n_axes):
    return pltpu.CompilerParams(dimension_semantics=("arbitrary",) * n_axes,
                                vmem_limit_bytes=VMEM_LIMIT)


def kernel(x, norm1_g, w_in, gate_b, pool_w, pool_scale, q_norm_g, k_norm_g, lambda_q1, lambda_k1,
           lambda_q2, lambda_k2, subln_g, w_branch_a, w_branch_b, w_out, norm2_g, w_ff1, w_ff2):
    B, S, D = x.shape
    depth = norm1_g.shape[0]
    assert depth == 1
    M = B * S
    pw = pool_scale.shape[1]
    aw = 2 * N_HEADS * HEAD_DIM
    vw = N_HEADS * V_DIM
    gw = 2 * D
    d_ff = w_ff1.shape[2]
    tm = ROW_TILE
    ta = ATTN_TILE
    assert S % tm == 0 and S % ta == 0 and tm % HALO == 0
    assert d_ff % FFN_CHUNK == 0
    assert pw == len(POOL_WINDOWS) * POOL_GROUP and w_in.shape[2] == pw + 2 * aw + vw + gw

    x2 = x.reshape(M, D)
    row = lambda width: pl.BlockSpec((tm, width), lambda i: (i, 0))

    assert tm == ta
    n_tiles = S // ta
    steps = M // tm
    later = [w_branch_b[0], w_out[0], w_ff1[0], w_ff2[0]]
    assert steps >= LATER_STEPS and all(w.shape[0] % (LATER_STEPS * BF16_SUBLANES) == 0 for w in later)
    row_slice = lambda w: pl.BlockSpec((w.shape[0] // LATER_STEPS, w.shape[1]),
                                       lambda i: (jnp.minimum(i, LATER_STEPS - 1), 0))
    u, qt, k, v, gates, w_pool_a, *later_bf16 = pl.pallas_call(
        functools.partial(_inproj_kernel, widths=(pw, aw, vw, gw), scale=LOG2E / math.sqrt(HEAD_DIM),
                          n_later=len(later)),
        grid=(steps,),
        in_specs=[row(D), _const_spec((1, D)), _resident_spec(w_in.shape[1:]), _const_spec((1, gw)),
                  _const_spec((1, HEAD_DIM)), _const_spec((1, HEAD_DIM)),
                  _const_spec((pw, POOL_GROUP)), _const_spec((1, pw)), _const_spec((pw, D))]
                 + [row_slice(w) for w in later],
        out_specs=[row(pw),
                   pl.BlockSpec((1, N_HEADS, 1, 2 * HEAD_DIM, ta),
                                lambda i: (i // n_tiles, 0, i % n_tiles, 0, 0)),
                   row(aw), row(vw), row(gw), _const_spec((pw, D))] + [row_slice(w) for w in later],
        out_shape=[jax.ShapeDtypeStruct((M, pw), F32),
                   jax.ShapeDtypeStruct((B, N_HEADS, n_tiles, 2 * HEAD_DIM, ta), BF16),
                   jax.ShapeDtypeStruct((M, aw), BF16), jax.ShapeDtypeStruct((M, vw), BF16),
                   jax.ShapeDtypeStruct((M, gw), BF16), jax.ShapeDtypeStruct((pw, D), BF16)]
                  + [jax.ShapeDtypeStruct(w.shape, BF16) for w in later],
        scratch_shapes=[pltpu.VMEM(w_in.shape[1:], BF16)],
        compiler_params=_params(1),
        name="inproj",
    )(x2, norm1_g, w_in[0], gate_b, q_norm_g, k_norm_g,
      pool_w[0].reshape(pw, POOL_GROUP), pool_scale, w_branch_a[0], *later)

    qcols, kcols, slopes2 = _alibi_columns(ta)

    gain_product = jnp.max(jnp.abs(q_norm_g.reshape(-1, 1) * k_norm_g.reshape(1, -1)))
    bound = (1.01 * LOG2E * math.sqrt(HEAD_DIM)) * gain_product
    bounded = bound <= SCORE_BOUND_MAX
    reach = jnp.floor(((bound - EXP2_ZERO_BELOW) / jnp.asarray(slopes2, F32) - 1.0) / ta) + 1.0
    reach = jnp.clip(jnp.where(bounded, reach, S // ta), 0, S // ta)
    plan = jnp.concatenate([bounded.astype(jnp.int32)[None], reach.astype(jnp.int32)])

    head_seq = pl.BlockSpec((1, S, LANES), lambda b, h: (b, 0, h))
    attn = pl.pallas_call(
        functools.partial(_attn_kernel, tile=ta, n_tiles=n_tiles),
        grid=(B, N_HEADS),
        in_specs=[pl.BlockSpec(memory_space=pltpu.SMEM),
                  pl.BlockSpec((1, 1, n_tiles, 2 * HEAD_DIM, ta), lambda b, h: (b, h, 0, 0, 0)),
                  head_seq, head_seq,
                  pl.BlockSpec((1, ta, LANES), lambda b, h: (h, 0, 0)),
                  pl.BlockSpec((1, ta, LANES), lambda b, h: (h, 0, 0)),
                  _const_spec((1, HEAD_DIM)), _const_spec((1, HEAD_DIM)),
                  _const_spec((1, HEAD_DIM)), _const_spec((1, HEAD_DIM)), _const_spec((1, V_DIM))],
        out_specs=head_seq,
        out_shape=jax.ShapeDtypeStruct((B, S, vw), BF16),
        scratch_shapes=[pltpu.VMEM((2, 2, ta, ta), BF16),
                        pltpu.VMEM((2, 1, ta), F32),
                        pltpu.VMEM((n_tiles + 1, 2, 1, ta), F32),
                        pltpu.VMEM((n_tiles + 1, 2, V_DIM, ta), F32)],
        compiler_params=_params(2),
        name="diffattn",
    )(plan, qt, k.reshape(B, S, aw), v.reshape(B, S, vw), qcols, kcols,
      lambda_q1, lambda_k1, lambda_q2, lambda_k2, subln_g)

    out = pl.pallas_call(
        functools.partial(_merge_ffn_kernel, seq=S, tm=tm, chunk=MERGE_CHUNK, ff_chunk=FFN_CHUNK),
        grid=(M // tm,),
        in_specs=[row(D), row(pw),
                  pl.BlockSpec((HALO, pw), lambda i: (jnp.maximum(i * (tm // HALO) - 1, 0), 0)),
                  row(vw), row(gw), _const_spec((1, D))]
                 + [_resident_spec(w.shape) for w in [w_pool_a] + later_bf16],
        out_specs=row(D),
        out_shape=jax.ShapeDtypeStruct((M, D), F32),
        scratch_shapes=[pltpu.VMEM((HALO + tm, pw), F32),
                        pltpu.VMEM((tm, D), F32)],
        compiler_params=_params(1),
        name="merge_ffn",
    )(x2, u, u, attn.reshape(M, vw), gates, norm2_g, w_pool_a, *later_bf16)
    return out.reshape(B, S, D)
```

```python
import functools
import math

import numpy as np
import jax
import jax.numpy as jnp
from jax import lax
from jax.experimental import pallas as pl
from jax.experimental.pallas import tpu as pltpu

F32 = jnp.float32
BF16 = jnp.bfloat16
EPS = 1e-6

POOL_WINDOWS = (2, 4, 8, 16)
POOL_GROUP = 128
HALO = max(POOL_WINDOWS)
N_HEADS = 4
HEAD_DIM = 64
V_DIM = 2 * HEAD_DIM
LANES = 128
BF16_SUBLANES = 16
LAMBDA_INIT = 0.8 - 0.6 * math.exp(-0.3 * 0)
NEG_BIG = -1e30
LOG2E = math.log2(math.e)
BIAS_TERMS = 3
SCORE_BOUND_MAX = 60.0
EXP2_ZERO_BELOW = -150.0

ROW_TILE = 512
GATE_CHUNK = 512
FFN_CHUNK = 1024
LATER_STEPS = 16
MERGE_CHUNK = 256
ATTN_TILE = 512
DIAG_UNROLL = 5
FINISH_UNROLL = 4
BELOW_UNROLL = 8
VMEM_LIMIT = 56 * 1024 * 1024


def _dot(a, b):
    return jnp.dot(a, b, preferred_element_type=F32)


def _sigmoid(x):
    return 0.5 * jnp.tanh(0.5 * x) + 0.5


def _split_bf16(a):
    hi = a.astype(BF16)
    return hi, (a - hi.astype(F32)).astype(BF16)


def _inproj_kernel(x_ref, g1_ref, w_ref, gb_ref, gq_ref, gk_ref, pw32_ref, ps_ref, wa32_ref, s2_ref, *rest,
                   widths, scale, n_later, attn_tile, attn_tiles):
    later_f32, rest = rest[:n_later], rest[n_later:]
    u_ref, qt_ref, k_ref, v_ref, g_ref, wpa_ref, plan_ref = rest[:7]
    later_bf16, wbf_ref = rest[7:7 + n_later], rest[7 + n_later]
    pw, aw, vw, gw = widths

    @pl.when(pl.program_id(0) < LATER_STEPS)
    def _():
        for src, dst in zip(later_f32, later_bf16):
            dst[...] = src[...].astype(BF16)

    @pl.when(pl.program_id(0) == 0)
    def _():
        wbf_ref[...] = w_ref[...].astype(BF16)
        for g in range(len(POOL_WINDOWS)):
            rows = slice(g * POOL_GROUP, (g + 1) * POOL_GROUP)
            p_hi, p_lo = _split_bf16(pw32_ref[rows, :] * ps_ref[:, rows])
            a_hi, a_lo = _split_bf16(wa32_ref[rows, :])
            wpa_ref[rows, :] = (_dot(p_hi, a_hi) + _dot(p_hi, a_lo) + _dot(p_lo, a_hi)).astype(BF16)

        bound = ((1.01 * LOG2E * math.sqrt(HEAD_DIM))
                 * jnp.max(jnp.abs(gq_ref[...]), axis=1, keepdims=True)
                 * jnp.max(jnp.abs(gk_ref[...]), axis=1, keepdims=True))
        bounded = bound <= SCORE_BOUND_MAX
        reach = jnp.floor(((bound - EXP2_ZERO_BELOW) / s2_ref[...] - 1.0) / attn_tile) + 1.0
        reach = jnp.clip(jnp.where(bounded, reach, float(attn_tiles)), 0.0, float(attn_tiles))
        lane = lax.broadcasted_iota(jnp.int32, (1, LANES), 1)
        plan_ref[...] = jnp.where(lane == 0, jnp.where(bounded, 1.0, 0.0), reach).astype(jnp.int32)

    x = x_ref[...]
    ms = jnp.mean(x * x, axis=-1, keepdims=True)
    h = (x * lax.rsqrt(ms + EPS) * g1_ref[...]).astype(BF16)

    def proj(lo, width):
        return _dot(h, wbf_ref[:, lo:lo + width])

    def group_rms_t(y):
        yt = y.T
        parts = []
        for g in range(y.shape[1] // HEAD_DIM):
            blk = yt[g * HEAD_DIM:(g + 1) * HEAD_DIM, :]
            parts.append(blk * lax.rsqrt(jnp.mean(blk * blk, axis=0, keepdims=True) + EPS))
        return jnp.concatenate(parts, axis=0)

    qt = (group_rms_t(proj(pw, aw)) * scale).astype(BF16)
    for hd in range(N_HEADS):
        qt_ref[0, hd, 0] = qt[hd * 2 * HEAD_DIM:(hd + 1) * 2 * HEAD_DIM, :]
    gain_row = jnp.concatenate([gq_ref[...] * gk_ref[...]] * (aw // HEAD_DIM), axis=1)
    k_ref[...] = (group_rms_t(proj(pw + aw, aw)).T * gain_row).astype(BF16)
    g0 = pw + 2 * aw + vw
    chunk = GATE_CHUNK
    for c in range(gw // chunk):
        gl = proj(g0 + c * chunk, chunk) + gb_ref[:, c * chunk:(c + 1) * chunk]
        g_ref[:, c * chunk:(c + 1) * chunk] = _sigmoid(gl).astype(BF16)
    v_ref[...] = proj(pw + 2 * aw, vw).astype(BF16)
    u_ref[...] = proj(0, pw)


def _attn_kernel(plan_ref, qt_ref, k_ref, v_ref, qc_ref, kc_ref, lq1_ref, lk1_ref, lq2_ref, lk2_ref,
                 sg_ref, o_ref, p_ref, m_ref, l_ref, acc_ref, *, tile, n_tiles):
    h = pl.program_id(1)
    bounded = plan_ref[0, 0] == 1
    n_reach = plan_ref[0, 1 + h]
    spare = n_tiles

    lane1 = lax.broadcasted_iota(jnp.int32, (1, LANES), 1)
    dist_lane = (lane1 >= 2 * BIAS_TERMS) & (lane1 < 3 * BIAS_TERMS)

    whole = (0, tile)
    qcols_t = qc_ref[0].T

    def scores(qi, j, masked, keys=whole, queries=whole):
        k0, nk = keys
        q0, nq = queries
        qt = qt_ref[0, 0, qi, :, q0:q0 + nq]
        zero = jnp.zeros_like(qt)
        dist = jnp.where(dist_lane, jnp.asarray(j - qi).astype(F32), 0.0).astype(BF16)
        ka = jnp.concatenate([k_ref[0, pl.ds(pl.multiple_of(j * tile, tile) + k0, nk), :],
                              kc_ref[0, k0:k0 + nk, :] + dist], axis=1)
        if masked:
            future = (lax.broadcasted_iota(jnp.int32, (nk, nq), 0) + k0
                      > lax.broadcasted_iota(jnp.int32, (nk, nq), 1) + q0)
        channel = lax.broadcasted_iota(jnp.int32, (2 * HEAD_DIM, nq), 0)
        out = []
        for c in range(2):
            qc = jnp.where(channel >= HEAD_DIM if c else channel < HEAD_DIM, qt, zero)
            qa = jnp.concatenate([qc, qcols_t[:, q0:q0 + nq]], axis=0)
            s = _dot(ka, qa)
            out.append(jnp.where(future, NEG_BIG, s) if masked else s)
        return out

    def values_t_probs(j, p, keys=whole):
        start = pl.multiple_of(j * tile, tile) + keys[0]
        return lax.dot_general(v_ref[0, pl.ds(start, keys[1]), :], p, (((0,), (0,)), ((), ())),
                               preferred_element_type=F32)

    l_ref[spare] = jnp.zeros(l_ref.shape[1:], F32)
    acc_ref[spare] = jnp.zeros(acc_ref.shape[1:], F32)

    @pl.when(bounded)
    def _():
        def probs(item, slot, masked):
            qi, j, state = item
            for c, s in enumerate(scores(qi, j, masked)):
                p = jnp.exp2(s)
                l_ref[state, c] += jnp.sum(p, axis=0, keepdims=True)
                p_ref[slot, c] = p.astype(BF16)

        def accumulate(item, slot):
            _, j, state = item
            for c in range(2):
                acc_ref[state, c] += values_t_probs(j, p_ref[slot, c])

        half = tile // 2
        first_keys, last_keys, last_queries = (0, half), (half, half), (half, half)

        def first_half_probs(i):
            for c, s in enumerate(scores(i, i, True, keys=first_keys)):
                p = jnp.exp2(s)
                l_ref[i, c] = jnp.sum(p, axis=0, keepdims=True)
                p_ref[0, c, 0:half, :] = p.astype(BF16)

        def first_half_accumulate(i):
            for c in range(2):
                acc_ref[i, c] = values_t_probs(i, p_ref[0, c, 0:half, :], keys=first_keys)

        def last_half_probs(i):
            for c, s in enumerate(scores(i, i, True, keys=last_keys, queries=last_queries)):
                p = jnp.exp2(s)
                part = jnp.sum(p, axis=0, keepdims=True)
                l_ref[i, c] += jnp.concatenate([jnp.zeros_like(part), part], axis=1)
                p_ref[1, c, 0:half, 0:half] = p.astype(BF16)

        def last_half_accumulate(i):
            for c in range(2):
                acc_ref[i, c, :, half:] += values_t_probs(i, p_ref[1, c, 0:half, 0:half], keys=last_keys)

        def below(cursor):
            d, qi = cursor
            live = d <= reach
            return jnp.where(live, qi, 0), jnp.where(live, qi - d, 0), jnp.where(live, qi, spare)

        def advance(cursor):
            d, qi = cursor
            wrap = qi + 1 >= n_tiles
            d = jnp.where(wrap, d + 1, d)
            return d, jnp.where(wrap, d, qi + 1)

        reach = jnp.minimum(n_reach, n_tiles - 1)
        n_below = reach * n_tiles - lax.shift_right_logical(reach * (reach + 1), 1)

        first_half_probs(0)

        def diagonal_tile(i):
            last_half_probs(i)
            first_half_accumulate(i)
            first_half_probs(i + 1)
            last_half_accumulate(i)

        def diagonal_tiles(t, carry):
            for u in range(DIAG_UNROLL):
                diagonal_tile(DIAG_UNROLL * t + u)
            return carry

        n_looped = (n_tiles - 1) // DIAG_UNROLL * DIAG_UNROLL
        lax.fori_loop(0, n_looped // DIAG_UNROLL, diagonal_tiles, 0)
        for i in range(n_looped, n_tiles - 1):
            diagonal_tile(i)
        first = (jnp.int32(1), jnp.int32(1))
        last_half_probs(n_tiles - 1)
        first_half_accumulate(n_tiles - 1)
        probs(below(first), 0, False)
        last_half_accumulate(n_tiles - 1)

        def below_group(t, cursor):
            for u in range(BELOW_UNROLL):
                nxt = advance(cursor)
                probs(below(nxt), (u + 1) % 2, False)
                accumulate(below(cursor), u % 2)
                cursor = nxt
            return cursor

        lax.fori_loop(0, (n_below + BELOW_UNROLL - 1) // BELOW_UNROLL, below_group, first)

    @pl.when(jnp.logical_not(bounded))
    def _():
        def block(qi, j, masked):
            for c, s in enumerate(scores(qi, j, masked)):
                m_old = m_ref[c]
                m_new = jnp.maximum(m_old, jnp.max(s, axis=0, keepdims=True))
                p = jnp.exp2(s - m_new)
                alpha = jnp.exp2(m_old - m_new)
                l_ref[qi, c] = alpha * l_ref[qi, c] + jnp.sum(p, axis=0, keepdims=True)
                acc_ref[qi, c] = alpha * acc_ref[qi, c] + values_t_probs(j, p.astype(BF16))
                m_ref[c] = m_new

        def query_tile(qi, carry):
            m_ref[...] = jnp.full(m_ref.shape, NEG_BIG, F32)
            l_ref[qi] = jnp.zeros(l_ref.shape[1:], F32)
            acc_ref[qi] = jnp.zeros(acc_ref.shape[1:], F32)
            block(qi, qi, True)

            def body(t, c):
                block(qi, qi - 1 - t, False)
                return c

            lax.fori_loop(0, qi, body, 0)
            return carry

        lax.fori_loop(0, n_tiles, query_tile, 0)

    lam = (jnp.exp(jnp.sum(lq1_ref[...] * lk1_ref[...], axis=-1, keepdims=True))
           - jnp.exp(jnp.sum(lq2_ref[...] * lk2_ref[...], axis=-1, keepdims=True))
           + LAMBDA_INIT)
    diagonal = (lax.broadcasted_iota(jnp.int32, (V_DIM, V_DIM), 0)
                == lax.broadcasted_iota(jnp.int32, (V_DIM, V_DIM), 1))
    gain_col = jnp.sum(jnp.where(diagonal, sg_ref[...], 0.0), axis=1, keepdims=True) * (1.0 - LAMBDA_INIT)

    def finish(qi):
        o = acc_ref[qi, 0] * (1.0 / l_ref[qi, 0]) - acc_ref[qi, 1] * (lam / l_ref[qi, 1])
        msq = jnp.mean(o * o, axis=0, keepdims=True)
        y = o * lax.rsqrt(msq + EPS) * gain_col
        o_ref[0, pl.ds(pl.multiple_of(qi * tile, tile), tile), :] = y.astype(BF16).T

    def finish_group(t, carry):
        for u in range(FINISH_UNROLL):
            finish(FINISH_UNROLL * t + u)
        return carry

    assert n_tiles % FINISH_UNROLL == 0
    lax.fori_loop(0, n_tiles // FINISH_UNROLL, finish_group, 0)


def _merge_ffn_kernel(x_ref, u_ref, halo_ref, a_ref, g_ref, g2_ref,
                      wpa_ref, wb_ref, wo_ref, w1_ref, w2_ref, o_ref, ext_ref, mid_ref,
                      *, seq, tm, chunk, ff_chunk):
    i = pl.program_id(0)
    n_ff = w1_ref.shape[1] // ff_chunk
    t0 = (i * tm) % seq
    halo = halo_ref[...]
    ext_ref[0:HALO, :] = jnp.where(t0 == 0, jnp.zeros_like(halo), halo)
    ext_ref[HALO:, :] = u_ref[...]
    d = o_ref.shape[1]

    for r0 in range(0, tm, chunk):
        rows = slice(r0, r0 + chunk)
        t = t0 + r0 + lax.broadcasted_iota(jnp.int32, (chunk, 1), 0)
        pooled = []
        for g, w in enumerate(POOL_WINDOWS):
            y = ext_ref[r0:r0 + chunk + HALO, g * POOL_GROUP:(g + 1) * POOL_GROUP]
            tot = y
            span = 1
            while span < w:
                tot = tot + pltpu.roll(tot, span, axis=0)
                span *= 2
            cnt = jnp.minimum(t + 1, w).astype(F32)
            pooled.append((tot[HALO:] / cnt - y[HALO:]).astype(BF16))
        ya = _dot(jnp.concatenate(pooled, axis=1), wpa_ref[...])
        yb = _dot(a_ref[rows, :], wb_ref[...])
        merged = g_ref[rows, 0:d].astype(F32) * ya + g_ref[rows, d:2 * d].astype(F32) * yb
        mid_ref[rows, :] = x_ref[rows, :] + _dot(merged.astype(BF16), wo_ref[...])

    x = mid_ref[...]
    ms = jnp.mean(x * x, axis=-1, keepdims=True)
    h = (x * lax.rsqrt(ms + EPS) * g2_ref[...]).astype(BF16)
    acc = x
    for c in range(n_ff):
        cols = slice(c * ff_chunk, (c + 1) * ff_chunk)
        a = jnp.maximum(_dot(h, w1_ref[:, cols]), 0.0)
        acc = acc + _dot((a * a).astype(BF16), w2_ref[cols, :])
    o_ref[...] = acc


def _alibi_columns(tile):
    slopes2 = np.array([2.0 ** (-8.0 * (h + 1) / N_HEADS) for h in range(N_HEADS)]) * LOG2E
    pos = np.arange(tile, dtype=np.float64)

    def terms(x):
        out = []
        for _ in range(BIAS_TERMS):
            part = x.astype(BF16)
            out.append(part)
            x = x - part.astype(np.float64)
        return out

    kcols = np.zeros((N_HEADS, tile, LANES), BF16)
    qcols = np.zeros((N_HEADS, tile, LANES), BF16)
    k_terms = terms(slopes2[:, None] * pos[None, :])
    q_terms = terms(-slopes2[:, None] * pos[None, :])
    t_terms = terms(np.broadcast_to(slopes2[:, None] * tile, (N_HEADS, tile)))
    for t in range(BIAS_TERMS):
        kcols[:, :, t] = k_terms[t]
        kcols[:, :, BIAS_TERMS + t] = 1.0
        qcols[:, :, t] = 1.0
        qcols[:, :, BIAS_TERMS + t] = q_terms[t]
        qcols[:, :, 2 * BIAS_TERMS + t] = t_terms[t]
    return jnp.asarray(qcols), jnp.asarray(kcols), slopes2.astype(np.float32)


def _const_spec(shape):
    return pl.BlockSpec(shape, lambda *_: (0,) * len(shape))


def _resident_spec(shape):
    return pl.BlockSpec(shape, lambda *_: (0,) * len(shape), pipeline_mode=pl.Buffered(1))


def _params(n_axes):
    return pltpu.CompilerParams(dimension_semantics=("arbitrary",) * n_axes,
                                vmem_limit_bytes=VMEM_LIMIT)


def kernel(x, norm1_g, w_in, gate_b, pool_w, pool_scale, q_norm_g, k_norm_g, lambda_q1, lambda_k1,
           lambda_q2, lambda_k2, subln_g, w_branch_a, w_branch_b, w_out, norm2_g, w_ff1, w_ff2):
    B, S, D = x.shape
    depth = norm1_g.shape[0]
    assert depth == 1
    M = B * S
    pw = pool_scale.shape[1]
    aw = 2 * N_HEADS * HEAD_DIM
    vw = N_HEADS * V_DIM
    gw = 2 * D
    d_ff = w_ff1.shape[2]
    tm = ROW_TILE
    ta = ATTN_TILE
    assert S % tm == 0 and S % ta == 0 and tm % HALO == 0
    assert d_ff % FFN_CHUNK == 0
    assert pw == len(POOL_WINDOWS) * POOL_GROUP and w_in.shape[2] == pw + 2 * aw + vw + gw

    x2 = x.reshape(M, D)
    row = lambda width: pl.BlockSpec((tm, width), lambda i: (i, 0))

    assert tm == ta
    n_tiles = S // ta
    steps = M // tm
    later = [w_branch_b[0], w_out[0], w_ff1[0], w_ff2[0]]
    assert steps >= LATER_STEPS and all(w.shape[0] % (LATER_STEPS * BF16_SUBLANES) == 0 for w in later)
    row_slice = lambda w: pl.BlockSpec((w.shape[0] // LATER_STEPS, w.shape[1]),
                                       lambda i: (jnp.minimum(i, LATER_STEPS - 1), 0))
    qcols, kcols, slopes2 = _alibi_columns(ta)
    s2_row = np.ones((1, LANES), np.float32)
    s2_row[0, 1:1 + N_HEADS] = slopes2
    u, qt, k, v, gates, w_pool_a, plan, *later_bf16 = pl.pallas_call(
        functools.partial(_inproj_kernel, widths=(pw, aw, vw, gw), scale=LOG2E / math.sqrt(HEAD_DIM),
                          n_later=len(later), attn_tile=ta, attn_tiles=n_tiles),
        grid=(steps,),
        in_specs=[row(D), _const_spec((1, D)), _resident_spec(w_in.shape[1:]), _const_spec((1, gw)),
                  _const_spec((1, HEAD_DIM)), _const_spec((1, HEAD_DIM)),
                  _const_spec((pw, POOL_GROUP)), _const_spec((1, pw)), _const_spec((pw, D)),
                  _const_spec((1, LANES))]
                 + [row_slice(w) for w in later],
        out_specs=[row(pw),
                   pl.BlockSpec((1, N_HEADS, 1, 2 * HEAD_DIM, ta),
                                lambda i: (i // n_tiles, 0, i % n_tiles, 0, 0)),
                   row(aw), row(vw), row(gw), _const_spec((pw, D)), _const_spec((1, LANES))]
                  + [row_slice(w) for w in later],
        out_shape=[jax.ShapeDtypeStruct((M, pw), F32),
                   jax.ShapeDtypeStruct((B, N_HEADS, n_tiles, 2 * HEAD_DIM, ta), BF16),
                   jax.ShapeDtypeStruct((M, aw), BF16), jax.ShapeDtypeStruct((M, vw), BF16),
                   jax.ShapeDtypeStruct((M, gw), BF16), jax.ShapeDtypeStruct((pw, D), BF16),
                   jax.ShapeDtypeStruct((1, LANES), jnp.int32)]
                  + [jax.ShapeDtypeStruct(w.shape, BF16) for w in later],
        scratch_shapes=[pltpu.VMEM(w_in.shape[1:], BF16)],
        compiler_params=_params(1),
        name="inproj",
    )(x2, norm1_g, w_in[0], gate_b, q_norm_g, k_norm_g,
      pool_w[0].reshape(pw, POOL_GROUP), pool_scale, w_branch_a[0], jnp.asarray(s2_row), *later)

    head_seq = pl.BlockSpec((1, S, LANES), lambda b, h: (b, 0, h))
    attn = pl.pallas_call(
        functools.partial(_attn_kernel, tile=ta, n_tiles=n_tiles),
        grid=(B, N_HEADS),
        in_specs=[pl.BlockSpec(memory_space=pltpu.SMEM),
                  pl.BlockSpec((1, 1, n_tiles, 2 * HEAD_DIM, ta), lambda b, h: (b, h, 0, 0, 0)),
                  head_seq, head_seq,
                  pl.BlockSpec((1, ta, LANES), lambda b, h: (h, 0, 0)),
                  pl.BlockSpec((1, ta, LANES), lambda b, h: (h, 0, 0)),
                  _const_spec((1, HEAD_DIM)), _const_spec((1, HEAD_DIM)),
                  _const_spec((1, HEAD_DIM)), _const_spec((1, HEAD_DIM)), _const_spec((1, V_DIM))],
        out_specs=head_seq,
        out_shape=jax.ShapeDtypeStruct((B, S, vw), BF16),
        scratch_shapes=[pltpu.VMEM((2, 2, ta, ta), BF16),
                        pltpu.VMEM((2, 1, ta), F32),
                        pltpu.VMEM((n_tiles + 1, 2, 1, ta), F32),
                        pltpu.VMEM((n_tiles + 1, 2, V_DIM, ta), F32)],
        compiler_params=_params(2),
        name="diffattn",
    )(plan, qt, k.reshape(B, S, aw), v.reshape(B, S, vw), qcols, kcols,
      lambda_q1, lambda_k1, lambda_q2, lambda_k2, subln_g)

    out = pl.pallas_call(
        functools.partial(_merge_ffn_kernel, seq=S, tm=tm, chunk=MERGE_CHUNK, ff_chunk=FFN_CHUNK),
        grid=(M // tm,),
        in_specs=[row(D), row(pw),
                  pl.BlockSpec((HALO, pw), lambda i: (jnp.maximum(i * (tm // HALO) - 1, 0), 0)),
                  row(vw), row(gw), _const_spec((1, D))]
                 + [_resident_spec(w.shape) for w in [w_pool_a] + later_bf16],
        out_specs=row(D),
        out_shape=jax.ShapeDtypeStruct((M, D), F32),
        scratch_shapes=[pltpu.VMEM((HALO + tm, pw), F32),
                        pltpu.VMEM((tm, D), F32)],
        compiler_params=_params(1),
        name="merge_ffn",
    )(x2, u, u, attn.reshape(M, vw), gates, norm2_g, w_pool_a, *later_bf16)
    return out.reshape(B, S, D)
```

```python
import functools
import math

import numpy as np
import jax
import jax.numpy as jnp
from jax import lax
from jax.experimental import pallas as pl
from jax.experimental.pallas import tpu as pltpu

F32 = jnp.float32
BF16 = jnp.bfloat16
EPS = 1e-6

POOL_WINDOWS = (2, 4, 8, 16)
POOL_GROUP = 128
HALO = max(POOL_WINDOWS)
N_HEADS = 4
HEAD_DIM = 64
V_DIM = 2 * HEAD_DIM
LANES = 128
BF16_SUBLANES = 16
LAMBDA_INIT = 0.8 - 0.6 * math.exp(-0.3 * 0)
NEG_BIG = -1e30
LOG2E = math.log2(math.e)
BIAS_TERMS = 3
SCORE_BOUND_MAX = 60.0
EXP2_ZERO_BELOW = -150.0

ROW_TILE = 512
GATE_CHUNK = 512
FFN_CHUNK = 1024
LATER_STEPS = 16
MERGE_CHUNK = 512
ATTN_TILE = 512
DIAG_UNROLL = 5
FINISH_UNROLL = 4
BELOW_UNROLL = 8
VMEM_LIMIT = 56 * 1024 * 1024


def _dot(a, b):
    return jnp.dot(a, b, preferred_element_type=F32)


def _sigmoid(x):
    return 0.5 * jnp.tanh(0.5 * x) + 0.5


def _split_bf16(a):
    hi = a.astype(BF16)
    return hi, (a - hi.astype(F32)).astype(BF16)


def _inproj_kernel(x_ref, g1_ref, w_ref, gb_ref, gq_ref, gk_ref, pw32_ref, ps_ref, wa32_ref, s2_ref, *rest,
                   widths, scale, n_later, attn_tile, attn_tiles):
    later_f32, rest = rest[:n_later], rest[n_later:]
    u_ref, qt_ref, k_ref, v_ref, g_ref, wpa_ref, plan_ref = rest[:7]
    later_bf16, wbf_ref = rest[7:7 + n_later], rest[7 + n_later]
    pw, aw, vw, gw = widths

    @pl.when(pl.program_id(0) < LATER_STEPS)
    def _():
        for src, dst in zip(later_f32, later_bf16):
            dst[...] = src[...].astype(BF16)

    @pl.when(pl.program_id(0) == 0)
    def _():
        wbf_ref[...] = w_ref[...].astype(BF16)
        for g in range(len(POOL_WINDOWS)):
            rows = slice(g * POOL_GROUP, (g + 1) * POOL_GROUP)
            p_hi, p_lo = _split_bf16(pw32_ref[rows, :] * ps_ref[:, rows])
            a_hi, a_lo = _split_bf16(wa32_ref[rows, :])
            wpa_ref[rows, :] = (_dot(p_hi, a_hi) + _dot(p_hi, a_lo) + _dot(p_lo, a_hi)).astype(BF16)

        bound = ((1.01 * LOG2E * math.sqrt(HEAD_DIM))
                 * jnp.max(jnp.abs(gq_ref[...]), axis=1, keepdims=True)
                 * jnp.max(jnp.abs(gk_ref[...]), axis=1, keepdims=True))
        bounded = bound <= SCORE_BOUND_MAX
        reach = jnp.floor(((bound - EXP2_ZERO_BELOW) / s2_ref[...] - 1.0) / attn_tile) + 1.0
        reach = jnp.clip(jnp.where(bounded, reach, float(attn_tiles)), 0.0, float(attn_tiles))
        lane = lax.broadcasted_iota(jnp.int32, (1, LANES), 1)
        plan_ref[...] = jnp.where(lane == 0, jnp.where(bounded, 1.0, 0.0), reach).astype(jnp.int32)

    x = x_ref[...]
    ms = jnp.mean(x * x, axis=-1, keepdims=True)
    h = (x * lax.rsqrt(ms + EPS) * g1_ref[...]).astype(BF16)

    def proj(lo, width):
        return _dot(h, wbf_ref[:, lo:lo + width])

    def group_rms_t(y):
        yt = y.T
        parts = []
        for g in range(y.shape[1] // HEAD_DIM):
            blk = yt[g * HEAD_DIM:(g + 1) * HEAD_DIM, :]
            parts.append(blk * lax.rsqrt(jnp.mean(blk * blk, axis=0, keepdims=True) + EPS))
        return jnp.concatenate(parts, axis=0)

    qt = (group_rms_t(proj(pw, aw)) * scale).astype(BF16)
    for hd in range(N_HEADS):
        qt_ref[0, hd, 0] = qt[hd * 2 * HEAD_DIM:(hd + 1) * 2 * HEAD_DIM, :]
    gain_row = jnp.concatenate([gq_ref[...] * gk_ref[...]] * (aw // HEAD_DIM), axis=1)
    k_ref[...] = (group_rms_t(proj(pw + aw, aw)).T * gain_row).astype(BF16)
    g0 = pw + 2 * aw + vw
    chunk = GATE_CHUNK
    for c in range(gw // chunk):
        gl = proj(g0 + c * chunk, chunk) + gb_ref[:, c * chunk:(c + 1) * chunk]
        g_ref[:, c * chunk:(c + 1) * chunk] = _sigmoid(gl).astype(BF16)
    v_ref[...] = proj(pw + 2 * aw, vw).astype(BF16)
    u_ref[...] = proj(0, pw)


def _attn_kernel(plan_ref, qt_ref, k_ref, v_ref, qc_ref, kc_ref, lq1_ref, lk1_ref, lq2_ref, lk2_ref,
                 sg_ref, o_ref, p_ref, m_ref, l_ref, acc_ref, *, tile, n_tiles):
    h = pl.program_id(1)
    bounded = plan_ref[0, 0] == 1
    n_reach = plan_ref[0, 1 + h]
    spare = n_tiles

    lane1 = lax.broadcasted_iota(jnp.int32, (1, LANES), 1)
    dist_lane = (lane1 >= 2 * BIAS_TERMS) & (lane1 < 3 * BIAS_TERMS)

    whole = (0, tile)
    qcols_t = qc_ref[0].T

    def scores(qi, j, masked, keys=whole, queries=whole):
        k0, nk = keys
        q0, nq = queries
        qt = qt_ref[0, 0, qi, :, q0:q0 + nq]
        zero = jnp.zeros_like(qt)
        dist = jnp.where(dist_lane, jnp.asarray(j - qi).astype(F32), 0.0).astype(BF16)
        ka = jnp.concatenate([k_ref[0, pl.ds(pl.multiple_of(j * tile, tile) + k0, nk), :],
                              kc_ref[0, k0:k0 + nk, :] + dist], axis=1)
        if masked:
            future = (lax.broadcasted_iota(jnp.int32, (nk, nq), 0) + k0
                      > lax.broadcasted_iota(jnp.int32, (nk, nq), 1) + q0)
        channel = lax.broadcasted_iota(jnp.int32, (2 * HEAD_DIM, nq), 0)
        out = []
        for c in range(2):
            qc = jnp.where(channel >= HEAD_DIM if c else channel < HEAD_DIM, qt, zero)
            qa = jnp.concatenate([qc, qcols_t[:, q0:q0 + nq]], axis=0)
            s = _dot(ka, qa)
            out.append(jnp.where(future, NEG_BIG, s) if masked else s)
        return out

    def values_t_probs(j, p, keys=whole):
        start = pl.multiple_of(j * tile, tile) + keys[0]
        return lax.dot_general(v_ref[0, pl.ds(start, keys[1]), :], p, (((0,), (0,)), ((), ())),
                               preferred_element_type=F32)

    l_ref[spare] = jnp.zeros(l_ref.shape[1:], F32)
    acc_ref[spare] = jnp.zeros(acc_ref.shape[1:], F32)

    @pl.when(bounded)
    def _():
        def probs(item, slot, masked):
            qi, j, state = item
            for c, s in enumerate(scores(qi, j, masked)):
                p = jnp.exp2(s)
                l_ref[state, c] += jnp.sum(p, axis=0, keepdims=True)
                p_ref[slot, c] = p.astype(BF16)

        def accumulate(item, slot):
            _, j, state = item
            for c in range(2):
                acc_ref[state, c] += values_t_probs(j, p_ref[slot, c])

        half = tile // 2
        first_keys, last_keys, last_queries = (0, half), (half, half), (half, half)

        def first_half_probs(i):
            for c, s in enumerate(scores(i, i, True, keys=first_keys)):
                p = jnp.exp2(s)
                l_ref[i, c] = jnp.sum(p, axis=0, keepdims=True)
                p_ref[0, c, 0:half, :] = p.astype(BF16)

        def first_half_accumulate(i):
            for c in range(2):
                acc_ref[i, c] = values_t_probs(i, p_ref[0, c, 0:half, :], keys=first_keys)

        def last_half_probs(i):
            for c, s in enumerate(scores(i, i, True, keys=last_keys, queries=last_queries)):
                p = jnp.exp2(s)
                part = jnp.sum(p, axis=0, keepdims=True)
                l_ref[i, c] += jnp.concatenate([jnp.zeros_like(part), part], axis=1)
                p_ref[1, c, 0:half, 0:half] = p.astype(BF16)

        def last_half_accumulate(i):
            for c in range(2):
                acc_ref[i, c, :, half:] += values_t_probs(i, p_ref[1, c, 0:half, 0:half], keys=last_keys)

        def below(cursor):
            d, qi = cursor
            live = d <= reach
            return jnp.where(live, qi, 0), jnp.where(live, qi - d, 0), jnp.where(live, qi, spare)

        def advance(cursor):
            d, qi = cursor
            wrap = qi + 1 >= n_tiles
            d = jnp.where(wrap, d + 1, d)
            return d, jnp.where(wrap, d, qi + 1)

        reach = jnp.minimum(n_reach, n_tiles - 1)
        n_below = reach * n_tiles - lax.shift_right_logical(reach * (reach + 1), 1)

        first_half_probs(0)

        def diagonal_tile(i):
            last_half_probs(i)
            first_half_accumulate(i)
            first_half_probs(i + 1)
            last_half_accumulate(i)

        def diagonal_tiles(t, carry):
            for u in range(DIAG_UNROLL):
                diagonal_tile(DIAG_UNROLL * t + u)
            return carry

        n_looped = (n_tiles - 1) // DIAG_UNROLL * DIAG_UNROLL
        lax.fori_loop(0, n_looped // DIAG_UNROLL, diagonal_tiles, 0)
        for i in range(n_looped, n_tiles - 1):
            diagonal_tile(i)
        first = (jnp.int32(1), jnp.int32(1))
        last_half_probs(n_tiles - 1)
        first_half_accumulate(n_tiles - 1)
        probs(below(first), 0, False)
        last_half_accumulate(n_tiles - 1)

        def below_group(t, cursor):
            for u in range(BELOW_UNROLL):
                nxt = advance(cursor)
                probs(below(nxt), (u + 1) % 2, False)
                accumulate(below(cursor), u % 2)
                cursor = nxt
            return cursor

        lax.fori_loop(0, (n_below + BELOW_UNROLL - 1) // BELOW_UNROLL, below_group, first)

    @pl.when(jnp.logical_not(bounded))
    def _():
        def block(qi, j, masked):
            for c, s in enumerate(scores(qi, j, masked)):
                m_old = m_ref[c]
                m_new = jnp.maximum(m_old, jnp.max(s, axis=0, keepdims=True))
                p = jnp.exp2(s - m_new)
                alpha = jnp.exp2(m_old - m_new)
                l_ref[qi, c] = alpha * l_ref[qi, c] + jnp.sum(p, axis=0, keepdims=True)
                acc_ref[qi, c] = alpha * acc_ref[qi, c] + values_t_probs(j, p.astype(BF16))
                m_ref[c] = m_new

        def query_tile(qi, carry):
            m_ref[...] = jnp.full(m_ref.shape, NEG_BIG, F32)
            l_ref[qi] = jnp.zeros(l_ref.shape[1:], F32)
            acc_ref[qi] = jnp.zeros(acc_ref.shape[1:], F32)
            block(qi, qi, True)

            def body(t, c):
                block(qi, qi - 1 - t, False)
                return c

            lax.fori_loop(0, qi, body, 0)
            return carry

        lax.fori_loop(0, n_tiles, query_tile, 0)

    lam = (jnp.exp(jnp.sum(lq1_ref[...] * lk1_ref[...], axis=-1, keepdims=True))
           - jnp.exp(jnp.sum(lq2_ref[...] * lk2_ref[...], axis=-1, keepdims=True))
           + LAMBDA_INIT)
    diagonal = (lax.broadcasted_iota(jnp.int32, (V_DIM, V_DIM), 0)
                == lax.broadcasted_iota(jnp.int32, (V_DIM, V_DIM), 1))
    gain_col = jnp.sum(jnp.where(diagonal, sg_ref[...], 0.0), axis=1, keepdims=True) * (1.0 - LAMBDA_INIT)

    def finish(qi):
        o = acc_ref[qi, 0] * (1.0 / l_ref[qi, 0]) - acc_ref[qi, 1] * (lam / l_ref[qi, 1])
        msq = jnp.mean(o * o, axis=0, keepdims=True)
        y = o * lax.rsqrt(msq + EPS) * gain_col
        o_ref[0, pl.ds(pl.multiple_of(qi * tile, tile), tile), :] = y.astype(BF16).T

    def finish_group(t, carry):
        for u in range(FINISH_UNROLL):
            finish(FINISH_UNROLL * t + u)
        return carry

    assert n_tiles % FINISH_UNROLL == 0
    lax.fori_loop(0, n_tiles // FINISH_UNROLL, finish_group, 0)


def _merge_ffn_kernel(x_ref, u_ref, halo_ref, a_ref, g_ref, g2_ref,
                      wpa_ref, wb_ref, wo_ref, w1_ref, w2_ref, o_ref, ext_ref, mid_ref,
                      *, seq, tm, chunk, ff_chunk):
    i = pl.program_id(0)
    n_ff = w1_ref.shape[1] // ff_chunk
    t0 = (i * tm) % seq
    halo = halo_ref[...]
    ext_ref[0:HALO, :] = jnp.where(t0 == 0, jnp.zeros_like(halo), halo)
    ext_ref[HALO:, :] = u_ref[...]
    d = o_ref.shape[1]

    for r0 in range(0, tm, chunk):
        rows = slice(r0, r0 + chunk)
        t = t0 + r0 + lax.broadcasted_iota(jnp.int32, (chunk, 1), 0)
        pooled = []
        for g, w in enumerate(POOL_WINDOWS):
            y = ext_ref[r0:r0 + chunk + HALO, g * POOL_GROUP:(g + 1) * POOL_GROUP]
            tot = y
            span = 1
            while span < w:
                tot = tot + pltpu.roll(tot, span, axis=0)
                span *= 2
            cnt = jnp.minimum(t + 1, w).astype(F32)
            pooled.append((tot[HALO:] / cnt - y[HALO:]).astype(BF16))
        ya = _dot(jnp.concatenate(pooled, axis=1), wpa_ref[...])
        yb = _dot(a_ref[rows, :], wb_ref[...])
        merged = g_ref[rows, 0:d].astype(F32) * ya + g_ref[rows, d:2 * d].astype(F32) * yb
        mid_ref[rows, :] = x_ref[rows, :] + _dot(merged.astype(BF16), wo_ref[...])

    x = mid_ref[...]
    ms = jnp.mean(x * x, axis=-1, keepdims=True)
    h = (x * lax.rsqrt(ms + EPS) * g2_ref[...]).astype(BF16)
    acc = x
    for c in range(n_ff):
        cols = slice(c * ff_chunk, (c + 1) * ff_chunk)
        a = jnp.maximum(_dot(h, w1_ref[:, cols]), 0.0)
        acc = acc + _dot((a * a).astype(BF16), w2_ref[cols, :])
    o_ref[...] = acc


def _alibi_columns(tile):
    slopes2 = np.array([2.0 ** (-8.0 * (h + 1) / N_HEADS) for h in range(N_HEADS)]) * LOG2E
    pos = np.arange(tile, dtype=np.float64)

    def terms(x):
        out = []
        for _ in range(BIAS_TERMS):
            part = x.astype(BF16)
            out.append(part)
            x = x - part.astype(np.float64)
        return out

    kcols = np.zeros((N_HEADS, tile, LANES), BF16)
    qcols = np.zeros((N_HEADS, tile, LANES), BF16)
    k_terms = terms(slopes2[:, None] * pos[None, :])
    q_terms = terms(-slopes2[:, None] * pos[None, :])
    t_terms = terms(np.broadcast_to(slopes2[:, None] * tile, (N_HEADS, tile)))
    for t in range(BIAS_TERMS):
        kcols[:, :, t] = k_terms[t]
        kcols[:, :, BIAS_TERMS + t] = 1.0
        qcols[:, :, t] = 1.0
        qcols[:, :, BIAS_TERMS + t] = q_terms[t]
        qcols[:, :, 2 * BIAS_TERMS + t] = t_terms[t]
    return jnp.asarray(qcols), jnp.asarray(kcols), slopes2.astype(np.float32)


def _const_spec(shape):
    return pl.BlockSpec(shape, lambda *_: (0,) * len(shape))


def _resident_spec(shape):
    return pl.BlockSpec(shape, lambda *_: (0,) * len(shape), pipeline_mode=pl.Buffered(1))


def _params(n_axes):
    return pltpu.CompilerParams(dimension_semantics=("arbitrary",) * n_axes,
                                vmem_limit_bytes=VMEM_LIMIT)


def kernel(x, norm1_g, w_in, gate_b, pool_w, pool_scale, q_norm_g, k_norm_g, lambda_q1, lambda_k1,
           lambda_q2, lambda_k2, subln_g, w_branch_a, w_branch_b, w_out, norm2_g, w_ff1, w_ff2):
    B, S, D = x.shape
    depth = norm1_g.shape[0]
    assert depth == 1
    M = B * S
    pw = pool_scale.shape[1]
    aw = 2 * N_HEADS * HEAD_DIM
    vw = N_HEADS * V_DIM
    gw = 2 * D
    d_ff = w_ff1.shape[2]
    tm = ROW_TILE
    ta = ATTN_TILE
    assert S % tm == 0 and S % ta == 0 and tm % HALO == 0
    assert d_ff % FFN_CHUNK == 0
    assert pw == len(POOL_WINDOWS) * POOL_GROUP and w_in.shape[2] == pw + 2 * aw + vw + gw

    x2 = x.reshape(M, D)
    row = lambda width: pl.BlockSpec((tm, width), lambda i: (i, 0))

    assert tm == ta
    n_tiles = S // ta
    steps = M // tm
    later = [w_branch_b[0], w_out[0], w_ff1[0], w_ff2[0]]
    assert steps >= LATER_STEPS and all(w.shape[0] % (LATER_STEPS * BF16_SUBLANES) == 0 for w in later)
    row_slice = lambda w: pl.BlockSpec((w.shape[0] // LATER_STEPS, w.shape[1]),
                                       lambda i: (jnp.minimum(i, LATER_STEPS - 1), 0))
    qcols, kcols, slopes2 = _alibi_columns(ta)
    s2_row = np.ones((1, LANES), np.float32)
    s2_row[0, 1:1 + N_HEADS] = slopes2
    u, qt, k, v, gates, w_pool_a, plan, *later_bf16 = pl.pallas_call(
        functools.partial(_inproj_kernel, widths=(pw, aw, vw, gw), scale=LOG2E / math.sqrt(HEAD_DIM),
                          n_later=len(later), attn_tile=ta, attn_tiles=n_tiles),
        grid=(steps,),
        in_specs=[row(D), _const_spec((1, D)), _resident_spec(w_in.shape[1:]), _const_spec((1, gw)),
                  _const_spec((1, HEAD_DIM)), _const_spec((1, HEAD_DIM)),
                  _const_spec((pw, POOL_GROUP)), _const_spec((1, pw)), _const_spec((pw, D)),
                  _const_spec((1, LANES))]
                 + [row_slice(w) for w in later],
        out_specs=[row(pw),
                   pl.BlockSpec((1, N_HEADS, 1, 2 * HEAD_DIM, ta),
                                lambda i: (i // n_tiles, 0, i % n_tiles, 0, 0)),
                   row(aw), row(vw), row(gw), _const_spec((pw, D)), _const_spec((1, LANES))]
                  + [row_slice(w) for w in later],
        out_shape=[jax.ShapeDtypeStruct((M, pw), F32),
                   jax.ShapeDtypeStruct((B, N_HEADS, n_tiles, 2 * HEAD_DIM, ta), BF16),
                   jax.ShapeDtypeStruct((M, aw), BF16), jax.ShapeDtypeStruct((M, vw), BF16),
                   jax.ShapeDtypeStruct((M, gw), BF16), jax.ShapeDtypeStruct((pw, D), BF16),
                   jax.ShapeDtypeStruct((1, LANES), jnp.int32)]
                  + [jax.ShapeDtypeStruct(w.shape, BF16) for w in later],
        scratch_shapes=[pltpu.VMEM(w_in.shape[1:], BF16)],
        compiler_params=_params(1),
        name="inproj",
    )(x2, norm1_g, w_in[0], gate_b, q_norm_g, k_norm_g,
      pool_w[0].reshape(pw, POOL_GROUP), pool_scale, w_branch_a[0], jnp.asarray(s2_row), *later)

    head_seq = pl.BlockSpec((1, S, LANES), lambda b, h: (b, 0, h))
    attn = pl.pallas_call(
        functools.partial(_attn_kernel, tile=ta, n_tiles=n_tiles),
        grid=(B, N_HEADS),
        in_specs=[pl.BlockSpec(memory_space=pltpu.SMEM),
                  pl.BlockSpec((1, 1, n_tiles, 2 * HEAD_DIM, ta), lambda b, h: (b, h, 0, 0, 0)),
                  head_seq, head_seq,
                  pl.BlockSpec((1, ta, LANES), lambda b, h: (h, 0, 0)),
                  pl.BlockSpec((1, ta, LANES), lambda b, h: (h, 0, 0)),
                  _const_spec((1, HEAD_DIM)), _const_spec((1, HEAD_DIM)),
                  _const_spec((1, HEAD_DIM)), _const_spec((1, HEAD_DIM)), _const_spec((1, V_DIM))],
        out_specs=head_seq,
        out_shape=jax.ShapeDtypeStruct((B, S, vw), BF16),
        scratch_shapes=[pltpu.VMEM((2, 2, ta, ta), BF16),
                        pltpu.VMEM((2, 1, ta), F32),
                        pltpu.VMEM((n_tiles + 1, 2, 1, ta), F32),
                        pltpu.VMEM((n_tiles + 1, 2, V_DIM, ta), F32)],
        compiler_params=_params(2),
        name="diffattn",
    )(plan, qt, k.reshape(B, S, aw), v.reshape(B, S, vw), qcols, kcols,
      lambda_q1, lambda_k1, lambda_q2, lambda_k2, subln_g)

    out = pl.pallas_call(
        functools.partial(_merge_ffn_kernel, seq=S, tm=tm, chunk=MERGE_CHUNK, ff_chunk=FFN_CHUNK),
        grid=(M // tm,),
        in_specs=[row(D), row(pw),
                  pl.BlockSpec((HALO, pw), lambda i: (jnp.maximum(i * (tm // HALO) - 1, 0), 0)),
                  row(vw), row(gw), _const_spec((1, D))]
                 + [_resident_spec(w.shape) for w in [w_pool_a] + later_bf16],
        out_specs=row(D),
        out_shape=jax.ShapeDtypeStruct((M, D), F32),
        scratch_shapes=[pltpu.VMEM((HALO + tm, pw), F32),
                        pltpu.VMEM((tm, D), F32)],
        compiler_params=_params(1),
        name="merge_ffn",
    )(x2, u, u, attn.reshape(M, vw), gates, norm2_g, w_pool_a, *later_bf16)
    return out.reshape(B, S, D)
```

```python
import functools
import math

import numpy as np
import jax
import jax.numpy as jnp
from jax import lax
from jax.experimental import pallas as pl
from jax.experimental.pallas import tpu as pltpu

F32 = jnp.float32
BF16 = jnp.bfloat16
EPS = 1e-6

POOL_WINDOWS = (2, 4, 8, 16)
POOL_GROUP = 128
HALO = max(POOL_WINDOWS)
N_HEADS = 4
HEAD_DIM = 64
V_DIM = 2 * HEAD_DIM
LANES = 128
BF16_SUBLANES = 16
LAMBDA_INIT = 0.8 - 0.6 * math.exp(-0.3 * 0)
NEG_BIG = -1e30
LOG2E = math.log2(math.e)
BIAS_TERMS = 3
SCORE_BOUND_MAX = 60.0
EXP2_ZERO_BELOW = -150.0

ROW_TILE = 512
GATE_CHUNK = 512
FFN_CHUNK = 1024
LATER_STEPS = 16
MERGE_CHUNK = 512
ATTN_TILE = 512
DIAG_UNROLL = 15
FINISH_UNROLL = 16
BELOW_UNROLL = 8
VMEM_LIMIT = 56 * 1024 * 1024


def _dot(a, b):
    return jnp.dot(a, b, preferred_element_type=F32)


def _sigmoid(x):
    return 0.5 * jnp.tanh(0.5 * x) + 0.5


def _split_bf16(a):
    hi = a.astype(BF16)
    return hi, (a - hi.astype(F32)).astype(BF16)


def _inproj_kernel(x_ref, g1_ref, w_ref, gb_ref, gq_ref, gk_ref, pw32_ref, ps_ref, wa32_ref, s2_ref, *rest,
                   widths, scale, n_later, attn_tile, attn_tiles):
    later_f32, rest = rest[:n_later], rest[n_later:]
    u_ref, qt_ref, k_ref, v_ref, g_ref, wpa_ref, plan_ref = rest[:7]
    later_bf16, wbf_ref = rest[7:7 + n_later], rest[7 + n_later]
    pw, aw, vw, gw = widths

    @pl.when(pl.program_id(0) < LATER_STEPS)
    def _():
        for src, dst in zip(later_f32, later_bf16):
            dst[...] = src[...].astype(BF16)

    @pl.when(pl.program_id(0) == 0)
    def _():
        wbf_ref[...] = w_ref[...].astype(BF16)
        for g in range(len(POOL_WINDOWS)):
            rows = slice(g * POOL_GROUP, (g + 1) * POOL_GROUP)
            p_hi, p_lo = _split_bf16(pw32_ref[rows, :] * ps_ref[:, rows])
            a_hi, a_lo = _split_bf16(wa32_ref[rows, :])
            wpa_ref[rows, :] = (_dot(p_hi, a_hi) + _dot(p_hi, a_lo) + _dot(p_lo, a_hi)).astype(BF16)

        bound = ((1.01 * LOG2E * math.sqrt(HEAD_DIM))
                 * jnp.max(jnp.abs(gq_ref[...]), axis=1, keepdims=True)
                 * jnp.max(jnp.abs(gk_ref[...]), axis=1, keepdims=True))
        bounded = bound <= SCORE_BOUND_MAX
        reach = jnp.floor(((bound - EXP2_ZERO_BELOW) / s2_ref[...] - 1.0) / attn_tile) + 1.0
        reach = jnp.clip(jnp.where(bounded, reach, float(attn_tiles)), 0.0, float(attn_tiles))
        lane = lax.broadcasted_iota(jnp.int32, (1, LANES), 1)
        plan_ref[...] = jnp.where(lane == 0, jnp.where(bounded, 1.0, 0.0), reach).astype(jnp.int32)

    x = x_ref[...]
    ms = jnp.mean(x * x, axis=-1, keepdims=True)
    h = (x * lax.rsqrt(ms + EPS) * g1_ref[...]).astype(BF16)

    def proj(lo, width):
        return _dot(h, wbf_ref[:, lo:lo + width])

    def group_rms_t(y):
        yt = y.T
        parts = []
        for g in range(y.shape[1] // HEAD_DIM):
            blk = yt[g * HEAD_DIM:(g + 1) * HEAD_DIM, :]
            parts.append(blk * lax.rsqrt(jnp.mean(blk * blk, axis=0, keepdims=True) + EPS))
        return jnp.concatenate(parts, axis=0)

    qt = (group_rms_t(proj(pw, aw)) * scale).astype(BF16)
    for hd in range(N_HEADS):
        qt_ref[0, hd, 0] = qt[hd * 2 * HEAD_DIM:(hd + 1) * 2 * HEAD_DIM, :]
    gain_row = jnp.concatenate([gq_ref[...] * gk_ref[...]] * (aw // HEAD_DIM), axis=1)
    k_ref[...] = (group_rms_t(proj(pw + aw, aw)).T * gain_row).astype(BF16)
    g0 = pw + 2 * aw + vw
    chunk = GATE_CHUNK
    for c in range(gw // chunk):
        gl = proj(g0 + c * chunk, chunk) + gb_ref[:, c * chunk:(c + 1) * chunk]
        g_ref[:, c * chunk:(c + 1) * chunk] = _sigmoid(gl).astype(BF16)
    v_ref[...] = proj(pw + 2 * aw, vw).astype(BF16)
    u_ref[...] = proj(0, pw)


def _attn_kernel(plan_ref, qt_ref, k_ref, v_ref, qc_ref, kc_ref, lq1_ref, lk1_ref, lq2_ref, lk2_ref,
                 sg_ref, o_ref, p_ref, m_ref, l_ref, acc_ref, *, tile, n_tiles):
    h = pl.program_id(1)
    bounded = plan_ref[0, 0] == 1
    n_reach = plan_ref[0, 1 + h]
    spare = n_tiles

    lane1 = lax.broadcasted_iota(jnp.int32, (1, LANES), 1)
    dist_lane = (lane1 >= 2 * BIAS_TERMS) & (lane1 < 3 * BIAS_TERMS)

    whole = (0, tile)
    qcols_t = qc_ref[0].T

    def scores(qi, j, masked, keys=whole, queries=whole):
        k0, nk = keys
        q0, nq = queries
        qt = qt_ref[0, 0, qi, :, q0:q0 + nq]
        zero = jnp.zeros_like(qt)
        dist = jnp.where(dist_lane, jnp.asarray(j - qi).astype(F32), 0.0).astype(BF16)
        ka = jnp.concatenate([k_ref[0, pl.ds(pl.multiple_of(j * tile, tile) + k0, nk), :],
                              kc_ref[0, k0:k0 + nk, :] + dist], axis=1)
        if masked:
            future = (lax.broadcasted_iota(jnp.int32, (nk, nq), 0) + k0
                      > lax.broadcasted_iota(jnp.int32, (nk, nq), 1) + q0)
        channel = lax.broadcasted_iota(jnp.int32, (2 * HEAD_DIM, nq), 0)
        out = []
        for c in range(2):
            qc = jnp.where(channel >= HEAD_DIM if c else channel < HEAD_DIM, qt, zero)
            qa = jnp.concatenate([qc, qcols_t[:, q0:q0 + nq]], axis=0)
            s = _dot(ka, qa)
            out.append(jnp.where(future, NEG_BIG, s) if masked else s)
        return out

    def values_t_probs(j, p, keys=whole):
        start = pl.multiple_of(j * tile, tile) + keys[0]
        return lax.dot_general(v_ref[0, pl.ds(start, keys[1]), :], p, (((0,), (0,)), ((), ())),
                               preferred_element_type=F32)

    l_ref[spare] = jnp.zeros(l_ref.shape[1:], F32)
    acc_ref[spare] = jnp.zeros(acc_ref.shape[1:], F32)

    @pl.when(bounded)
    def _():
        def probs(item, slot, masked):
            qi, j, state = item
            for c, s in enumerate(scores(qi, j, masked)):
                p = jnp.exp2(s)
                l_ref[state, c] += jnp.sum(p, axis=0, keepdims=True)
                p_ref[slot, c] = p.astype(BF16)

        def accumulate(item, slot):
            _, j, state = item
            for c in range(2):
                acc_ref[state, c] += values_t_probs(j, p_ref[slot, c])

        half = tile // 2
        first_keys, last_keys, last_queries = (0, half), (half, half), (half, half)

        def first_half_probs(i):
            for c, s in enumerate(scores(i, i, True, keys=first_keys)):
                p = jnp.exp2(s)
                l_ref[i, c] = jnp.sum(p, axis=0, keepdims=True)
                p_ref[0, c, 0:half, :] = p.astype(BF16)

        def first_half_accumulate(i):
            for c in range(2):
                acc_ref[i, c] = values_t_probs(i, p_ref[0, c, 0:half, :], keys=first_keys)

        def last_half_probs(i):
            for c, s in enumerate(scores(i, i, True, keys=last_keys, queries=last_queries)):
                p = jnp.exp2(s)
                part = jnp.sum(p, axis=0, keepdims=True)
                l_ref[i, c] += jnp.concatenate([jnp.zeros_like(part), part], axis=1)
                p_ref[1, c, 0:half, 0:half] = p.astype(BF16)

        def last_half_accumulate(i):
            for c in range(2):
                acc_ref[i, c, :, half:] += values_t_probs(i, p_ref[1, c, 0:half, 0:half], keys=last_keys)

        def below(cursor):
            d, qi = cursor
            live = d <= reach
            return jnp.where(live, qi, 0), jnp.where(live, qi - d, 0), jnp.where(live, qi, spare)

        def advance(cursor):
            d, qi = cursor
            wrap = qi + 1 >= n_tiles
            d = jnp.where(wrap, d + 1, d)
            return d, jnp.where(wrap, d, qi + 1)

        reach = jnp.minimum(n_reach, n_tiles - 1)
        n_below = reach * n_tiles - lax.shift_right_logical(reach * (reach + 1), 1)

        first_half_probs(0)

        def diagonal_tile(i):
            last_half_probs(i)
            first_half_accumulate(i)
            first_half_probs(i + 1)
            last_half_accumulate(i)

        def diagonal_tiles(t, carry):
            for u in range(DIAG_UNROLL):
                diagonal_tile(DIAG_UNROLL * t + u)
            return carry

        n_looped = (n_tiles - 1) // DIAG_UNROLL * DIAG_UNROLL
        lax.fori_loop(0, n_looped // DIAG_UNROLL, diagonal_tiles, 0)
        for i in range(n_looped, n_tiles - 1):
            diagonal_tile(i)
        first = (jnp.int32(1), jnp.int32(1))
        last_half_probs(n_tiles - 1)
        first_half_accumulate(n_tiles - 1)
        probs(below(first), 0, False)
        last_half_accumulate(n_tiles - 1)

        def below_group(t, cursor):
            for u in range(BELOW_UNROLL):
                nxt = advance(cursor)
                probs(below(nxt), (u + 1) % 2, False)
                accumulate(below(cursor), u % 2)
                cursor = nxt
            return cursor

        lax.fori_loop(0, (n_below + BELOW_UNROLL - 1) // BELOW_UNROLL, below_group, first)

    @pl.when(jnp.logical_not(bounded))
    def _():
        def block(qi, j, masked):
            for c, s in enumerate(scores(qi, j, masked)):
                m_old = m_ref[c]
                m_new = jnp.maximum(m_old, jnp.max(s, axis=0, keepdims=True))
                p = jnp.exp2(s - m_new)
                alpha = jnp.exp2(m_old - m_new)
                l_ref[qi, c] = alpha * l_ref[qi, c] + jnp.sum(p, axis=0, keepdims=True)
                acc_ref[qi, c] = alpha * acc_ref[qi, c] + values_t_probs(j, p.astype(BF16))
                m_ref[c] = m_new

        def query_tile(qi, carry):
            m_ref[...] = jnp.full(m_ref.shape, NEG_BIG, F32)
            l_ref[qi] = jnp.zeros(l_ref.shape[1:], F32)
            acc_ref[qi] = jnp.zeros(acc_ref.shape[1:], F32)
            block(qi, qi, True)

            def body(t, c):
                block(qi, qi - 1 - t, False)
                return c

            lax.fori_loop(0, qi, body, 0)
            return carry

        lax.fori_loop(0, n_tiles, query_tile, 0)

    lam = (jnp.exp(jnp.sum(lq1_ref[...] * lk1_ref[...], axis=-1, keepdims=True))
           - jnp.exp(jnp.sum(lq2_ref[...] * lk2_ref[...], axis=-1, keepdims=True))
           + LAMBDA_INIT)
    diagonal = (lax.broadcasted_iota(jnp.int32, (V_DIM, V_DIM), 0)
                == lax.broadcasted_iota(jnp.int32, (V_DIM, V_DIM), 1))
    gain_col = jnp.sum(jnp.where(diagonal, sg_ref[...], 0.0), axis=1, keepdims=True) * (1.0 - LAMBDA_INIT)

    def finish(qi):
        o = acc_ref[qi, 0] * (1.0 / l_ref[qi, 0]) - acc_ref[qi, 1] * (lam / l_ref[qi, 1])
        msq = jnp.mean(o * o, axis=0, keepdims=True)
        y = o * lax.rsqrt(msq + EPS) * gain_col
        o_ref[0, pl.ds(pl.multiple_of(qi * tile, tile), tile), :] = y.astype(BF16).T

    def finish_group(t, carry):
        for u in range(FINISH_UNROLL):
            finish(FINISH_UNROLL * t + u)
        return carry

    assert n_tiles % FINISH_UNROLL == 0
    lax.fori_loop(0, n_tiles // FINISH_UNROLL, finish_group, 0)


def _merge_ffn_kernel(x_ref, u_ref, halo_ref, a_ref, g_ref, g2_ref,
                      wpa_ref, wb_ref, wo_ref, w1_ref, w2_ref, o_ref, ext_ref, mid_ref,
                      *, seq, tm, chunk, ff_chunk):
    i = pl.program_id(0)
    n_ff = w1_ref.shape[1] // ff_chunk
    t0 = (i * tm) % seq
    halo = halo_ref[...]
    ext_ref[0:HALO, :] = jnp.where(t0 == 0, jnp.zeros_like(halo), halo)
    ext_ref[HALO:, :] = u_ref[...]
    d = o_ref.shape[1]

    for r0 in range(0, tm, chunk):
        rows = slice(r0, r0 + chunk)
        t = t0 + r0 + lax.broadcasted_iota(jnp.int32, (chunk, 1), 0)
        pooled = []
        for g, w in enumerate(POOL_WINDOWS):
            y = ext_ref[r0:r0 + chunk + HALO, g * POOL_GROUP:(g + 1) * POOL_GROUP]
            tot = y
            span = 1
            while span < w:
                tot = tot + pltpu.roll(tot, span, axis=0)
                span *= 2
            cnt = jnp.minimum(t + 1, w).astype(F32)
            pooled.append((tot[HALO:] / cnt - y[HALO:]).astype(BF16))
        ya = _dot(jnp.concatenate(pooled, axis=1), wpa_ref[...])
        yb = _dot(a_ref[rows, :], wb_ref[...])
        merged = g_ref[rows, 0:d].astype(F32) * ya + g_ref[rows, d:2 * d].astype(F32) * yb
        mid_ref[rows, :] = x_ref[rows, :] + _dot(merged.astype(BF16), wo_ref[...])

    x = mid_ref[...]
    ms = jnp.mean(x * x, axis=-1, keepdims=True)
    h = (x * lax.rsqrt(ms + EPS) * g2_ref[...]).astype(BF16)
    acc = x
    for c in range(n_ff):
        cols = slice(c * ff_chunk, (c + 1) * ff_chunk)
        a = jnp.maximum(_dot(h, w1_ref[:, cols]), 0.0)
        acc = acc + _dot((a * a).astype(BF16), w2_ref[cols, :])
    o_ref[...] = acc


def _alibi_columns(tile):
    slopes2 = np.array([2.0 ** (-8.0 * (h + 1) / N_HEADS) for h in range(N_HEADS)]) * LOG2E
    pos = np.arange(tile, dtype=np.float64)

    def terms(x):
        out = []
        for _ in range(BIAS_TERMS):
            part = x.astype(BF16)
            out.append(part)
            x = x - part.astype(np.float64)
        return out

    kcols = np.zeros((N_HEADS, tile, LANES), BF16)
    qcols = np.zeros((N_HEADS, tile, LANES), BF16)
    k_terms = terms(slopes2[:, None] * pos[None, :])
    q_terms = terms(-slopes2[:, None] * pos[None, :])
    t_terms = terms(np.broadcast_to(slopes2[:, None] * tile, (N_HEADS, tile)))
    for t in range(BIAS_TERMS):
        kcols[:, :, t] = k_terms[t]
        kcols[:, :, BIAS_TERMS + t] = 1.0
        qcols[:, :, t] = 1.0
        qcols[:, :, BIAS_TERMS + t] = q_terms[t]
        qcols[:, :, 2 * BIAS_TERMS + t] = t_terms[t]
    return jnp.asarray(qcols), jnp.asarray(kcols), slopes2.astype(np.float32)


def _const_spec(shape):
    return pl.BlockSpec(shape, lambda *_: (0,) * len(shape))


def _resident_spec(shape):
    return pl.BlockSpec(shape, lambda *_: (0,) * len(shape), pipeline_mode=pl.Buffered(1))


def _params(n_axes):
    return pltpu.CompilerParams(dimension_semantics=("arbitrary",) * n_axes,
                                vmem_limit_bytes=VMEM_LIMIT)


def kernel(x, norm1_g, w_in, gate_b, pool_w, pool_scale, q_norm_g, k_norm_g, lambda_q1, lambda_k1,
           lambda_q2, lambda_k2, subln_g, w_branch_a, w_branch_b, w_out, norm2_g, w_ff1, w_ff2):
    B, S, D = x.shape
    depth = norm1_g.shape[0]
    assert depth == 1
    M = B * S
    pw = pool_scale.shape[1]
    aw = 2 * N_HEADS * HEAD_DIM
    vw = N_HEADS * V_DIM
    gw = 2 * D
    d_ff = w_ff1.shape[2]
    tm = ROW_TILE
    ta = ATTN_TILE
    assert S % tm == 0 and S % ta == 0 and tm % HALO == 0
    assert d_ff % FFN_CHUNK == 0
    assert pw == len(POOL_WINDOWS) * POOL_GROUP and w_in.shape[2] == pw + 2 * aw + vw + gw

    x2 = x.reshape(M, D)
    row = lambda width: pl.BlockSpec((tm, width), lambda i: (i, 0))

    assert tm == ta
    n_tiles = S // ta
    steps = M // tm
    later = [w_branch_b[0], w_out[0], w_ff1[0], w_ff2[0]]
    assert steps >= LATER_STEPS and all(w.shape[0] % (LATER_STEPS * BF16_SUBLANES) == 0 for w in later)
    row_slice = lambda w: pl.BlockSpec((w.shape[0] // LATER_STEPS, w.shape[1]),
                                       lambda i: (jnp.minimum(i, LATER_STEPS - 1), 0))
    qcols, kcols, slopes2 = _alibi_columns(ta)
    s2_row = np.ones((1, LANES), np.float32)
    s2_row[0, 1:1 + N_HEADS] = slopes2
    u, qt, k, v, gates, w_pool_a, plan, *later_bf16 = pl.pallas_call(
        functools.partial(_inproj_kernel, widths=(pw, aw, vw, gw), scale=LOG2E / math.sqrt(HEAD_DIM),
                          n_later=len(later), attn_tile=ta, attn_tiles=n_tiles),
        grid=(steps,),
        in_specs=[row(D), _const_spec((1, D)), _resident_spec(w_in.shape[1:]), _const_spec((1, gw)),
                  _const_spec((1, HEAD_DIM)), _const_spec((1, HEAD_DIM)),
                  _const_spec((pw, POOL_GROUP)), _const_spec((1, pw)), _const_spec((pw, D)),
                  _const_spec((1, LANES))]
                 + [row_slice(w) for w in later],
        out_specs=[row(pw),
                   pl.BlockSpec((1, N_HEADS, 1, 2 * HEAD_DIM, ta),
                                lambda i: (i // n_tiles, 0, i % n_tiles, 0, 0)),
                   row(aw), row(vw), row(gw), _const_spec((pw, D)), _const_spec((1, LANES))]
                  + [row_slice(w) for w in later],
        out_shape=[jax.ShapeDtypeStruct((M, pw), F32),
                   jax.ShapeDtypeStruct((B, N_HEADS, n_tiles, 2 * HEAD_DIM, ta), BF16),
                   jax.ShapeDtypeStruct((M, aw), BF16), jax.ShapeDtypeStruct((M, vw), BF16),
                   jax.ShapeDtypeStruct((M, gw), BF16), jax.ShapeDtypeStruct((pw, D), BF16),
                   jax.ShapeDtypeStruct((1, LANES), jnp.int32)]
                  + [jax.ShapeDtypeStruct(w.shape, BF16) for w in later],
        scratch_shapes=[pltpu.VMEM(w_in.shape[1:], BF16)],
        compiler_params=_params(1),
        name="inproj",
    )(x2, norm1_g, w_in[0], gate_b, q_norm_g, k_norm_g,
      pool_w[0].reshape(pw, POOL_GROUP), pool_scale, w_branch_a[0], jnp.asarray(s2_row), *later)

    head_seq = pl.BlockSpec((1, S, LANES), lambda b, h: (b, 0, h))
    attn = pl.pallas_call(
        functools.partial(_attn_kernel, tile=ta, n_tiles=n_tiles),
        grid=(B, N_HEADS),
        in_specs=[pl.BlockSpec(memory_space=pltpu.SMEM),
                  pl.BlockSpec((1, 1, n_tiles, 2 * HEAD_DIM, ta), lambda b, h: (b, h, 0, 0, 0)),
                  head_seq, head_seq,
                  pl.BlockSpec((1, ta, LANES), lambda b, h: (h, 0, 0)),
                  pl.BlockSpec((1, ta, LANES), lambda b, h: (h, 0, 0)),
                  _const_spec((1, HEAD_DIM)), _const_spec((1, HEAD_DIM)),
                  _const_spec((1, HEAD_DIM)), _const_spec((1, HEAD_DIM)), _const_spec((1, V_DIM))],
        out_specs=head_seq,
        out_shape=jax.ShapeDtypeStruct((B, S, vw), BF16),
        scratch_shapes=[pltpu.VMEM((2, 2, ta, ta), BF16),
                        pltpu.VMEM((2, 1, ta), F32),
                        pltpu.VMEM((n_tiles + 1, 2, 1, ta), F32),
                        pltpu.VMEM((n_tiles + 1, 2, V_DIM, ta), F32)],
        compiler_params=_params(2),
        name="diffattn",
    )(plan, qt, k.reshape(B, S, aw), v.reshape(B, S, vw), qcols, kcols,
      lambda_q1, lambda_k1, lambda_q2, lambda_k2, subln_g)

    out = pl.pallas_call(
        functools.partial(_merge_ffn_kernel, seq=S, tm=tm, chunk=MERGE_CHUNK, ff_chunk=FFN_CHUNK),
        grid=(M // tm,),
        in_specs=[row(D), row(pw),
                  pl.BlockSpec((HALO, pw), lambda i: (jnp.maximum(i * (tm // HALO) - 1, 0), 0)),
                  row(vw), row(gw), _const_spec((1, D))]
                 + [_resident_spec(w.shape) for w in [w_pool_a] + later_bf16],
        out_specs=row(D),
        out_shape=jax.ShapeDtypeStruct((M, D), F32),
        scratch_shapes=[pltpu.VMEM((HALO + tm, pw), F32),
                        pltpu.VMEM((tm, D), F32)],
        compiler_params=_params(1),
        name="merge_ffn",
    )(x2, u, u, attn.reshape(M, vw), gates, norm2_g, w_pool_a, *later_bf16)
    return out.reshape(B, S, D)
```

```python
import functools
import math

import numpy as np
import jax
import jax.numpy as jnp
from jax import lax
from jax.experimental import pallas as pl
from jax.experimental.pallas import tpu as pltpu

F32 = jnp.float32
BF16 = jnp.bfloat16
EPS = 1e-6

POOL_WINDOWS = (2, 4, 8, 16)
POOL_GROUP = 128
HALO = max(POOL_WINDOWS)
N_HEADS = 4
HEAD_DIM = 64
V_DIM = 2 * HEAD_DIM
LANES = 128
BF16_SUBLANES = 16
LAMBDA_INIT = 0.8 - 0.6 * math.exp(-0.3 * 0)
NEG_BIG = -1e30
LOG2E = math.log2(math.e)
BIAS_TERMS = 3
SCORE_BOUND_MAX = 60.0
EXP2_ZERO_BELOW = -150.0

ROW_TILE = 512
GATE_CHUNK = 512
FFN_CHUNK = 1024
LATER_STEPS = 16
MERGE_CHUNK = 512
ATTN_TILE = 512
DIAG_UNROLL = 15
FINISH_UNROLL = 16
BELOW_UNROLL_LONG = 24
BELOW_UNROLL = 8
VMEM_LIMIT = 56 * 1024 * 1024


def _dot(a, b):
    return jnp.dot(a, b, preferred_element_type=F32)


def _sigmoid(x):
    return 0.5 * jnp.tanh(0.5 * x) + 0.5


def _split_bf16(a):
    hi = a.astype(BF16)
    return hi, (a - hi.astype(F32)).astype(BF16)


def _inproj_kernel(x_ref, g1_ref, w_ref, gb_ref, gq_ref, gk_ref, pw32_ref, ps_ref, wa32_ref, s2_ref, *rest,
                   widths, scale, n_later, attn_tile, attn_tiles):
    later_f32, rest = rest[:n_later], rest[n_later:]
    u_ref, qt_ref, k_ref, v_ref, g_ref, wpa_ref, plan_ref = rest[:7]
    later_bf16, wbf_ref = rest[7:7 + n_later], rest[7 + n_later]
    pw, aw, vw, gw = widths

    @pl.when(pl.program_id(0) < LATER_STEPS)
    def _():
        for src, dst in zip(later_f32, later_bf16):
            dst[...] = src[...].astype(BF16)

    @pl.when(pl.program_id(0) == 0)
    def _():
        wbf_ref[...] = w_ref[...].astype(BF16)
        for g in range(len(POOL_WINDOWS)):
            rows = slice(g * POOL_GROUP, (g + 1) * POOL_GROUP)
            p_hi, p_lo = _split_bf16(pw32_ref[rows, :] * ps_ref[:, rows])
            a_hi, a_lo = _split_bf16(wa32_ref[rows, :])
            wpa_ref[rows, :] = (_dot(p_hi, a_hi) + _dot(p_hi, a_lo) + _dot(p_lo, a_hi)).astype(BF16)

        bound = ((1.01 * LOG2E * math.sqrt(HEAD_DIM))
                 * jnp.max(jnp.abs(gq_ref[...]), axis=1, keepdims=True)
                 * jnp.max(jnp.abs(gk_ref[...]), axis=1, keepdims=True))
        bounded = bound <= SCORE_BOUND_MAX
        reach = jnp.floor(((bound - EXP2_ZERO_BELOW) / s2_ref[...] - 1.0) / attn_tile) + 1.0
        reach = jnp.clip(jnp.where(bounded, reach, float(attn_tiles)), 0.0, float(attn_tiles))
        lane = lax.broadcasted_iota(jnp.int32, (1, LANES), 1)
        plan_ref[...] = jnp.where(lane == 0, jnp.where(bounded, 1.0, 0.0), reach).astype(jnp.int32)

    x = x_ref[...]
    ms = jnp.mean(x * x, axis=-1, keepdims=True)
    h = (x * lax.rsqrt(ms + EPS) * g1_ref[...]).astype(BF16)

    def proj(lo, width):
        return _dot(h, wbf_ref[:, lo:lo + width])

    def group_rms_t(y):
        yt = y.T
        parts = []
        for g in range(y.shape[1] // HEAD_DIM):
            blk = yt[g * HEAD_DIM:(g + 1) * HEAD_DIM, :]
            parts.append(blk * lax.rsqrt(jnp.mean(blk * blk, axis=0, keepdims=True) + EPS))
        return jnp.concatenate(parts, axis=0)

    qt = (group_rms_t(proj(pw, aw)) * scale).astype(BF16)
    for hd in range(N_HEADS):
        qt_ref[0, hd, 0] = qt[hd * 2 * HEAD_DIM:(hd + 1) * 2 * HEAD_DIM, :]
    gain_row = jnp.concatenate([gq_ref[...] * gk_ref[...]] * (aw // HEAD_DIM), axis=1)
    k_ref[...] = (group_rms_t(proj(pw + aw, aw)).T * gain_row).astype(BF16)
    g0 = pw + 2 * aw + vw
    chunk = GATE_CHUNK
    for c in range(gw // chunk):
        gl = proj(g0 + c * chunk, chunk) + gb_ref[:, c * chunk:(c + 1) * chunk]
        g_ref[:, c * chunk:(c + 1) * chunk] = _sigmoid(gl).astype(BF16)
    v_ref[...] = proj(pw + 2 * aw, vw).astype(BF16)
    u_ref[...] = proj(0, pw)


def _attn_kernel(plan_ref, qt_ref, k_ref, v_ref, qc_ref, kc_ref, lq1_ref, lk1_ref, lq2_ref, lk2_ref,
                 sg_ref, o_ref, p_ref, m_ref, l_ref, acc_ref, *, tile, n_tiles):
    h = pl.program_id(1)
    bounded = plan_ref[0, 0] == 1
    n_reach = plan_ref[0, 1 + h]
    spare = n_tiles

    lane1 = lax.broadcasted_iota(jnp.int32, (1, LANES), 1)
    dist_lane = (lane1 >= 2 * BIAS_TERMS) & (lane1 < 3 * BIAS_TERMS)

    whole = (0, tile)
    qcols_t = qc_ref[0].T

    def scores(qi, j, masked, keys=whole, queries=whole):
        k0, nk = keys
        q0, nq = queries
        qt = qt_ref[0, 0, qi, :, q0:q0 + nq]
        zero = jnp.zeros_like(qt)
        dist = jnp.where(dist_lane, jnp.asarray(j - qi).astype(F32), 0.0).astype(BF16)
        ka = jnp.concatenate([k_ref[0, pl.ds(pl.multiple_of(j * tile, tile) + k0, nk), :],
                              kc_ref[0, k0:k0 + nk, :] + dist], axis=1)
        if masked:
            future = (lax.broadcasted_iota(jnp.int32, (nk, nq), 0) + k0
                      > lax.broadcasted_iota(jnp.int32, (nk, nq), 1) + q0)
        channel = lax.broadcasted_iota(jnp.int32, (2 * HEAD_DIM, nq), 0)
        out = []
        for c in range(2):
            qc = jnp.where(channel >= HEAD_DIM if c else channel < HEAD_DIM, qt, zero)
            qa = jnp.concatenate([qc, qcols_t[:, q0:q0 + nq]], axis=0)
            s = _dot(ka, qa)
            out.append(jnp.where(future, NEG_BIG, s) if masked else s)
        return out

    def values_t_probs(j, p, keys=whole):
        start = pl.multiple_of(j * tile, tile) + keys[0]
        return lax.dot_general(v_ref[0, pl.ds(start, keys[1]), :], p, (((0,), (0,)), ((), ())),
                               preferred_element_type=F32)

    l_ref[spare] = jnp.zeros(l_ref.shape[1:], F32)
    acc_ref[spare] = jnp.zeros(acc_ref.shape[1:], F32)

    @pl.when(bounded)
    def _():
        def probs(item, slot, masked):
            qi, j, state = item
            for c, s in enumerate(scores(qi, j, masked)):
                p = jnp.exp2(s)
                l_ref[state, c] += jnp.sum(p, axis=0, keepdims=True)
                p_ref[slot, c] = p.astype(BF16)

        def accumulate(item, slot):
            _, j, state = item
            for c in range(2):
                acc_ref[state, c] += values_t_probs(j, p_ref[slot, c])

        half = tile // 2
        first_keys, last_keys, last_queries = (0, half), (half, half), (half, half)

        def first_half_probs(i):
            for c, s in enumerate(scores(i, i, True, keys=first_keys)):
                p = jnp.exp2(s)
                l_ref[i, c] = jnp.sum(p, axis=0, keepdims=True)
                p_ref[0, c, 0:half, :] = p.astype(BF16)

        def first_half_accumulate(i):
            for c in range(2):
                acc_ref[i, c] = values_t_probs(i, p_ref[0, c, 0:half, :], keys=first_keys)

        def last_half_probs(i):
            for c, s in enumerate(scores(i, i, True, keys=last_keys, queries=last_queries)):
                p = jnp.exp2(s)
                part = jnp.sum(p, axis=0, keepdims=True)
                l_ref[i, c] += jnp.concatenate([jnp.zeros_like(part), part], axis=1)
                p_ref[1, c, 0:half, 0:half] = p.astype(BF16)

        def last_half_accumulate(i):
            for c in range(2):
                acc_ref[i, c, :, half:] += values_t_probs(i, p_ref[1, c, 0:half, 0:half], keys=last_keys)

        def below(cursor):
            d, qi = cursor
            live = d <= reach
            return jnp.where(live, qi, 0), jnp.where(live, qi - d, 0), jnp.where(live, qi, spare)

        def advance(cursor):
            d, qi = cursor
            wrap = qi + 1 >= n_tiles
            d = jnp.where(wrap, d + 1, d)
            return d, jnp.where(wrap, d, qi + 1)

        reach = jnp.minimum(n_reach, n_tiles - 1)
        n_below = reach * n_tiles - lax.shift_right_logical(reach * (reach + 1), 1)

        first_half_probs(0)

        def diagonal_tile(i):
            last_half_probs(i)
            first_half_accumulate(i)
            first_half_probs(i + 1)
            last_half_accumulate(i)

        def diagonal_tiles(t, carry):
            for u in range(DIAG_UNROLL):
                diagonal_tile(DIAG_UNROLL * t + u)
            return carry

        n_looped = (n_tiles - 1) // DIAG_UNROLL * DIAG_UNROLL
        lax.fori_loop(0, n_looped // DIAG_UNROLL, diagonal_tiles, 0)
        for i in range(n_looped, n_tiles - 1):
            diagonal_tile(i)
        first = (jnp.int32(1), jnp.int32(1))
        last_half_probs(n_tiles - 1)
        first_half_accumulate(n_tiles - 1)
        probs(below(first), 0, False)
        last_half_accumulate(n_tiles - 1)

        def below_group(unroll):
            def body(t, cursor):
                for u in range(unroll):
                    nxt = advance(cursor)
                    probs(below(nxt), (u + 1) % 2, False)
                    accumulate(below(cursor), u % 2)
                    cursor = nxt
                return cursor
            return body

        n_long = n_below // BELOW_UNROLL_LONG
        cursor = lax.fori_loop(0, n_long, below_group(BELOW_UNROLL_LONG), first)
        n_rest = n_below - n_long * BELOW_UNROLL_LONG
        lax.fori_loop(0, (n_rest + BELOW_UNROLL - 1) // BELOW_UNROLL, below_group(BELOW_UNROLL), cursor)

    @pl.when(jnp.logical_not(bounded))
    def _():
        def block(qi, j, masked):
            for c, s in enumerate(scores(qi, j, masked)):
                m_old = m_ref[c]
                m_new = jnp.maximum(m_old, jnp.max(s, axis=0, keepdims=True))
                p = jnp.exp2(s - m_new)
                alpha = jnp.exp2(m_old - m_new)
                l_ref[qi, c] = alpha * l_ref[qi, c] + jnp.sum(p, axis=0, keepdims=True)
                acc_ref[qi, c] = alpha * acc_ref[qi, c] + values_t_probs(j, p.astype(BF16))
                m_ref[c] = m_new

        def query_tile(qi, carry):
            m_ref[...] = jnp.full(m_ref.shape, NEG_BIG, F32)
            l_ref[qi] = jnp.zeros(l_ref.shape[1:], F32)
            acc_ref[qi] = jnp.zeros(acc_ref.shape[1:], F32)
            block(qi, qi, True)

            def body(t, c):
                block(qi, qi - 1 - t, False)
                return c

            lax.fori_loop(0, qi, body, 0)
            return carry

        lax.fori_loop(0, n_tiles, query_tile, 0)

    lam = (jnp.exp(jnp.sum(lq1_ref[...] * lk1_ref[...], axis=-1, keepdims=True))
           - jnp.exp(jnp.sum(lq2_ref[...] * lk2_ref[...], axis=-1, keepdims=True))
           + LAMBDA_INIT)
    diagonal = (lax.broadcasted_iota(jnp.int32, (V_DIM, V_DIM), 0)
                == lax.broadcasted_iota(jnp.int32, (V_DIM, V_DIM), 1))
    gain_col = jnp.sum(jnp.where(diagonal, sg_ref[...], 0.0), axis=1, keepdims=True) * (1.0 - LAMBDA_INIT)

    def finish(qi):
        o = acc_ref[qi, 0] * (1.0 / l_ref[qi, 0]) - acc_ref[qi, 1] * (lam / l_ref[qi, 1])
        msq = jnp.mean(o * o, axis=0, keepdims=True)
        y = o * lax.rsqrt(msq + EPS) * gain_col
        o_ref[0, pl.ds(pl.multiple_of(qi * tile, tile), tile), :] = y.astype(BF16).T

    def finish_group(t, carry):
        for u in range(FINISH_UNROLL):
            finish(FINISH_UNROLL * t + u)
        return carry

    assert n_tiles % FINISH_UNROLL == 0
    lax.fori_loop(0, n_tiles // FINISH_UNROLL, finish_group, 0)


def _merge_ffn_kernel(x_ref, u_ref, halo_ref, a_ref, g_ref, g2_ref,
                      wpa_ref, wb_ref, wo_ref, w1_ref, w2_ref, o_ref, ext_ref, mid_ref,
                      *, seq, tm, chunk, ff_chunk):
    i = pl.program_id(0)
    n_ff = w1_ref.shape[1] // ff_chunk
    t0 = (i * tm) % seq
    halo = halo_ref[...]
    ext_ref[0:HALO, :] = jnp.where(t0 == 0, jnp.zeros_like(halo), halo)
    ext_ref[HALO:, :] = u_ref[...]
    d = o_ref.shape[1]

    for r0 in range(0, tm, chunk):
        rows = slice(r0, r0 + chunk)
        t = t0 + r0 + lax.broadcasted_iota(jnp.int32, (chunk, 1), 0)
        pooled = []
        for g, w in enumerate(POOL_WINDOWS):
            y = ext_ref[r0:r0 + chunk + HALO, g * POOL_GROUP:(g + 1) * POOL_GROUP]
            tot = y
            span = 1
            while span < w:
                tot = tot + pltpu.roll(tot, span, axis=0)
                span *= 2
            cnt = jnp.minimum(t + 1, w).astype(F32)
            pooled.append((tot[HALO:] / cnt - y[HALO:]).astype(BF16))
        ya = _dot(jnp.concatenate(pooled, axis=1), wpa_ref[...])
        yb = _dot(a_ref[rows, :], wb_ref[...])
        merged = g_ref[rows, 0:d].astype(F32) * ya + g_ref[rows, d:2 * d].astype(F32) * yb
        mid_ref[rows, :] = x_ref[rows, :] + _dot(merged.astype(BF16), wo_ref[...])

    x = mid_ref[...]
    ms = jnp.mean(x * x, axis=-1, keepdims=True)
    h = (x * lax.rsqrt(ms + EPS) * g2_ref[...]).astype(BF16)
    acc = x
    for c in range(n_ff):
        cols = slice(c * ff_chunk, (c + 1) * ff_chunk)
        a = jnp.maximum(_dot(h, w1_ref[:, cols]), 0.0)
        acc = acc + _dot((a * a).astype(BF16), w2_ref[cols, :])
    o_ref[...] = acc


def _alibi_columns(tile):
    slopes2 = np.array([2.0 ** (-8.0 * (h + 1) / N_HEADS) for h in range(N_HEADS)]) * LOG2E
    pos = np.arange(tile, dtype=np.float64)

    def terms(x):
        out = []
        for _ in range(BIAS_TERMS):
            part = x.astype(BF16)
            out.append(part)
            x = x - part.astype(np.float64)
        return out

    kcols = np.zeros((N_HEADS, tile, LANES), BF16)
    qcols = np.zeros((N_HEADS, tile, LANES), BF16)
    k_terms = terms(slopes2[:, None] * pos[None, :])
    q_terms = terms(-slopes2[:, None] * pos[None, :])
    t_terms = terms(np.broadcast_to(slopes2[:, None] * tile, (N_HEADS, tile)))
    for t in range(BIAS_TERMS):
        kcols[:, :, t] = k_terms[t]
        kcols[:, :, BIAS_TERMS + t] = 1.0
        qcols[:, :, t] = 1.0
        qcols[:, :, BIAS_TERMS + t] = q_terms[t]
        qcols[:, :, 2 * BIAS_TERMS + t] = t_terms[t]
    return jnp.asarray(qcols), jnp.asarray(kcols), slopes2.astype(np.float32)


def _const_spec(shape):
    return pl.BlockSpec(shape, lambda *_: (0,) * len(shape))


def _resident_spec(shape):
    return pl.BlockSpec(shape, lambda *_: (0,) * len(shape), pipeline_mode=pl.Buffered(1))


def _params(n_axes):
    return pltpu.CompilerParams(dimension_semantics=("arbitrary",) * n_axes,
                                vmem_limit_bytes=VMEM_LIMIT)


def kernel(x, norm1_g, w_in, gate_b, pool_w, pool_scale, q_norm_g, k_norm_g, lambda_q1, lambda_k1,
           lambda_q2, lambda_k2, subln_g, w_branch_a, w_branch_b, w_out, norm2_g, w_ff1, w_ff2):
    B, S, D = x.shape
    depth = norm1_g.shape[0]
    assert depth == 1
    M = B * S
    pw = pool_scale.shape[1]
    aw = 2 * N_HEADS * HEAD_DIM
    vw = N_HEADS * V_DIM
    gw = 2 * D
    d_ff = w_ff1.shape[2]
    tm = ROW_TILE
    ta = ATTN_TILE
    assert S % tm == 0 and S % ta == 0 and tm % HALO == 0
    assert d_ff % FFN_CHUNK == 0
    assert pw == len(POOL_WINDOWS) * POOL_GROUP and w_in.shape[2] == pw + 2 * aw + vw + gw

    x2 = x.reshape(M, D)
    row = lambda width: pl.BlockSpec((tm, width), lambda i: (i, 0))

    assert tm == ta
    n_tiles = S // ta
    steps = M // tm
    later = [w_branch_b[0], w_out[0], w_ff1[0], w_ff2[0]]
    assert steps >= LATER_STEPS and all(w.shape[0] % (LATER_STEPS * BF16_SUBLANES) == 0 for w in later)
    row_slice = lambda w: pl.BlockSpec((w.shape[0] // LATER_STEPS, w.shape[1]),
                                       lambda i: (jnp.minimum(i, LATER_STEPS - 1), 0))
    qcols, kcols, slopes2 = _alibi_columns(ta)
    s2_row = np.ones((1, LANES), np.float32)
    s2_row[0, 1:1 + N_HEADS] = slopes2
    u, qt, k, v, gates, w_pool_a, plan, *later_bf16 = pl.pallas_call(
        functools.partial(_inproj_kernel, widths=(pw, aw, vw, gw), scale=LOG2E / math.sqrt(HEAD_DIM),
                          n_later=len(later), attn_tile=ta, attn_tiles=n_tiles),
        grid=(steps,),
        in_specs=[row(D), _const_spec((1, D)), _resident_spec(w_in.shape[1:]), _const_spec((1, gw)),
                  _const_spec((1, HEAD_DIM)), _const_spec((1, HEAD_DIM)),
                  _const_spec((pw, POOL_GROUP)), _const_spec((1, pw)), _const_spec((pw, D)),
                  _const_spec((1, LANES))]
                 + [row_slice(w) for w in later],
        out_specs=[row(pw),
                   pl.BlockSpec((1, N_HEADS, 1, 2 * HEAD_DIM, ta),
                                lambda i: (i // n_tiles, 0, i % n_tiles, 0, 0)),
                   row(aw), row(vw), row(gw), _const_spec((pw, D)), _const_spec((1, LANES))]
                  + [row_slice(w) for w in later],
        out_shape=[jax.ShapeDtypeStruct((M, pw), F32),
                   jax.ShapeDtypeStruct((B, N_HEADS, n_tiles, 2 * HEAD_DIM, ta), BF16),
                   jax.ShapeDtypeStruct((M, aw), BF16), jax.ShapeDtypeStruct((M, vw), BF16),
                   jax.ShapeDtypeStruct((M, gw), BF16), jax.ShapeDtypeStruct((pw, D), BF16),
                   jax.ShapeDtypeStruct((1, LANES), jnp.int32)]
                  + [jax.ShapeDtypeStruct(w.shape, BF16) for w in later],
        scratch_shapes=[pltpu.VMEM(w_in.shape[1:], BF16)],
        compiler_params=_params(1),
        name="inproj",
    )(x2, norm1_g, w_in[0], gate_b, q_norm_g, k_norm_g,
      pool_w[0].reshape(pw, POOL_GROUP), pool_scale, w_branch_a[0], jnp.asarray(s2_row), *later)

    head_seq = pl.BlockSpec((1, S, LANES), lambda b, h: (b, 0, h))
    attn = pl.pallas_call(
        functools.partial(_attn_kernel, tile=ta, n_tiles=n_tiles),
        grid=(B, N_HEADS),
        in_specs=[pl.BlockSpec(memory_space=pltpu.SMEM),
                  pl.BlockSpec((1, 1, n_tiles, 2 * HEAD_DIM, ta), lambda b, h: (b, h, 0, 0, 0)),
                  head_seq, head_seq,
                  pl.BlockSpec((1, ta, LANES), lambda b, h: (h, 0, 0)),
                  pl.BlockSpec((1, ta, LANES), lambda b, h: (h, 0, 0)),
                  _const_spec((1, HEAD_DIM)), _const_spec((1, HEAD_DIM)),
                  _const_spec((1, HEAD_DIM)), _const_spec((1, HEAD_DIM)), _const_spec((1, V_DIM))],
        out_specs=head_seq,
        out_shape=jax.ShapeDtypeStruct((B, S, vw), BF16),
        scratch_shapes=[pltpu.VMEM((2, 2, ta, ta), BF16),
                        pltpu.VMEM((2, 1, ta), F32),
                        pltpu.VMEM((n_tiles + 1, 2, 1, ta), F32),
                        pltpu.VMEM((n_tiles + 1, 2, V_DIM, ta), F32)],
        compiler_params=_params(2),
        name="diffattn",
    )(plan, qt, k.reshape(B, S, aw), v.reshape(B, S, vw), qcols, kcols,
      lambda_q1, lambda_k1, lambda_q2, lambda_k2, subln_g)

    out = pl.pallas_call(
        functools.partial(_merge_ffn_kernel, seq=S, tm=tm, chunk=MERGE_CHUNK, ff_chunk=FFN_CHUNK),
        grid=(M // tm,),
        in_specs=[row(D), row(pw),
                  pl.BlockSpec((HALO, pw), lambda i: (jnp.maximum(i * (tm // HALO) - 1, 0), 0)),
                  row(vw), row(gw), _const_spec((1, D))]
                 + [_resident_spec(w.shape) for w in [w_pool_a] + later_bf16],
        out_specs=row(D),
        out_shape=jax.ShapeDtypeStruct((M, D), F32),
        scratch_shapes=[pltpu.VMEM((HALO + tm, pw), F32),
                        pltpu.VMEM((tm, D), F32)],
        compiler_params=_params(1),
        name="merge_ffn",
    )(x2, u, u, attn.reshape(M, vw), gates, norm2_g, w_pool_a, *later_bf16)
    return out.reshape(B, S, D)
```
